```python
import math
import jax
import jax.numpy as jnp
from jax import lax
import numpy as np

D_MODEL = 1024
BATCH = 4
SEQ = 8192
DEPTH = 4

MIX_WIDTH = D_MODEL
SSD_HEADS = 8
SSD_HEAD_DIM = 64
SSD_INNER = SSD_HEADS * SSD_HEAD_DIM
SSD_GROUPS = 2
SSD_STATE = 128
SSD_CONV = 4
SSD_CHUNK = 256
SSD_XBC = SSD_INNER + 2 * SSD_GROUPS * SSD_STATE
SC_HEADS = 4
SC_HEAD_DIM = 64
SC_WIDTH = SC_HEADS * SC_HEAD_DIM
SC_CONV = 3
NSA_HEADS = 4
NSA_HEAD_DIM = 64
NSA_WIDTH = NSA_HEADS * NSA_HEAD_DIM
NSA_KV = NSA_HEAD_DIM
CMP_BLOCK = 32
CMP_STRIDE = 16
CMP_HIDDEN = 128
SEL_BLOCK = 64
SEL_TOPK = 16
SEL_LOCAL = 2
WINDOW = 512
Q_BLOCK = 128
ROPE_THETA = 10000.0
D_FF = 2752
FFN_CONV = 3
EPS = 1e-6
NEG = -1e30
FORCE = 1e9
IN_SIZES = (SSD_INNER, SSD_XBC, SSD_HEADS, SC_WIDTH, SC_WIDTH, SC_WIDTH, NSA_WIDTH,
            NSA_KV, NSA_KV, NSA_KV, NSA_KV, NSA_KV, NSA_KV, 3 * NSA_HEADS)
N_IN = sum(IN_SIZES)

kernel_name = 'hymba_ssd_shortconv_nsa_trunk'


def rmsnorm(x, w):
    xf = x.astype(jnp.float32)
    y = xf * lax.rsqrt(jnp.mean(xf * xf, axis=-1, keepdims=True) + EPS)
    return (y * w.astype(jnp.float32)).astype(x.dtype)


def causal_dwconv(u, w, b=None):
    width = w.shape[0]
    s_len = u.shape[1]
    up = jnp.pad(u, ((0, 0), (width - 1, 0), (0, 0)))
    out = up[:, 0:s_len] * w[0]
    for j in range(1, width):
        out = out + up[:, j:j + s_len] * w[j]
    if b is not None:
        out = out + b
    return out


def rope_tables(s_len):
    half = NSA_HEAD_DIM // 2
    inv = 1.0 / (ROPE_THETA ** (jnp.arange(half, dtype=jnp.float32) / half))
    ang = jnp.arange(s_len, dtype=jnp.float32)[:, None] * inv[None, :]
    return jnp.cos(ang), jnp.sin(ang)


def apply_rope(x, cos, sin):
    x1, x2 = jnp.split(x, 2, axis=-1)
    c = cos[None, :, None, :].astype(x.dtype)
    s = sin[None, :, None, :].astype(x.dtype)
    return jnp.concatenate([x1 * c - x2 * s, x2 * c + x1 * s], axis=-1)


def masked_softmax(s, mask):
    s = jnp.where(mask, s.astype(jnp.float32), NEG)
    m = jnp.max(s, axis=-1, keepdims=True)
    p = jnp.exp(s - m) * mask
    return p / jnp.maximum(jnp.sum(p, axis=-1, keepdims=True), 1e-20)


def ssd_mixer(z, xbc, dt_raw, conv_w, conv_b, dt_bias, a_log, d_skip, norm_w):
    f32 = jnp.float32
    bsz, s_len, _ = z.shape
    n_k = SSD_HEADS // SSD_GROUPS
    xbc = jax.nn.silu(causal_dwconv(xbc, conv_w, conv_b))
    xs, b_in, c_in = jnp.split(xbc, [SSD_INNER, SSD_INNER + SSD_GROUPS * SSD_STATE], axis=-1)
    dt = jax.nn.softplus((dt_raw + dt_bias).astype(f32))
    a_neg = -jnp.exp(a_log.astype(f32)).reshape(SSD_GROUPS, n_k)
    pad = (-s_len) % SSD_CHUNK
    s_pad = s_len + pad
    n_c = s_pad // SSD_CHUNK

    def chunked(t, *tail):
        t = jnp.pad(t.astype(f32), ((0, 0), (0, pad), (0, 0)))
        return t.reshape(bsz, n_c, SSD_CHUNK, *tail)

    x_c = chunked(xs, SSD_GROUPS, n_k, SSD_HEAD_DIM)
    dt_c = chunked(dt, SSD_GROUPS, n_k)
    b_c = chunked(b_in, SSD_GROUPS, SSD_STATE)
    c_c = chunked(c_in, SSD_GROUPS, SSD_STATE)
    xdt = x_c * dt_c[..., None]
    a = jnp.transpose(dt_c * a_neg, (0, 3, 4, 1, 2))
    a_cs = jnp.cumsum(a, axis=-1)
    causal = jnp.tril(jnp.ones((SSD_CHUNK, SSD_CHUNK), dtype=bool))
    seg = a_cs[..., :, None] - a_cs[..., None, :]
    decay = jnp.exp(jnp.where(causal, seg, -jnp.inf))
    cb = jnp.einsum('bclgn,bcsgn->bgcls', c_c, b_c)
    y_diag = jnp.einsum('bgkcls,bcsgkp->bclgkp', decay * cb[:, :, None], xdt)
    decay_to_end = jnp.exp(a_cs[..., -1:] - a_cs)
    chunk_states = jnp.einsum('bclgn,bgkcl,bclgkp->cbgkpn', b_c, decay_to_end, xdt)
    chunk_decay = jnp.moveaxis(jnp.exp(a_cs[..., -1]), -1, 0)

    def step(h, inp):
        s_new, d_c = inp
        return h * d_c[..., None, None] + s_new, h

    h0 = jnp.zeros((bsz, SSD_GROUPS, n_k, SSD_HEAD_DIM, SSD_STATE), f32)
    _, prev_states = lax.scan(step, h0, (chunk_states, chunk_decay))
    y_off = jnp.einsum('bclgn,cbgkpn,bgkcl->bclgkp', c_c, prev_states, jnp.exp(a_cs))
    y = y_diag + y_off + x_c * d_skip.astype(f32).reshape(SSD_GROUPS, n_k)[:, :, None]
    y = y.reshape(bsz, s_pad, SSD_INNER)[:, :s_len]
    y = y * jax.nn.silu(z.astype(f32))
    return rmsnorm(y, norm_w).astype(z.dtype)


def compress_block(t, pos, w1, w2):
    bsz, s_len, dk = t.shape
    n_cmp = (s_len - CMP_BLOCK) // CMP_STRIDE + 1
    idx = np.arange(n_cmp)[:, None] * CMP_STRIDE + np.arange(CMP_BLOCK)[None, :]
    blocks = (t[:, idx] + pos).reshape(bsz, n_cmp, CMP_BLOCK * dk)
    return jax.nn.gelu(blocks @ w1) @ w2


def nsa_mixer(q, k_cmp, v_cmp, k_sel, v_sel, k_win, v_win, gate_logits, cos, sin,
              kpos, kw1, kw2, vpos, vw1, vw2):
    bsz, s_len, _ = q.shape
    dk = NSA_HEAD_DIM
    q = apply_rope(q.reshape(bsz, s_len, NSA_HEADS, dk), cos, sin) * (dk ** -0.5)

    def rope_k(t):
        return apply_rope(t[:, :, None, :], cos, sin)[:, :, 0, :]

    kc = compress_block(rope_k(k_cmp), kpos, kw1, kw2)
    vc = compress_block(v_cmp, vpos, vw1, vw2)
    n_cmp = kc.shape[1]
    n_slc = s_len // SEL_BLOCK
    top = min(SEL_TOPK, n_slc)
    cmp_start = np.arange(n_cmp) * CMP_STRIDE
    slc_start = np.arange(n_slc) * SEL_BLOCK
    cmp_end = jnp.asarray(cmp_start + CMP_BLOCK - 1, jnp.int32)
    overlap = jnp.asarray(((cmp_start[:, None] < slc_start[None, :] + SEL_BLOCK)
                           & (cmp_start[:, None] + CMP_BLOCK > slc_start[None, :])).astype(np.float32))
    ks_blocks = rope_k(k_sel).reshape(bsz, n_slc, SEL_BLOCK, dk)
    vs_blocks = v_sel.reshape(bsz, n_slc, SEL_BLOCK, dk)
    win_pad = ((0, 0), (WINDOW, 0), (0, 0))
    kw = jnp.pad(rope_k(k_win), win_pad)
    vw = jnp.pad(v_win, win_pad)
    gates = jax.nn.sigmoid(gate_logits.astype(jnp.float32)).reshape(bsz, s_len, NSA_HEADS, 3)
    n_qb = s_len // Q_BLOCK

    def to_blocks(t):
        return jnp.moveaxis(t.reshape(bsz, n_qb, Q_BLOCK, *t.shape[2:]), 1, 0)

    blk = jnp.arange(n_slc)
    offs = jnp.arange(SEL_BLOCK)
    gather = jax.vmap(lambda blocks, idx: blocks[idx])

    def block_fn(args):
        qb, gb, qi = args
        t = qi * Q_BLOCK + jnp.arange(Q_BLOCK)
        p_c = masked_softmax(jnp.einsum('bqhd,bnd->bhqn', qb, kc), cmp_end[None, :] <= t[:, None])
        o_c = jnp.einsum('bhqn,bnd->bqhd', p_c, vc)
        imp = jnp.einsum('bhqn,nj->bqj', p_c, overlap)
        cur = (t // SEL_BLOCK)[:, None]
        valid = blk[None, :] <= cur
        forced = (blk[None, :] == 0) | (valid & (blk[None, :] > cur - SEL_LOCAL))
        imp = jnp.where(forced, FORCE, jnp.where(valid, imp, NEG))
        _, sel = lax.top_k(imp, top)
        ks = gather(ks_blocks, sel).reshape(bsz, Q_BLOCK, top * SEL_BLOCK, dk)
        vs = gather(vs_blocks, sel).reshape(bsz, Q_BLOCK, top * SEL_BLOCK, dk)
        tok = (sel[..., None] * SEL_BLOCK + offs).reshape(bsz, Q_BLOCK, top * SEL_BLOCK)
        m_s = (tok <= t[None, :, None])[:, None]
        p_s = masked_softmax(jnp.einsum('bqhd,bqnd->bhqn', qb, ks), m_s)
        o_s = jnp.einsum('bhqn,bqnd->bqhd', p_s, vs)
        start = qi * Q_BLOCK
        kwb = lax.dynamic_slice_in_dim(kw, start, Q_BLOCK + WINDOW, axis=1)
        vwb = lax.dynamic_slice_in_dim(vw, start, Q_BLOCK + WINDOW, axis=1)
        kp = start - WINDOW + jnp.arange(Q_BLOCK + WINDOW)
        m_w = (kp[None, :] >= 0) & (kp[None, :] <= t[:, None]) & (kp[None, :] > t[:, None] - WINDOW)
        p_w = masked_softmax(jnp.einsum('bqhd,bkd->bhqk', qb, kwb), m_w)
        o_w = jnp.einsum('bhqk,bkd->bqhd', p_w, vwb)
        return o_c * gb[..., 0:1] + o_s * gb[..., 1:2] + o_w * gb[..., 2:3]

    out = lax.map(block_fn, (to_blocks(q), to_blocks(gates), jnp.arange(n_qb)))
    return jnp.moveaxis(out, 0, 1).reshape(bsz, s_len, NSA_WIDTH).astype(q.dtype)


def setup_inputs(seed: int = 0) -> dict:
    key = jax.random.key(seed)
    ks = jax.random.split(key, 24)
    f32 = jnp.float32
    nl = DEPTH
    cmp_in = CMP_BLOCK * NSA_HEAD_DIM

    def nrm(k, shape, scale):
        return jax.random.normal(k, shape, f32) * scale

    dt0 = jnp.exp(jax.random.uniform(ks[5], (nl, SSD_HEADS), f32, math.log(1e-3), math.log(1e-1)))
    return {
        'x': nrm(ks[0], (BATCH, SEQ, D_MODEL), 1.0),
        'attn_norm_w': 1.0 + nrm(ks[1], (nl, D_MODEL), 0.02),
        'w_in': nrm(ks[2], (nl, D_MODEL, N_IN), D_MODEL ** -0.5),
        'ssd_conv_w': nrm(ks[3], (nl, SSD_CONV, SSD_XBC), SSD_CONV ** -0.5),
        'ssd_conv_b': nrm(ks[4], (nl, SSD_XBC), 0.02),
        'ssd_dt_bias': dt0 + jnp.log(-jnp.expm1(-dt0)),
        'ssd_a_log': jnp.log(jax.random.uniform(ks[6], (nl, SSD_HEADS), f32, 1.0, 16.0)),
        'ssd_d': 1.0 + nrm(ks[7], (nl, SSD_HEADS), 0.02),
        'ssd_norm_w': 1.0 + nrm(ks[8], (nl, SSD_INNER), 0.02),
        'sc_conv_w': nrm(ks[9], (nl, SC_CONV, SC_WIDTH), SC_CONV ** -0.5),
        'cmp_k_pos': nrm(ks[10], (nl, CMP_BLOCK, NSA_HEAD_DIM), 0.02),
        'cmp_k_w1': nrm(ks[11], (nl, cmp_in, CMP_HIDDEN), cmp_in ** -0.5),
        'cmp_k_w2': nrm(ks[12], (nl, CMP_HIDDEN, NSA_HEAD_DIM), CMP_HIDDEN ** -0.5),
        'cmp_v_pos': nrm(ks[13], (nl, CMP_BLOCK, NSA_HEAD_DIM), 0.02),
        'cmp_v_w1': nrm(ks[14], (nl, cmp_in, CMP_HIDDEN), cmp_in ** -0.5),
        'cmp_v_w2': nrm(ks[15], (nl, CMP_HIDDEN, NSA_HEAD_DIM), CMP_HIDDEN ** -0.5),
        'w_out': nrm(ks[16], (nl, MIX_WIDTH, D_MODEL), MIX_WIDTH ** -0.5),
        'ffn_norm_w': 1.0 + nrm(ks[17], (nl, D_MODEL), 0.02),
        'ffn_w_up': nrm(ks[18], (nl, D_MODEL, 2 * D_FF), D_MODEL ** -0.5),
        'ffn_conv_w': nrm(ks[19], (nl, FFN_CONV, 2 * D_FF), FFN_CONV ** -0.5),
        'ffn_conv_b': nrm(ks[20], (nl, 2 * D_FF), 0.02),
        'ffn_w_down': nrm(ks[21], (nl, D_FF, D_MODEL), D_FF ** -0.5),
        'final_norm_w': 1.0 + nrm(ks[22], (D_MODEL,), 0.02),
    }


def reference(x, attn_norm_w, w_in, ssd_conv_w, ssd_conv_b, ssd_dt_bias, ssd_a_log, ssd_d,
              ssd_norm_w, sc_conv_w, cmp_k_pos, cmp_k_w1, cmp_k_w2, cmp_v_pos, cmp_v_w1, cmp_v_w2,
              w_out, ffn_norm_w, ffn_w_up, ffn_conv_w, ffn_conv_b, ffn_w_down, final_norm_w):
    s_len = x.shape[1]
    cos, sin = rope_tables(s_len)
    split_at = np.cumsum(np.array(IN_SIZES))[:-1].tolist()
    for l in range(DEPTH):
        h = rmsnorm(x, attn_norm_w[l])
        proj = h @ w_in[l]
        (z, xbc, dt_raw, sc_b, sc_c, sc_h, q, k_c, v_c, k_s, v_s, k_w, v_w,
         g_nsa) = jnp.split(proj, split_at, axis=-1)
        y_ssd = ssd_mixer(z, xbc, dt_raw, ssd_conv_w[l], ssd_conv_b[l], ssd_dt_bias[l],
                          ssd_a_log[l], ssd_d[l], ssd_norm_w[l])
        y_sc = sc_b * causal_dwconv(sc_c * sc_h, sc_conv_w[l])
        y_nsa = nsa_mixer(q, k_c, v_c, k_s, v_s, k_w, v_w, g_nsa, cos, sin,
                          cmp_k_pos[l], cmp_k_w1[l], cmp_k_w2[l],
                          cmp_v_pos[l], cmp_v_w1[l], cmp_v_w2[l])
        x = x + jnp.concatenate([y_ssd, y_sc, y_nsa], axis=-1) @ w_out[l]
        h = rmsnorm(x, ffn_norm_w[l])
        u = causal_dwconv(h @ ffn_w_up[l], ffn_conv_w[l], ffn_conv_b[l])
        gate, val = jnp.split(u, 2, axis=-1)
        x = x + (jax.nn.silu(gate) * val) @ ffn_w_down[l]
    return rmsnorm(x, final_norm_w)
```

```python
import functools
import math

import numpy as np
import jax
import jax.numpy as jnp
from jax import lax
from jax.experimental import pallas as pl
from jax.experimental.pallas import tpu as pltpu

F32 = jnp.float32
BF16 = jnp.bfloat16

D_MODEL = 1024
SSD_HEADS = 8
SSD_HEAD_DIM = 64
SSD_INNER = SSD_HEADS * SSD_HEAD_DIM
SSD_GROUPS = 2
SSD_STATE = 128
SSD_CONV = 4
SSD_CHUNK = 256
SSD_XBC = SSD_INNER + 2 * SSD_GROUPS * SSD_STATE
SC_WIDTH = 256
SC_CONV = 3
NSA_HEADS = 4
NSA_HEAD_DIM = 64
NSA_WIDTH = NSA_HEADS * NSA_HEAD_DIM
CMP_BLOCK = 32
CMP_STRIDE = 16
CMP_HIDDEN = 128
SEL_BLOCK = 64
SEL_TOPK = 16
SEL_LOCAL = 2
WINDOW = 512
ROPE_THETA = 10000.0
D_FF = 2752
FFN_CONV = 3
EPS = 1e-6
NEG = -1e30
FORCE = 1e9
REMOVED = -3.0e38

LANES = 128
SUBLANES = 8
VMEM_LIMIT_BYTES = 56 * 1024 * 1024

D_FF_PAD = 2816
FF_TILE = 256
N_FF_TILES = D_FF_PAD // FF_TILE

COL_Z = 0
COL_XBC = COL_Z + SSD_INNER
COL_SC = COL_XBC + SSD_XBC
COL_Q = COL_SC + 3 * SC_WIDTH
COL_KV = COL_Q + NSA_WIDTH
COL_DTG = COL_KV + 6 * NSA_HEAD_DIM
N_PACK = COL_DTG + LANES
GATE_COL = SSD_HEADS

INPROJ_TM = 512
NSA_TQ = 256
NSA_TK = 512
NSA_WT = 256
FFN_TM = 256


def _dot(a, b):
    return jnp.dot(a, b, preferred_element_type=F32)


def _dot_nt(a, b):
    return lax.dot_general(a, b, (((1,), (1,)), ((), ())), preferred_element_type=F32)


def _sigmoid(x):
    return 1.0 / (1.0 + jnp.exp(-x))


def _softplus(x):
    return jnp.maximum(x, 0.0) + jnp.log1p(jnp.exp(-jnp.abs(x)))


def _split3(a):
    a1 = a.astype(BF16)
    r1 = a - a1.astype(F32)
    a2 = r1.astype(BF16)
    r2 = r1 - a2.astype(F32)
    return a1, a2, r2.astype(BF16)


def _shift_rows(cur, tail, k):
    if k == 0:
        return cur
    rc = pltpu.roll(cur, k, 0)
    rt = pltpu.roll(tail, k, 0)
    row = lax.broadcasted_iota(jnp.int32, tail.shape, 0)
    first = jnp.where(row < k, rt, rc[0:SUBLANES])
    return jnp.concatenate([first, rc[SUBLANES:]], axis=0)


def _inproj_kernel(x_ref, nw_ref, w_ref, rqc_ref, rqs_ref, rkc_ref, rks_ref,
                   z_ref, xbc_ref, sc_ref, q_ref, cpair_ref, ks_ref, kw_ref, vst_ref, vwt_ref, dtg_ref):
    tm = x_ref.shape[0]
    x = x_ref[...]
    ms = jnp.mean(x * x, axis=-1, keepdims=True)
    h = (x * lax.rsqrt(ms + EPS) * nw_ref[...]).astype(BF16)

    def proj(a, b):
        return _dot(h, w_ref[:, a:b])

    z_ref[...] = proj(COL_Z, COL_XBC)
    xbc_ref[...] = proj(COL_XBC, COL_SC)
    sc_ref[...] = proj(COL_SC, COL_Q)
    dtg_ref[...] = proj(COL_DTG, N_PACK)

    lane = lax.broadcasted_iota(jnp.int32, (tm, LANES), 1)
    first_half = (lane % NSA_HEAD_DIM) < (NSA_HEAD_DIM // 2)

    def rope(v, c, s):
        partner = jnp.where(first_half, pltpu.roll(v, LANES - 32, 1), pltpu.roll(v, 32, 1))
        return v * c + partner * s

    rqc, rqs, rkc, rks = rqc_ref[...], rqs_ref[...], rkc_ref[...], rks_ref[...]
    q = proj(COL_Q, COL_KV)
    q_ref[:, 0:LANES] = rope(q[:, 0:LANES], rqc, rqs)
    q_ref[:, LANES:2 * LANES] = rope(q[:, LANES:2 * LANES], rqc, rqs)

    kv = proj(COL_KV, COL_DTG)
    cpair_ref[...] = rope(kv[:, 0:LANES], rkc, rks)
    for pair, k_ref, vt_ref in ((1, ks_ref, vst_ref), (2, kw_ref, vwt_ref)):
        p = rope(kv[:, pair * LANES:(pair + 1) * LANES], rkc, rks)
        k_ref[...] = p[:, 0:NSA_HEAD_DIM].astype(BF16)
        pt = p.T
        for j in range(tm // LANES):
            vt_ref[j] = pt[NSA_HEAD_DIM:LANES, j * LANES:(j + 1) * LANES].astype(BF16)


def _inproj(x2d, nw, w_pack, ropes, seq):
    t = x2d.shape[0]
    tm = INPROJ_TM
    nt = t // tm
    pos_blocks = seq // tm
    row = lambda i: (i, 0)
    pos = lambda i: (i % pos_blocks, 0)
    const = lambda i: (0, 0)
    out_shapes = (
        jax.ShapeDtypeStruct((t, SSD_INNER), F32),
        jax.ShapeDtypeStruct((t, SSD_XBC), F32),
        jax.ShapeDtypeStruct((t, 3 * SC_WIDTH), F32),
        jax.ShapeDtypeStruct((t, NSA_WIDTH), F32),
        jax.ShapeDtypeStruct((t, LANES), F32),
        jax.ShapeDtypeStruct((t, NSA_HEAD_DIM), BF16),
        jax.ShapeDtypeStruct((t, NSA_HEAD_DIM), BF16),
        jax.ShapeDtypeStruct((t // LANES, NSA_HEAD_DIM, LANES), BF16),
        jax.ShapeDtypeStruct((t // LANES, NSA_HEAD_DIM, LANES), BF16),
        jax.ShapeDtypeStruct((t, LANES), F32),
    )
    vt_spec = pl.BlockSpec((tm // LANES, NSA_HEAD_DIM, LANES), lambda i: (i, 0, 0))
    out_specs = (
        pl.BlockSpec((tm, SSD_INNER), row),
        pl.BlockSpec((tm, SSD_XBC), row),
        pl.BlockSpec((tm, 3 * SC_WIDTH), row),
        pl.BlockSpec((tm, NSA_WIDTH), row),
        pl.BlockSpec((tm, LANES), row),
        pl.BlockSpec((tm, NSA_HEAD_DIM), row),
        pl.BlockSpec((tm, NSA_HEAD_DIM), row),
        vt_spec,
        vt_spec,
        pl.BlockSpec((tm, LANES), row),
    )
    in_specs = [
        pl.BlockSpec((tm, D_MODEL), row),
        pl.BlockSpec((1, D_MODEL), const),
        pl.BlockSpec((D_MODEL, N_PACK), const),
        pl.BlockSpec((tm, LANES), pos),
        pl.BlockSpec((tm, LANES), pos),
        pl.BlockSpec((tm, LANES), pos),
        pl.BlockSpec((tm, LANES), pos),
    ]
    return pl.pallas_call(
        _inproj_kernel,
        grid=(nt,),
        in_specs=in_specs,
        out_specs=out_specs,
        out_shape=out_shapes,
        compiler_params=pltpu.CompilerParams(
            dimension_semantics=("arbitrary",), vmem_limit_bytes=VMEM_LIMIT_BYTES),
        name="inproj",
    )(x2d, nw, w_pack, *ropes)


def _compress_kernel(x2_ref, pos_ref, wc_ref, w2_ref, kc_ref, vct_ref):
    nc = x2_ref.shape[1]
    x2 = x2_ref[0].astype(BF16)
    wc = wc_ref[...]
    y = _dot(x2, wc)
    r = _dot(pos_ref[...].astype(BF16), wc)
    h = CMP_HIDDEN

    def pre(base):
        bias = r[0:1, base:base + h] + r[1:2, base + h:base + 2 * h]
        return y[:, base:base + h] + pltpu.roll(y[:, base + h:base + 2 * h], nc - 1, 0) + bias

    hid = jnp.concatenate([jax.nn.gelu(pre(0)), jax.nn.gelu(pre(2 * h))], axis=1).astype(BF16)
    o = _dot(hid, w2_ref[...])
    kc_ref[0] = o[:, 0:NSA_HEAD_DIM].astype(BF16)
    vct_ref[0] = o.T[NSA_HEAD_DIM:LANES, :].astype(BF16)


def _compress(x2p, pos2, wc, w2bd):
    b, nc, width = x2p.shape
    return pl.pallas_call(
        _compress_kernel,
        grid=(b,),
        in_specs=[
            pl.BlockSpec((1, nc, width), lambda i: (i, 0, 0)),
            pl.BlockSpec(pos2.shape, lambda i: (0, 0)),
            pl.BlockSpec(wc.shape, lambda i: (0, 0)),
            pl.BlockSpec(w2bd.shape, lambda i: (0, 0)),
        ],
        out_specs=(
            pl.BlockSpec((1, nc, NSA_HEAD_DIM), lambda i: (i, 0, 0)),
            pl.BlockSpec((1, NSA_HEAD_DIM, nc), lambda i: (i, 0, 0)),
        ),
        out_shape=(
            jax.ShapeDtypeStruct((b, nc, NSA_HEAD_DIM), BF16),
            jax.ShapeDtypeStruct((b, NSA_HEAD_DIM, nc), BF16),
        ),
        compiler_params=pltpu.CompilerParams(
            dimension_semantics=("arbitrary",), vmem_limit_bytes=VMEM_LIMIT_BYTES),
        name="compress",
    )(x2p, pos2, wc, w2bd)


def _ssd_kernel(z_ref, xbc_ref, sc_ref, dtg_ref, cw_ref, cb_ref, dtb_ref, aneg_ref, dsk_ref, nw_ref, scw_ref,
                y_ref, h_ref, xtail_ref, stail_ref):
    L = SSD_CHUNK
    half = SSD_HEAD_DIM

    @pl.when(pl.program_id(1) == 0)
    def _():
        h_ref[...] = jnp.zeros_like(h_ref)
        xtail_ref[...] = jnp.zeros_like(xtail_ref)
        stail_ref[...] = jnp.zeros_like(stail_ref)

    xraw = xbc_ref[0]
    xtail = xtail_ref[...]
    conv = cb_ref[...] + cw_ref[SSD_CONV - 1:SSD_CONV, :] * xraw
    for j in range(SSD_CONV - 1):
        conv = conv + cw_ref[j:j + 1, :] * _shift_rows(xraw, xtail, SSD_CONV - 1 - j)
    xtail_ref[...] = xraw[L - SUBLANES:L]
    xc = conv * _sigmoid(conv)

    dt = _softplus(dtg_ref[0] + dtb_ref[...])
    a = dt * aneg_ref[...]
    row = lax.broadcasted_iota(jnp.int32, (L, L), 0)
    col = lax.broadcasted_iota(jnp.int32, (L, L), 1)
    causal = row >= col
    tri = jnp.where(causal, 1.0, 0.0).astype(BF16)
    a1, a2, a3 = _split3(a)
    acs = _dot(tri, a1) + _dot(tri, a2) + _dot(tri, a3)
    acs_t = acs.T
    exp_acs = jnp.exp(acs)
    last = acs[L - 1:L, :]
    dte = jnp.exp(last - acs)
    chunk_decay = jnp.exp(last)

    lo = lax.broadcasted_iota(jnp.int32, (L, LANES), 1) < half
    lo_row = lax.broadcasted_iota(jnp.int32, (1, LANES), 1) < half

    def per_lane(m, ha, hb, mask):
        return jnp.where(mask, m[:, ha:ha + 1], m[:, hb:hb + 1])

    ys = []
    for g in range(SSD_GROUPS):
        b_g = xc[:, SSD_INNER + g * SSD_STATE:SSD_INNER + (g + 1) * SSD_STATE]
        c_g = xc[:, SSD_INNER + (SSD_GROUPS + g) * SSD_STATE:SSD_INNER + (SSD_GROUPS + g + 1) * SSD_STATE]
        b_bf = b_g.astype(BF16)
        c_bf = c_g.astype(BF16)
        cb = _dot_nt(c_bf, b_bf)
        bt_bf = b_g.T.astype(BF16)
        for pp in range(SSD_HEADS // SSD_GROUPS // 2):
            p = g * (SSD_HEADS // SSD_GROUPS // 2) + pp
            ha, hb = 2 * p, 2 * p + 1
            x_pair = xc[:, p * LANES:(p + 1) * LANES]
            xdt = x_pair * per_lane(dt, ha, hb, lo)
            xdt_bf = xdt.astype(BF16)
            yd = []
            for hd in (ha, hb):
                seg = acs[:, hd:hd + 1] - acs_t[hd:hd + 1, :]
                decay = jnp.exp(jnp.where(causal, seg, NEG))
                yd.append(_dot((decay * cb).astype(BF16), xdt_bf))
            y_diag = jnp.where(lo, yd[0], yd[1])
            st = _dot(bt_bf, (xdt * per_lane(dte, ha, hb, lo)).astype(BF16))
            h_prev = h_ref[p]
            y_off = _dot(c_bf, h_prev.astype(BF16)) * per_lane(exp_acs, ha, hb, lo)
            h_ref[p] = h_prev * per_lane(chunk_decay, ha, hb, lo_row) + st
            ys.append(y_diag + y_off + x_pair * dsk_ref[:, p * LANES:(p + 1) * LANES])
    y = jnp.concatenate(ys, axis=1)
    z = z_ref[0]
    y = y * (z * _sigmoid(z))
    ms = jnp.mean(y * y, axis=-1, keepdims=True)
    y_ref[0, :, 0:SSD_INNER] = y * lax.rsqrt(ms + EPS) * nw_ref[...]

    sc = sc_ref[0]
    u = sc[:, SC_WIDTH:2 * SC_WIDTH] * sc[:, 2 * SC_WIDTH:3 * SC_WIDTH]
    stail = stail_ref[...]
    cv = scw_ref[SC_CONV - 1:SC_CONV, :] * u
    for j in range(SC_CONV - 1):
        cv = cv + scw_ref[j:j + 1, :] * _shift_rows(u, stail, SC_CONV - 1 - j)
    stail_ref[...] = u[L - SUBLANES:L]
    y_ref[0, :, SSD_INNER:SSD_INNER + SC_WIDTH] = sc[:, 0:SC_WIDTH] * cv


def _ssd(z, xbc, sc, dtg, cw, cb, dtb, aneg, dsk, nw, scw):
    b, s, _ = z.shape
    L = SSD_CHUNK
    blk = lambda w: pl.BlockSpec((1, L, w), lambda i, c: (i, c, 0))
    par = lambda a: pl.BlockSpec(a.shape, lambda i, c: (0, 0))
    return pl.pallas_call(
        _ssd_kernel,
        grid=(b, s // L),
        in_specs=[blk(SSD_INNER), blk(SSD_XBC), blk(3 * SC_WIDTH), blk(LANES),
                  par(cw), par(cb), par(dtb), par(aneg), par(dsk), par(nw), par(scw)],
        out_specs=blk(SSD_INNER + SC_WIDTH),
        out_shape=jax.ShapeDtypeStruct((b, s, SSD_INNER + SC_WIDTH), F32),
        scratch_shapes=[
            pltpu.VMEM((SSD_HEADS // 2, SSD_STATE, LANES), F32),
            pltpu.VMEM((SUBLANES, SSD_XBC), F32),
            pltpu.VMEM((SUBLANES, SC_WIDTH), F32),
        ],
        compiler_params=pltpu.CompilerParams(
            dimension_semantics=("arbitrary", "arbitrary"), vmem_limit_bytes=VMEM_LIMIT_BYTES),
        name="ssd_sc",
    )(z, xbc, sc, dtg, cw, cb, dtb, aneg, dsk, nw, scw)


def _nsa_kernel(q_ref, dtg_ref, kc_ref, vct_ref, ks_ref, vst_ref, kw_ref, vwt_ref, ovt_ref,
                o_ref, sel_ref, acc_ref, *, seq):
    tq, tk, wt = NSA_TQ, NSA_TK, NSA_WT
    hd = NSA_HEAD_DIM
    nc = seq // CMP_STRIDE
    nb = seq // SEL_BLOCK
    q0 = pl.program_id(1) * tq
    tpos = q0 + lax.broadcasted_iota(jnp.int32, (1, tq), 1)

    qt = q_ref[0].T.astype(BF16)
    q_heads = [qt[h * hd:(h + 1) * hd, :] for h in range(NSA_HEADS)]
    gates = _sigmoid(dtg_ref[0]).T

    kc = kc_ref[0]
    vct = vct_ref[0]
    n_io = lax.broadcasted_iota(jnp.int32, (nc, tq), 0)
    cmask = (n_io * CMP_STRIDE + (CMP_BLOCK - 1)) <= tpos
    psum = jnp.zeros((nc, tq), F32)
    o_cmp = []
    for h in range(NSA_HEADS):
        s = jnp.where(cmask, _dot(kc, q_heads[h]), NEG)
        m = jnp.max(s, axis=0, keepdims=True)
        p = jnp.where(cmask, jnp.exp(s - m), 0.0)
        p = p / jnp.maximum(jnp.sum(p, axis=0, keepdims=True), 1e-20)
        psum = psum + p
        o_cmp.append(_dot(vct, p.astype(BF16)))

    ovt = ovt_ref[...]
    p1, p2, p3 = _split3(psum)
    imp = _dot(ovt, p1) + _dot(ovt, p2) + _dot(ovt, p3)
    j_io = lax.broadcasted_iota(jnp.int32, (nb, tq), 0)
    cur = jnp.right_shift(tpos, int(math.log2(SEL_BLOCK)))
    valid = j_io <= cur
    forced = (j_io == 0) | (valid & (j_io > cur - SEL_LOCAL))
    v = jnp.where(forced, FORCE, jnp.where(valid, imp, NEG))
    j_f = j_io.astype(F32)

    def extract(_, carry):
        v, sel = carry
        m = jnp.max(v, axis=0, keepdims=True)
        first = jnp.min(jnp.where(v == m, j_f, float(nb)), axis=0, keepdims=True)
        hit = j_f == first
        return jnp.where(hit, REMOVED, v), jnp.where(hit, 1.0, sel)

    _, sel = lax.fori_loop(0, SEL_TOPK, extract, (v, jnp.zeros((nb, tq), F32)), unroll=True)
    sel_ref[...] = sel

    def attend(k_tile, vt_tile, mask, ms, ls, base):
        new_m, new_l = [], []
        for h in range(NSA_HEADS):
            s = jnp.where(mask, _dot(k_tile, q_heads[h]), NEG)
            m_new = jnp.maximum(ms[h], jnp.max(s, axis=0, keepdims=True))
            alpha = jnp.exp(ms[h] - m_new)
            p = jnp.exp(s - m_new)
            new_l.append(alpha * ls[h] + jnp.sum(p, axis=0, keepdims=True))
            rows = slice(base + h * hd, base + (h + 1) * hd)
            acc_ref[rows, :] = alpha * acc_ref[rows, :] + _dot(vt_tile, p.astype(BF16))
            new_m.append(m_new)
        return tuple(new_m), tuple(new_l)

    def load_vt(ref, key0, width):
        first = key0 // LANES
        return jnp.concatenate([ref[0, first + i] for i in range(width // LANES)], axis=1)

    acc_ref[...] = jnp.zeros_like(acc_ref)
    init = tuple(jnp.full((1, tq), NEG, F32) for _ in range(NSA_HEADS))
    zero = tuple(jnp.zeros((1, tq), F32) for _ in range(NSA_HEADS))

    k_io = lax.broadcasted_iota(jnp.int32, (tk, tq), 0)
    blocks_per_tile = tk // SEL_BLOCK

    def sel_tile(kt, carry):
        ms, ls = carry
        k0 = pl.multiple_of(kt * tk, tk)
        chunk = sel_ref[pl.ds(pl.multiple_of(kt * blocks_per_tile, blocks_per_tile), blocks_per_tile), :]
        blockmask = jnp.concatenate(
            [jnp.broadcast_to(chunk[r:r + 1, :], (SEL_BLOCK, tq)) for r in range(blocks_per_tile)], axis=0)
        mask = jnp.where((k0 + k_io) <= tpos, blockmask, 0.0) > 0.5
        return attend(ks_ref[0, pl.ds(k0, tk), :], load_vt(vst_ref, k0, tk), mask, ms, ls, 0)

    n_kt = (q0 + tq + tk - 1) // tk
    ms_s, ls_s = lax.fori_loop(0, n_kt, sel_tile, (init, zero))

    w_io = lax.broadcasted_iota(jnp.int32, (wt, tq), 0)
    ms_w, ls_w = init, zero
    for off in range(0, -(WINDOW + wt), -wt):
        start = q0 + off
        startc = pl.multiple_of(jnp.maximum(start, 0), wt)
        kp = start + w_io
        mask = jnp.where(kp <= tpos, jnp.where(kp > tpos - WINDOW, jnp.where(kp >= 0, 1.0, 0.0), 0.0), 0.0) > 0.5
        ms_w, ls_w = attend(kw_ref[0, pl.ds(startc, wt), :], load_vt(vwt_ref, startc, wt), mask,
                            ms_w, ls_w, NSA_WIDTH)

    outs = []
    for h in range(NSA_HEADS):
        g = GATE_COL + 3 * h
        o_sel = acc_ref[h * hd:(h + 1) * hd, :] / ls_s[h]
        o_win = acc_ref[NSA_WIDTH + h * hd:NSA_WIDTH + (h + 1) * hd, :] / ls_w[h]
        outs.append(o_cmp[h] * gates[g:g + 1, :] + o_sel * gates[g + 1:g + 2, :] + o_win * gates[g + 2:g + 3, :])
    o_ref[0] = jnp.concatenate(outs, axis=0).T


def _nsa(q, dtg, kc, vct, ks, vst, kw, vwt, ovt):
    b, s, _ = q.shape
    tq = NSA_TQ
    nc = s // CMP_STRIDE
    per_b3 = lambda shape: pl.BlockSpec((1,) + shape, lambda i, j: (i, 0, 0))
    per_b4 = lambda shape: pl.BlockSpec((1,) + shape, lambda i, j: (i, 0, 0, 0))
    qblk = lambda w: pl.BlockSpec((1, tq, w), lambda i, j: (i, j, 0))
    return pl.pallas_call(
        functools.partial(_nsa_kernel, seq=s),
        grid=(b, s // tq),
        in_specs=[
            qblk(NSA_WIDTH), qblk(LANES),
            per_b3((nc, NSA_HEAD_DIM)), per_b3((NSA_HEAD_DIM, nc)),
            per_b3((s, NSA_HEAD_DIM)), per_b4((s // LANES, NSA_HEAD_DIM, LANES)),
            per_b3((s, NSA_HEAD_DIM)), per_b4((s // LANES, NSA_HEAD_DIM, LANES)),
            pl.BlockSpec(ovt.shape, lambda i, j: (0, 0)),
        ],
        out_specs=qblk(NSA_WIDTH),
        out_shape=jax.ShapeDtypeStruct((b, s, NSA_WIDTH), F32),
        scratch_shapes=[
            pltpu.VMEM((s // SEL_BLOCK, tq), F32),
            pltpu.VMEM((2 * NSA_WIDTH, tq), F32),
        ],
        compiler_params=pltpu.CompilerParams(
            dimension_semantics=("arbitrary", "arbitrary"), vmem_limit_bytes=VMEM_LIMIT_BYTES),
        name="nsa",
    )(q, dtg, kc, vct, ks, vst, kw, vwt, ovt)


def _ffn_kernel(x_ref, ya_ref, yn_ref, wo_ref, nw_ref, wu_ref, cw_ref, cb_ref, wd_ref, fw_ref,
                o_ref, tail_ref, act_ref, *, tiles_per_seq, final_norm):
    tm = x_ref.shape[0]
    na = ya_ref.shape[1]

    @pl.when(pl.program_id(0) % tiles_per_seq == 0)
    def _():
        tail_ref[...] = jnp.zeros_like(tail_ref)

    x1 = (x_ref[...] + _dot(ya_ref[...].astype(BF16), wo_ref[0:na, :])
          + _dot(yn_ref[...].astype(BF16), wo_ref[na:, :]))
    ms = jnp.mean(x1 * x1, axis=-1, keepdims=True)
    h = (x1 * lax.rsqrt(ms + EPS) * nw_ref[...]).astype(BF16)

    for j in range(N_FF_TILES):
        u = _dot(h, wu_ref[j])
        tail = tail_ref[j]
        cw = cw_ref[j]
        cv = cb_ref[j] + cw[FFN_CONV - 1:FFN_CONV, :] * u
        for t in range(FFN_CONV - 1):
            cv = cv + cw[t:t + 1, :] * _shift_rows(u, tail, FFN_CONV - 1 - t)
        tail_ref[j] = u[tm - SUBLANES:tm]
        gate = cv[:, 0:FF_TILE]
        act_ref[:, j * FF_TILE:(j + 1) * FF_TILE] = (gate * _sigmoid(gate) * cv[:, FF_TILE:]).astype(BF16)

    x2 = x1 + _dot(act_ref[...], wd_ref[...])
    if final_norm:
        ms2 = jnp.mean(x2 * x2, axis=-1, keepdims=True)
        x2 = x2 * lax.rsqrt(ms2 + EPS) * fw_ref[...]
    o_ref[...] = x2


def _ffn(x2d, ya, yn, wo, nw, wu_t, cw_t, cb_t, wd, fw, seq, final_norm):
    t = x2d.shape[0]
    tm = FFN_TM
    row = lambda i: (i, 0)
    c2 = lambda a: pl.BlockSpec(a.shape, lambda i: (0, 0))
    c3 = lambda a: pl.BlockSpec(a.shape, lambda i: (0, 0, 0))
    return pl.pallas_call(
        functools.partial(_ffn_kernel, tiles_per_seq=seq // tm, final_norm=final_norm),
        grid=(t // tm,),
        in_specs=[
            pl.BlockSpec((tm, D_MODEL), row),
            pl.BlockSpec((tm, ya.shape[1]), row),
            pl.BlockSpec((tm, yn.shape[1]), row),
            c2(wo), c2(nw), c3(wu_t), c3(cw_t), c3(cb_t), c2(wd), c2(fw),
        ],
        out_specs=pl.BlockSpec((tm, D_MODEL), row),
        out_shape=jax.ShapeDtypeStruct((t, D_MODEL), F32),
        scratch_shapes=[
            pltpu.VMEM((N_FF_TILES, SUBLANES, 2 * FF_TILE), F32),
            pltpu.VMEM((tm, D_FF_PAD), BF16),
        ],
        compiler_params=pltpu.CompilerParams(
            dimension_semantics=("arbitrary",), vmem_limit_bytes=VMEM_LIMIT_BYTES),
        name="outproj_ffn",
    )(x2d, ya, yn, wo, nw, wu_t, cw_t, cb_t, wd, fw)


def _pack_w_in(w):
    sizes = (SSD_INNER, SSD_XBC, SSD_HEADS, SC_WIDTH, SC_WIDTH, SC_WIDTH, NSA_WIDTH,
             NSA_HEAD_DIM, NSA_HEAD_DIM, NSA_HEAD_DIM, NSA_HEAD_DIM, NSA_HEAD_DIM, NSA_HEAD_DIM, 3 * NSA_HEADS)
    offs = np.concatenate([[0], np.cumsum(sizes)])
    part = [w[:, offs[i]:offs[i + 1]] for i in range(len(sizes))]
    (z, xbc, dt, sc_b, sc_c, sc_h, q, k_c, v_c, k_s, v_s, k_w, v_w, g) = part
    pad = jnp.zeros((w.shape[0], LANES - SSD_HEADS - 3 * NSA_HEADS), w.dtype)
    return jnp.concatenate([z, xbc, sc_b, sc_c, sc_h, q, k_c, v_c, k_s, v_s, k_w, v_w, dt, g, pad],
                           axis=1).astype(BF16)


def _rope_tables(seq):
    half = NSA_HEAD_DIM // 2
    inv = 1.0 / (ROPE_THETA ** (jnp.arange(half, dtype=F32) / half))
    ang = jnp.arange(seq, dtype=F32)[:, None] * inv[None, :]
    cos, sin = jnp.cos(ang), jnp.sin(ang)
    scale = NSA_HEAD_DIM ** -0.5
    one = jnp.ones_like(cos)
    zero = jnp.zeros_like(cos)
    rqc = jnp.concatenate([cos, cos, cos, cos], axis=1) * scale
    rqs = jnp.concatenate([-sin, sin, -sin, sin], axis=1) * scale
    rkc = jnp.concatenate([cos, cos, one, one], axis=1)
    rks = jnp.concatenate([-sin, sin, zero, zero], axis=1)
    return rqc, rqs, rkc, rks


def _compress_weights(kw1, kw2, vw1, vw2):
    half_tokens = CMP_BLOCK // 2
    hd, hid = NSA_HEAD_DIM, CMP_HIDDEN
    kw1r = kw1.reshape(2, half_tokens, hd, hid)
    vw1r = vw1.reshape(2, half_tokens, hd, hid)
    zeros = jnp.zeros((half_tokens, hd, hid), kw1.dtype)
    cols = []
    for w1r, is_k in ((kw1r, True), (vw1r, False)):
        for part in range(2):
            blk = w1r[part]
            rows = jnp.concatenate([blk, zeros] if is_k else [zeros, blk], axis=1)
            cols.append(rows.reshape(half_tokens * LANES, hid))
    wc = jnp.concatenate(cols, axis=1).astype(BF16)
    zk = jnp.zeros((hid, hd), kw2.dtype)
    w2bd = jnp.concatenate([jnp.concatenate([kw2, zk], axis=1),
                            jnp.concatenate([zk, vw2], axis=1)], axis=0).astype(BF16)
    return wc, w2bd


def _overlap_t(seq):
    nc = seq // CMP_STRIDE
    nb = seq // SEL_BLOCK
    cmp_start = np.arange(nc) * CMP_STRIDE
    slc_start = np.arange(nb) * SEL_BLOCK
    ov = ((cmp_start[None, :] < slc_start[:, None] + SEL_BLOCK)
          & (cmp_start[None, :] + CMP_BLOCK > slc_start[:, None])
          & (np.arange(nc)[None, :] < nc - 1))
    return jnp.asarray(ov.astype(np.float32), dtype=BF16)


def _pad_rows(a, rows):
    return jnp.concatenate([a, jnp.zeros((rows - a.shape[0],) + a.shape[1:], a.dtype)], axis=0)


def _pad_cols(a, cols):
    return jnp.concatenate([a, jnp.zeros(a.shape[:-1] + (cols - a.shape[-1],), a.dtype)], axis=-1)


def _ff_tiles(a):
    r = a.shape[0]
    gate = _pad_cols(a[:, :D_FF], D_FF_PAD).reshape(r, N_FF_TILES, FF_TILE)
    val = _pad_cols(a[:, D_FF:], D_FF_PAD).reshape(r, N_FF_TILES, FF_TILE)
    return jnp.transpose(jnp.concatenate([gate, val], axis=2), (1, 0, 2))


def kernel(x, attn_norm_w, w_in, ssd_conv_w, ssd_conv_b, ssd_dt_bias, ssd_a_log, ssd_d, ssd_norm_w, sc_conv_w,
           cmp_k_pos, cmp_k_w1, cmp_k_w2, cmp_v_pos, cmp_v_w1, cmp_v_w2, w_out, ffn_norm_w, ffn_w_up,
           ffn_conv_w, ffn_conv_b, ffn_w_down, final_norm_w):
    b, s, d = x.shape
    depth = w_in.shape[0]
    assert d == D_MODEL and s % NSA_TK == 0 and s // SEL_BLOCK >= SEL_TOPK and s % INPROJ_TM == 0
    t = b * s
    ropes = _rope_tables(s)
    ovt = _overlap_t(s)
    nc = s // CMP_STRIDE
    half_tokens = CMP_BLOCK // 2
    x2d = x.reshape(t, d)
    for l in range(depth):
        w_pack = _pack_w_in(w_in[l])
        cw = _pad_rows(ssd_conv_w[l], SUBLANES)
        cb = ssd_conv_b[l][None, :]
        dtb = _pad_cols(ssd_dt_bias[l][None, :], LANES)
        aneg = _pad_cols(-jnp.exp(ssd_a_log[l].astype(F32))[None, :], LANES)
        dsk = jnp.repeat(ssd_d[l].astype(F32), SSD_HEAD_DIM)[None, :]
        scw = _pad_rows(sc_conv_w[l], SUBLANES)
        wc, w2bd = _compress_weights(cmp_k_w1[l], cmp_k_w2[l], cmp_v_w1[l], cmp_v_w2[l])
        pos2 = _pad_rows(jnp.concatenate([cmp_k_pos[l], cmp_v_pos[l]], axis=1).reshape(2, half_tokens * LANES),
                         SUBLANES)
        wo = w_out[l].astype(BF16)
        wu_t = _ff_tiles(ffn_w_up[l]).astype(BF16)
        cw_t = _ff_tiles(_pad_rows(ffn_conv_w[l], SUBLANES))
        cb_t = _ff_tiles(ffn_conv_b[l][None, :])
        wd = _pad_rows(ffn_w_down[l], D_FF_PAD).astype(BF16)

        z, xbc, sc, q, cpair, ks, kw, vst, vwt, dtg = _inproj(x2d, attn_norm_w[l][None, :], w_pack, ropes, s)
        kc, vct = _compress(cpair.reshape(b, nc, half_tokens * LANES), pos2, wc, w2bd)
        r3 = lambda a: a.reshape(b, s, a.shape[-1])
        ya = _ssd(r3(z), r3(xbc), r3(sc), r3(dtg), cw, cb, dtb, aneg, dsk, ssd_norm_w[l][None, :], scw)
        vt4 = lambda a: a.reshape(b, s // LANES, NSA_HEAD_DIM, LANES)
        yn = _nsa(r3(q), r3(dtg), kc, vct, r3(ks), vt4(vst), r3(kw), vt4(vwt), ovt)
        x2d = _ffn(x2d, ya.reshape(t, -1), yn.reshape(t, -1), wo, ffn_norm_w[l][None, :], wu_t, cw_t, cb_t, wd,
                   final_norm_w[None, :], s, l == depth - 1)
    return x2d.reshape(b, s, d)
```

```python
import functools
import math

import numpy as np
import jax
import jax.numpy as jnp
from jax import lax
from jax.experimental import pallas as pl
from jax.experimental.pallas import tpu as pltpu

F32 = jnp.float32
BF16 = jnp.bfloat16

D_MODEL = 1024
SSD_HEADS = 8
SSD_HEAD_DIM = 64
SSD_INNER = SSD_HEADS * SSD_HEAD_DIM
SSD_GROUPS = 2
SSD_STATE = 128
SSD_CONV = 4
SSD_CHUNK = 256
SSD_XBC = SSD_INNER + 2 * SSD_GROUPS * SSD_STATE
SC_WIDTH = 256
SC_CONV = 3
NSA_HEADS = 4
NSA_HEAD_DIM = 64
NSA_WIDTH = NSA_HEADS * NSA_HEAD_DIM
CMP_BLOCK = 32
CMP_STRIDE = 16
CMP_HIDDEN = 128
SEL_BLOCK = 64
SEL_TOPK = 16
SEL_LOCAL = 2
WINDOW = 512
ROPE_THETA = 10000.0
D_FF = 2752
FFN_CONV = 3
EPS = 1e-6
NEG = -1e30
FORCE = 1e9
REMOVED = -3.0e38
LOG2E = 1.4426950408889634

LANES = 128
SUBLANES = 8
VMEM_LIMIT_BYTES = 56 * 1024 * 1024

D_FF_PAD = 2816
FF_TILE = 256
N_FF_TILES = D_FF_PAD // FF_TILE

COL_Z = 0
COL_XBC = COL_Z + SSD_INNER
COL_SC = COL_XBC + SSD_XBC
COL_Q = COL_SC + 3 * SC_WIDTH
COL_KV = COL_Q + NSA_WIDTH
COL_DTG = COL_KV + 6 * NSA_HEAD_DIM
N_PACK = COL_DTG + LANES
GATE_COL = SSD_HEADS

INPROJ_TM = 512
NSA_TQ = 256
NSA_TK = 512
NSA_WKEYS = WINDOW + NSA_TQ
FFN_TM = 256


def _dot(a, b):
    return jnp.dot(a, b, preferred_element_type=F32)


def _dot_nt(a, b):
    return lax.dot_general(a, b, (((1,), (1,)), ((), ())), preferred_element_type=F32)


def _sigmoid(x):
    return 1.0 / (1.0 + jnp.exp(-x))


def _softplus(x):
    return jnp.maximum(x, 0.0) + jnp.log1p(jnp.exp(-jnp.abs(x)))


def _split3(a):
    a1 = a.astype(BF16)
    r1 = a - a1.astype(F32)
    a2 = r1.astype(BF16)
    r2 = r1 - a2.astype(F32)
    return a1, a2, r2.astype(BF16)


def _shift_rows(cur, tail, k):
    if k == 0:
        return cur
    rc = pltpu.roll(cur, k, 0)
    rt = pltpu.roll(tail, k, 0)
    row = lax.broadcasted_iota(jnp.int32, tail.shape, 0)
    first = jnp.where(row < k, rt, rc[0:SUBLANES])
    return jnp.concatenate([first, rc[SUBLANES:]], axis=0)


def _inproj_kernel(x_ref, nw_ref, w_ref, rqc_ref, rqs_ref, rkc_ref, rks_ref,
                   z_ref, xbc_ref, sc_ref, q_ref, cpair_ref, ks_ref, kw_ref, vst_ref, vwt_ref, dtg_ref):
    tm = x_ref.shape[0]
    x = x_ref[...]
    ms = jnp.mean(x * x, axis=-1, keepdims=True)
    h = (x * lax.rsqrt(ms + EPS) * nw_ref[...]).astype(BF16)

    def proj(a, b):
        return _dot(h, w_ref[:, a:b])

    z_ref[...] = proj(COL_Z, COL_XBC)
    xbc_ref[...] = proj(COL_XBC, COL_SC)
    sc_ref[...] = proj(COL_SC, COL_Q)
    dtg_ref[...] = proj(COL_DTG, N_PACK)

    lane = lax.broadcasted_iota(jnp.int32, (tm, LANES), 1)
    first_half = (lane % NSA_HEAD_DIM) < (NSA_HEAD_DIM // 2)

    def rope(v, c, s):
        partner = jnp.where(first_half, pltpu.roll(v, LANES - 32, 1), pltpu.roll(v, 32, 1))
        return v * c + partner * s

    rqc, rqs, rkc, rks = rqc_ref[...], rqs_ref[...], rkc_ref[...], rks_ref[...]
    q = proj(COL_Q, COL_KV)
    q_ref[:, 0:LANES] = rope(q[:, 0:LANES], rqc, rqs)
    q_ref[:, LANES:2 * LANES] = rope(q[:, LANES:2 * LANES], rqc, rqs)

    kv = proj(COL_KV, COL_DTG)
    cpair_ref[...] = rope(kv[:, 0:LANES], rkc, rks)
    row = lax.broadcasted_iota(jnp.int32, (tm, LANES), 0)
    block_in_tile = (row % NSA_TK) // SEL_BLOCK
    onehot = jnp.where(lane - NSA_HEAD_DIM == block_in_tile, 1.0, 0.0)
    k_lanes = lane < NSA_HEAD_DIM
    for pair, k_ref, vt_ref, fill in ((1, ks_ref, vst_ref, onehot), (2, kw_ref, vwt_ref, 0.0)):
        p = rope(kv[:, pair * LANES:(pair + 1) * LANES], rkc, rks)
        k_ref[...] = jnp.where(k_lanes, p, fill).astype(BF16)
        pt = p.T
        for j in range(tm // LANES):
            vt_ref[j] = pt[NSA_HEAD_DIM:LANES, j * LANES:(j + 1) * LANES].astype(BF16)


def _inproj(x2d, nw, w_pack, ropes, seq):
    t = x2d.shape[0]
    tm = INPROJ_TM
    nt = t // tm
    pos_blocks = seq // tm
    row = lambda i: (i, 0)
    pos = lambda i: (i % pos_blocks, 0)
    const = lambda i: (0, 0)
    out_shapes = (
        jax.ShapeDtypeStruct((t, SSD_INNER), F32),
        jax.ShapeDtypeStruct((t, SSD_XBC), F32),
        jax.ShapeDtypeStruct((t, 3 * SC_WIDTH), F32),
        jax.ShapeDtypeStruct((t, NSA_WIDTH), F32),
        jax.ShapeDtypeStruct((t, LANES), F32),
        jax.ShapeDtypeStruct((t, LANES), BF16),
        jax.ShapeDtypeStruct((t, LANES), BF16),
        jax.ShapeDtypeStruct((t // LANES, NSA_HEAD_DIM, LANES), BF16),
        jax.ShapeDtypeStruct((t // LANES, NSA_HEAD_DIM, LANES), BF16),
        jax.ShapeDtypeStruct((t, LANES), F32),
    )
    vt_spec = pl.BlockSpec((tm // LANES, NSA_HEAD_DIM, LANES), lambda i: (i, 0, 0))
    out_specs = (
        pl.BlockSpec((tm, SSD_INNER), row),
        pl.BlockSpec((tm, SSD_XBC), row),
        pl.BlockSpec((tm, 3 * SC_WIDTH), row),
        pl.BlockSpec((tm, NSA_WIDTH), row),
        pl.BlockSpec((tm, LANES), row),
        pl.BlockSpec((tm, LANES), row),
        pl.BlockSpec((tm, LANES), row),
        vt_spec,
        vt_spec,
        pl.BlockSpec((tm, LANES), row),
    )
    in_specs = [
        pl.BlockSpec((tm, D_MODEL), row),
        pl.BlockSpec((1, D_MODEL), const),
        pl.BlockSpec((D_MODEL, N_PACK), const),
        pl.BlockSpec((tm, LANES), pos),
        pl.BlockSpec((tm, LANES), pos),
        pl.BlockSpec((tm, LANES), pos),
        pl.BlockSpec((tm, LANES), pos),
    ]
    return pl.pallas_call(
        _inproj_kernel,
        grid=(nt,),
        in_specs=in_specs,
        out_specs=out_specs,
        out_shape=out_shapes,
        compiler_params=pltpu.CompilerParams(
            dimension_semantics=("arbitrary",), vmem_limit_bytes=VMEM_LIMIT_BYTES),
        name="inproj",
    )(x2d, nw, w_pack, *ropes)


def _compress_kernel(x2_ref, pos_ref, wc_ref, w2_ref, kc_ref, vct_ref):
    nc = x2_ref.shape[1]
    x2 = x2_ref[0].astype(BF16)
    wc = wc_ref[...]
    y = _dot(x2, wc)
    r = _dot(pos_ref[...].astype(BF16), wc)
    h = CMP_HIDDEN

    def pre(base):
        bias = r[0:1, base:base + h] + r[1:2, base + h:base + 2 * h]
        return y[:, base:base + h] + pltpu.roll(y[:, base + h:base + 2 * h], nc - 1, 0) + bias

    hid = jnp.concatenate([jax.nn.gelu(pre(0)), jax.nn.gelu(pre(2 * h))], axis=1).astype(BF16)
    o = _dot(hid, w2_ref[...])
    kc_ref[0] = o.astype(BF16)
    vct_ref[0] = o.T[NSA_HEAD_DIM:LANES, :].astype(BF16)


def _compress(x2p, pos2, wc, w2bd):
    b, nc, width = x2p.shape
    return pl.pallas_call(
        _compress_kernel,
        grid=(b,),
        in_specs=[
            pl.BlockSpec((1, nc, width), lambda i: (i, 0, 0)),
            pl.BlockSpec(pos2.shape, lambda i: (0, 0)),
            pl.BlockSpec(wc.shape, lambda i: (0, 0)),
            pl.BlockSpec(w2bd.shape, lambda i: (0, 0)),
        ],
        out_specs=(
            pl.BlockSpec((1, nc, LANES), lambda i: (i, 0, 0)),
            pl.BlockSpec((1, NSA_HEAD_DIM, nc), lambda i: (i, 0, 0)),
        ),
        out_shape=(
            jax.ShapeDtypeStruct((b, nc, LANES), BF16),
            jax.ShapeDtypeStruct((b, NSA_HEAD_DIM, nc), BF16),
        ),
        compiler_params=pltpu.CompilerParams(
            dimension_semantics=("arbitrary",), vmem_limit_bytes=VMEM_LIMIT_BYTES),
        name="compress",
    )(x2p, pos2, wc, w2bd)


def _ssd_kernel(z_ref, xbc_ref, sc_ref, dtg_ref, cw_ref, cb_ref, dtb_ref, aneg_ref, dsk_ref, nw_ref, scw_ref,
                y_ref, h_ref, xtail_ref, stail_ref):
    L = SSD_CHUNK
    half = SSD_HEAD_DIM

    @pl.when(pl.program_id(1) == 0)
    def _():
        h_ref[...] = jnp.zeros_like(h_ref)
        xtail_ref[...] = jnp.zeros_like(xtail_ref)
        stail_ref[...] = jnp.zeros_like(stail_ref)

    xraw = xbc_ref[0]
    xtail = xtail_ref[...]
    conv = cb_ref[...] + cw_ref[SSD_CONV - 1:SSD_CONV, :] * xraw
    for j in range(SSD_CONV - 1):
        conv = conv + cw_ref[j:j + 1, :] * _shift_rows(xraw, xtail, SSD_CONV - 1 - j)
    xtail_ref[...] = xraw[L - SUBLANES:L]
    xc = conv * _sigmoid(conv)

    dt = _softplus(dtg_ref[0] + dtb_ref[...])
    a = dt * aneg_ref[...]
    row = lax.broadcasted_iota(jnp.int32, (L, L), 0)
    col = lax.broadcasted_iota(jnp.int32, (L, L), 1)
    causal = row >= col
    tri = jnp.where(causal, 1.0, 0.0).astype(BF16)
    a1, a2, a3 = _split3(a)
    acs = _dot(tri, a1) + _dot(tri, a2) + _dot(tri, a3)
    acs_t = acs.T
    exp_acs = jnp.exp(acs)
    last = acs[L - 1:L, :]
    dte = jnp.exp(last - acs)
    chunk_decay = jnp.exp(last)

    lo = lax.broadcasted_iota(jnp.int32, (L, LANES), 1) < half
    lo_row = lax.broadcasted_iota(jnp.int32, (1, LANES), 1) < half

    def per_lane(m, ha, hb, mask):
        return jnp.where(mask, m[:, ha:ha + 1], m[:, hb:hb + 1])

    ys = []
    for g in range(SSD_GROUPS):
        b_g = xc[:, SSD_INNER + g * SSD_STATE:SSD_INNER + (g + 1) * SSD_STATE]
        c_g = xc[:, SSD_INNER + (SSD_GROUPS + g) * SSD_STATE:SSD_INNER + (SSD_GROUPS + g + 1) * SSD_STATE]
        b_bf = b_g.astype(BF16)
        c_bf = c_g.astype(BF16)
        cb = _dot_nt(c_bf, b_bf)
        bt_bf = b_g.T.astype(BF16)
        for pp in range(SSD_HEADS // SSD_GROUPS // 2):
            p = g * (SSD_HEADS // SSD_GROUPS // 2) + pp
            ha, hb = 2 * p, 2 * p + 1
            x_pair = xc[:, p * LANES:(p + 1) * LANES]
            xdt = x_pair * per_lane(dt, ha, hb, lo)
            xdt_bf = xdt.astype(BF16)
            yd = []
            for hd in (ha, hb):
                seg = acs[:, hd:hd + 1] - acs_t[hd:hd + 1, :]
                decay = jnp.exp(jnp.where(causal, seg, NEG))
                yd.append(_dot((decay * cb).astype(BF16), xdt_bf))
            y_diag = jnp.where(lo, yd[0], yd[1])
            st = _dot(bt_bf, (xdt * per_lane(dte, ha, hb, lo)).astype(BF16))
            h_prev = h_ref[p]
            y_off = _dot(c_bf, h_prev.astype(BF16)) * per_lane(exp_acs, ha, hb, lo)
            h_ref[p] = h_prev * per_lane(chunk_decay, ha, hb, lo_row) + st
            ys.append(y_diag + y_off + x_pair * dsk_ref[:, p * LANES:(p + 1) * LANES])
    y = jnp.concatenate(ys, axis=1)
    z = z_ref[0]
    y = y * (z * _sigmoid(z))
    ms = jnp.mean(y * y, axis=-1, keepdims=True)
    y_ref[0, :, 0:SSD_INNER] = y * lax.rsqrt(ms + EPS) * nw_ref[...]

    sc = sc_ref[0]
    u = sc[:, SC_WIDTH:2 * SC_WIDTH] * sc[:, 2 * SC_WIDTH:3 * SC_WIDTH]
    stail = stail_ref[...]
    cv = scw_ref[SC_CONV - 1:SC_CONV, :] * u
    for j in range(SC_CONV - 1):
        cv = cv + scw_ref[j:j + 1, :] * _shift_rows(u, stail, SC_CONV - 1 - j)
    stail_ref[...] = u[L - SUBLANES:L]
    y_ref[0, :, SSD_INNER:SSD_INNER + SC_WIDTH] = sc[:, 0:SC_WIDTH] * cv


def _ssd(z, xbc, sc, dtg, cw, cb, dtb, aneg, dsk, nw, scw):
    b, s, _ = z.shape
    L = SSD_CHUNK
    blk = lambda w: pl.BlockSpec((1, L, w), lambda i, c: (i, c, 0))
    par = lambda a: pl.BlockSpec(a.shape, lambda i, c: (0, 0))
    return pl.pallas_call(
        _ssd_kernel,
        grid=(b, s // L),
        in_specs=[blk(SSD_INNER), blk(SSD_XBC), blk(3 * SC_WIDTH), blk(LANES),
                  par(cw), par(cb), par(dtb), par(aneg), par(dsk), par(nw), par(scw)],
        out_specs=blk(SSD_INNER + SC_WIDTH),
        out_shape=jax.ShapeDtypeStruct((b, s, SSD_INNER + SC_WIDTH), F32),
        scratch_shapes=[
            pltpu.VMEM((SSD_HEADS // 2, SSD_STATE, LANES), F32),
            pltpu.VMEM((SUBLANES, SSD_XBC), F32),
            pltpu.VMEM((SUBLANES, SC_WIDTH), F32),
        ],
        compiler_params=pltpu.CompilerParams(
            dimension_semantics=("arbitrary", "arbitrary"), vmem_limit_bytes=VMEM_LIMIT_BYTES),
        name="ssd_sc",
    )(z, xbc, sc, dtg, cw, cb, dtb, aneg, dsk, nw, scw)


def _nsa_kernel(q_ref, dtg_ref, kc_ref, vct_ref, ks_ref, vst_ref, kw_ref, vwt_ref, ovt_ref,
                o_ref, sel_ref, qa_ref, acc_ref, m_ref, l_ref, s_ref, *, seq):
    tq, tk = NSA_TQ, NSA_TK
    hd = NSA_HEAD_DIM
    nh = NSA_HEADS
    nc = seq // CMP_STRIDE
    nb = seq // SEL_BLOCK
    q0 = pl.program_id(1) * tq
    tpos = q0 + lax.broadcasted_iota(jnp.int32, (1, tq), 1)

    def heads_on_lanes(a):
        return jnp.concatenate([a] * nh, axis=1)

    qt = q_ref[0].T
    qa_ref[0:hd, :] = jnp.concatenate([qt[h * hd:(h + 1) * hd, :] for h in range(nh)], axis=1).astype(BF16)
    qa_ref[hd:LANES, :] = jnp.zeros((LANES - hd, nh * tq), BF16)
    gates = _sigmoid(dtg_ref[0]).T

    n_io = lax.broadcasted_iota(jnp.int32, (nc, tq), 0)
    cbias = jnp.where((n_io * CMP_STRIDE + (CMP_BLOCK - 1)) <= tpos, 0.0, NEG)
    s = _dot(kc_ref[0], qa_ref[...]) + heads_on_lanes(cbias)
    p = jnp.exp2(s - jnp.max(s, axis=0, keepdims=True))
    has_key = heads_on_lanes(jnp.where(tpos >= CMP_BLOCK - 1, 1.0, 0.0))
    p = p * (has_key / jnp.sum(p, axis=0, keepdims=True))
    o_cmp = _dot(vct_ref[0], p.astype(BF16))
    psum = p[:, 0:tq]
    for h in range(1, nh):
        psum = psum + p[:, h * tq:(h + 1) * tq]

    ovt = ovt_ref[...]
    p1, p2, p3 = _split3(psum)
    imp = _dot(ovt, p1) + _dot(ovt, p2) + _dot(ovt, p3)
    j_io = lax.broadcasted_iota(jnp.int32, (nb, tq), 0)
    cur = jnp.right_shift(tpos, int(math.log2(SEL_BLOCK)))
    valid = j_io <= cur
    forced = (j_io == 0) | (valid & (j_io > cur - SEL_LOCAL))
    v = jnp.where(forced, FORCE, jnp.where(valid, imp, NEG))
    j_f = j_io.astype(F32)

    def extract(_, carry):
        v, sel = carry
        m = jnp.max(v, axis=0, keepdims=True)
        first = jnp.min(jnp.where(v == m, j_f, float(nb)), axis=0, keepdims=True)
        hit = j_f == first
        return jnp.where(hit, REMOVED, v), jnp.where(hit, 1.0, sel)

    _, sel = lax.fori_loop(0, SEL_TOPK, extract, (v, jnp.zeros((nb, tq), F32)), unroll=True)
    sel_ref[...] = sel

    def load_vt(ref, key0, width):
        first = key0 // LANES
        return jnp.concatenate([ref[0, first + i] for i in range(width // LANES)], axis=1)

    wk = NSA_WKEYS
    w0 = pl.multiple_of(jnp.maximum(q0 - WINDOW, 0), tq)
    d = (w0 - q0) + (lax.broadcasted_iota(jnp.int32, (wk, tq), 0)
                     - lax.broadcasted_iota(jnp.int32, (wk, tq), 1))
    wbias = jnp.where(d <= 0, jnp.where(d > -WINDOW, 0.0, NEG), NEG)
    s = _dot(kw_ref[0, pl.ds(w0, wk), :], qa_ref[...]) + heads_on_lanes(wbias)
    p = jnp.exp2(s - jnp.max(s, axis=0, keepdims=True))
    o_win = _dot(load_vt(vwt_ref, w0, wk), p.astype(BF16)) / jnp.sum(p, axis=0, keepdims=True)

    blocks_per_tile = tk // SEL_BLOCK
    bias_rows = 2 * SUBLANES
    acc_ref[...] = jnp.zeros_like(acc_ref)
    m_ref[...] = jnp.full(m_ref.shape, NEG, F32)
    l_ref[...] = jnp.zeros_like(l_ref)

    def scores(buf, kt):
        k0 = pl.multiple_of(kt * tk, tk)
        chunk = sel_ref[pl.ds(pl.multiple_of(kt * blocks_per_tile, blocks_per_tile), blocks_per_tile), :]
        bias = jnp.concatenate([(1.0 - chunk) * NEG, jnp.zeros((bias_rows - blocks_per_tile, tq), F32)], axis=0)
        qa_ref[hd:hd + bias_rows, :] = heads_on_lanes(bias).astype(BF16)
        s_ref[buf] = _dot(ks_ref[0, pl.ds(k0, tk), :], qa_ref[...])

    def absorb(buf, kt, causal):
        k0 = pl.multiple_of(kt * tk, tk)
        s = s_ref[buf]
        if causal:
            kpos = k0 + lax.broadcasted_iota(jnp.int32, (tk, tq), 0)
            s = s + heads_on_lanes(jnp.where(kpos <= tpos, 0.0, NEG))
        m_old = m_ref[...]
        m_new = jnp.maximum(m_old, jnp.max(s, axis=0, keepdims=True))
        alpha = jnp.exp2(m_old - m_new)
        p = jnp.exp2(s - m_new)
        acc_ref[...] = alpha * acc_ref[...] + _dot(load_vt(vst_ref, k0, tk), p.astype(BF16))
        l_ref[...] = alpha * l_ref[...] + jnp.sum(p, axis=0, keepdims=True)
        m_ref[...] = m_new

    n_last = (q0 + tq + tk - 1) // tk - 1
    odd = n_last % 2

    @pl.when(odd == 1)
    def _():
        scores(0, 0)
        absorb(0, 0, False)

    scores(0, odd)

    def pair(j, _):
        kt = odd + 2 * j
        scores(1, kt + 1)
        absorb(0, kt, False)
        scores(0, kt + 2)
        absorb(1, kt + 1, False)
        return 0

    lax.fori_loop(0, (n_last - odd) // 2, pair, 0)
    absorb(0, n_last, True)
    o_sel = acc_ref[...] / l_ref[...]

    def gate_row(branch):
        return jnp.concatenate([gates[GATE_COL + 3 * h + branch:GATE_COL + 3 * h + branch + 1, :]
                                for h in range(nh)], axis=1)

    out = o_cmp * gate_row(0) + o_sel * gate_row(1) + o_win * gate_row(2)
    o_ref[0] = jnp.concatenate([out[:, h * tq:(h + 1) * tq] for h in range(nh)], axis=0).T


def _nsa(q, dtg, kc, vct, ks, vst, kw, vwt, ovt):
    b, s, _ = q.shape
    tq = NSA_TQ
    nc = s // CMP_STRIDE
    per_b3 = lambda shape: pl.BlockSpec((1,) + shape, lambda i, j: (i, 0, 0))
    per_b4 = lambda shape: pl.BlockSpec((1,) + shape, lambda i, j: (i, 0, 0, 0))
    qblk = lambda w: pl.BlockSpec((1, tq, w), lambda i, j: (i, j, 0))
    return pl.pallas_call(
        functools.partial(_nsa_kernel, seq=s),
        grid=(b, s // tq),
        in_specs=[
            qblk(NSA_WIDTH), qblk(LANES),
            per_b3((nc, LANES)), per_b3((NSA_HEAD_DIM, nc)),
            per_b3((s, LANES)), per_b4((s // LANES, NSA_HEAD_DIM, LANES)),
            per_b3((s, LANES)), per_b4((s // LANES, NSA_HEAD_DIM, LANES)),
            pl.BlockSpec(ovt.shape, lambda i, j: (0, 0)),
        ],
        out_specs=qblk(NSA_WIDTH),
        out_shape=jax.ShapeDtypeStruct((b, s, NSA_WIDTH), F32),
        scratch_shapes=[
            pltpu.VMEM((s // SEL_BLOCK, tq), F32),
            pltpu.VMEM((LANES, NSA_HEADS * tq), BF16),
            pltpu.VMEM((NSA_HEAD_DIM, NSA_HEADS * tq), F32),
            pltpu.VMEM((1, NSA_HEADS * tq), F32),
            pltpu.VMEM((1, NSA_HEADS * tq), F32),
            pltpu.VMEM((2, NSA_TK, NSA_HEADS * tq), F32),
        ],
        compiler_params=pltpu.CompilerParams(
            dimension_semantics=("arbitrary", "arbitrary"), vmem_limit_bytes=VMEM_LIMIT_BYTES),
        name="nsa",
    )(q, dtg, kc, vct, ks, vst, kw, vwt, ovt)


def _ffn_kernel(x_ref, ya_ref, yn_ref, wo_ref, nw_ref, wu_ref, cw_ref, cb_ref, wd_ref, fw_ref,
                o_ref, tail_ref, act_ref, *, tiles_per_seq, final_norm):
    tm = x_ref.shape[0]
    na = ya_ref.shape[1]

    @pl.when(pl.program_id(0) % tiles_per_seq == 0)
    def _():
        tail_ref[...] = jnp.zeros_like(tail_ref)

    x1 = (x_ref[...] + _dot(ya_ref[...].astype(BF16), wo_ref[0:na, :])
          + _dot(yn_ref[...].astype(BF16), wo_ref[na:, :]))
    ms = jnp.mean(x1 * x1, axis=-1, keepdims=True)
    h = (x1 * lax.rsqrt(ms + EPS) * nw_ref[...]).astype(BF16)

    def conv_cols(c0):
        cols = slice(c0, c0 + FF_TILE)
        u = _dot(h, wu_ref[:, cols])
        tail = tail_ref[:, cols]
        cv = cb_ref[:, cols] + cw_ref[FFN_CONV - 1:FFN_CONV, cols] * u
        for t in range(FFN_CONV - 1):
            cv = cv + cw_ref[t:t + 1, cols] * _shift_rows(u, tail, FFN_CONV - 1 - t)
        tail_ref[:, cols] = u[tm - SUBLANES:tm]
        return cv

    for j in range(N_FF_TILES):
        gate = conv_cols(j * FF_TILE)
        val = conv_cols(D_FF_PAD + j * FF_TILE)
        act_ref[:, j * FF_TILE:(j + 1) * FF_TILE] = (gate * _sigmoid(gate) * val).astype(BF16)

    x2 = x1 + _dot(act_ref[...], wd_ref[...])
    if final_norm:
        ms2 = jnp.mean(x2 * x2, axis=-1, keepdims=True)
        x2 = x2 * lax.rsqrt(ms2 + EPS) * fw_ref[...]
    o_ref[...] = x2


def _ffn(x2d, ya, yn, wo, nw, wu_t, cw_t, cb_t, wd, fw, seq, final_norm):
    t = x2d.shape[0]
    tm = FFN_TM
    row = lambda i: (i, 0)
    c2 = lambda a: pl.BlockSpec(a.shape, lambda i: (0, 0))
    return pl.pallas_call(
        functools.partial(_ffn_kernel, tiles_per_seq=seq // tm, final_norm=final_norm),
        grid=(t // tm,),
        in_specs=[
            pl.BlockSpec((tm, D_MODEL), row),
            pl.BlockSpec((tm, ya.shape[1]), row),
            pl.BlockSpec((tm, yn.shape[1]), row),
            c2(wo), c2(nw), c2(wu_t), c2(cw_t), c2(cb_t), c2(wd), c2(fw),
        ],
        out_specs=pl.BlockSpec((tm, D_MODEL), row),
        out_shape=jax.ShapeDtypeStruct((t, D_MODEL), F32),
        scratch_shapes=[
            pltpu.VMEM((SUBLANES, 2 * D_FF_PAD), F32),
            pltpu.VMEM((tm, D_FF_PAD), BF16),
        ],
        compiler_params=pltpu.CompilerParams(
            dimension_semantics=("arbitrary",), vmem_limit_bytes=VMEM_LIMIT_BYTES),
        name="outproj_ffn",
    )(x2d, ya, yn, wo, nw, wu_t, cw_t, cb_t, wd, fw)


def _pack_w_in(w):
    sizes = (SSD_INNER, SSD_XBC, SSD_HEADS, SC_WIDTH, SC_WIDTH, SC_WIDTH, NSA_WIDTH,
             NSA_HEAD_DIM, NSA_HEAD_DIM, NSA_HEAD_DIM, NSA_HEAD_DIM, NSA_HEAD_DIM, NSA_HEAD_DIM, 3 * NSA_HEADS)
    offs = np.concatenate([[0], np.cumsum(sizes)])
    part = [w[:, offs[i]:offs[i + 1]] for i in range(len(sizes))]
    (z, xbc, dt, sc_b, sc_c, sc_h, q, k_c, v_c, k_s, v_s, k_w, v_w, g) = part
    pad = jnp.zeros((w.shape[0], LANES - SSD_HEADS - 3 * NSA_HEADS), w.dtype)
    return jnp.concatenate([z, xbc, sc_b, sc_c, sc_h, q, k_c, v_c, k_s, v_s, k_w, v_w, dt, g, pad],
                           axis=1).astype(BF16)


def _rope_tables(seq):
    half = NSA_HEAD_DIM // 2
    inv = 1.0 / (ROPE_THETA ** (jnp.arange(half, dtype=F32) / half))
    ang = jnp.arange(seq, dtype=F32)[:, None] * inv[None, :]
    cos, sin = jnp.cos(ang), jnp.sin(ang)
    scale = NSA_HEAD_DIM ** -0.5 * LOG2E
    one = jnp.ones_like(cos)
    zero = jnp.zeros_like(cos)
    rqc = jnp.concatenate([cos, cos, cos, cos], axis=1) * scale
    rqs = jnp.concatenate([-sin, sin, -sin, sin], axis=1) * scale
    rkc = jnp.concatenate([cos, cos, one, one], axis=1)
    rks = jnp.concatenate([-sin, sin, zero, zero], axis=1)
    return rqc, rqs, rkc, rks


def _compress_weights(kw1, kw2, vw1, vw2):
    half_tokens = CMP_BLOCK // 2
    hd, hid = NSA_HEAD_DIM, CMP_HIDDEN
    kw1r = kw1.reshape(2, half_tokens, hd, hid)
    vw1r = vw1.reshape(2, half_tokens, hd, hid)
    zeros = jnp.zeros((half_tokens, hd, hid), kw1.dtype)
    cols = []
    for w1r, is_k in ((kw1r, True), (vw1r, False)):
        for part in range(2):
            blk = w1r[part]
            rows = jnp.concatenate([blk, zeros] if is_k else [zeros, blk], axis=1)
            cols.append(rows.reshape(half_tokens * LANES, hid))
    wc = jnp.concatenate(cols, axis=1).astype(BF16)
    zk = jnp.zeros((hid, hd), kw2.dtype)
    w2bd = jnp.concatenate([jnp.concatenate([kw2, zk], axis=1),
                            jnp.concatenate([zk, vw2], axis=1)], axis=0).astype(BF16)
    return wc, w2bd


def _overlap_t(seq):
    nc = seq // CMP_STRIDE
    nb = seq // SEL_BLOCK
    cmp_start = np.arange(nc) * CMP_STRIDE
    slc_start = np.arange(nb) * SEL_BLOCK
    ov = ((cmp_start[None, :] < slc_start[:, None] + SEL_BLOCK)
          & (cmp_start[None, :] + CMP_BLOCK > slc_start[:, None])
          & (np.arange(nc)[None, :] < nc - 1))
    return jnp.asarray(ov.astype(np.float32), dtype=BF16)


def _pad_rows(a, rows):
    return jnp.concatenate([a, jnp.zeros((rows - a.shape[0],) + a.shape[1:], a.dtype)], axis=0)


def _pad_cols(a, cols):
    return jnp.concatenate([a, jnp.zeros(a.shape[:-1] + (cols - a.shape[-1],), a.dtype)], axis=-1)


def _ff_tiles(a):
    return jnp.concatenate([_pad_cols(a[:, :D_FF], D_FF_PAD), _pad_cols(a[:, D_FF:], D_FF_PAD)], axis=1)


def kernel(x, attn_norm_w, w_in, ssd_conv_w, ssd_conv_b, ssd_dt_bias, ssd_a_log, ssd_d, ssd_norm_w, sc_conv_w,
           cmp_k_pos, cmp_k_w1, cmp_k_w2, cmp_v_pos, cmp_v_w1, cmp_v_w2, w_out, ffn_norm_w, ffn_w_up,
           ffn_conv_w, ffn_conv_b, ffn_w_down, final_norm_w):
    b, s, d = x.shape
    depth = w_in.shape[0]
    assert d == D_MODEL and s % NSA_TK == 0 and s // SEL_BLOCK >= SEL_TOPK and s % INPROJ_TM == 0
    assert INPROJ_TM % NSA_TK == 0 and s >= NSA_WKEYS and NSA_TK // SEL_BLOCK <= SUBLANES
    t = b * s
    ropes = _rope_tables(s)
    ovt = _overlap_t(s)
    nc = s // CMP_STRIDE
    half_tokens = CMP_BLOCK // 2
    x2d = x.reshape(t, d)
    for l in range(depth):
        w_pack = _pack_w_in(w_in[l])
        cw = _pad_rows(ssd_conv_w[l], SUBLANES)
        cb = ssd_conv_b[l][None, :]
        dtb = _pad_cols(ssd_dt_bias[l][None, :], LANES)
        aneg = _pad_cols(-jnp.exp(ssd_a_log[l].astype(F32))[None, :], LANES)
        dsk = jnp.repeat(ssd_d[l].astype(F32), SSD_HEAD_DIM)[None, :]
        scw = _pad_rows(sc_conv_w[l], SUBLANES)
        wc, w2bd = _compress_weights(cmp_k_w1[l], cmp_k_w2[l], cmp_v_w1[l], cmp_v_w2[l])
        pos2 = _pad_rows(jnp.concatenate([cmp_k_pos[l], cmp_v_pos[l]], axis=1).reshape(2, half_tokens * LANES),
                         SUBLANES)
        wo = w_out[l].astype(BF16)
        wu_t = _ff_tiles(ffn_w_up[l]).astype(BF16)
        cw_t = _ff_tiles(_pad_rows(ffn_conv_w[l], SUBLANES))
        cb_t = _ff_tiles(ffn_conv_b[l][None, :])
        wd = _pad_rows(ffn_w_down[l], D_FF_PAD).astype(BF16)

        z, xbc, sc, q, cpair, ks, kw, vst, vwt, dtg = _inproj(x2d, attn_norm_w[l][None, :], w_pack, ropes, s)
        kc, vct = _compress(cpair.reshape(b, nc, half_tokens * LANES), pos2, wc, w2bd)
        r3 = lambda a: a.reshape(b, s, a.shape[-1])
        ya = _ssd(r3(z), r3(xbc), r3(sc), r3(dtg), cw, cb, dtb, aneg, dsk, ssd_norm_w[l][None, :], scw)
        vt4 = lambda a: a.reshape(b, s // LANES, NSA_HEAD_DIM, LANES)
        yn = _nsa(r3(q), r3(dtg), kc, vct, r3(ks), vt4(vst), r3(kw), vt4(vwt), ovt)
        x2d = _ffn(x2d, ya.reshape(t, -1), yn.reshape(t, -1), wo, ffn_norm_w[l][None, :], wu_t, cw_t, cb_t, wd,
                   final_norm_w[None, :], s, l == depth - 1)
    return x2d.reshape(b, s, d)
```

```python
import functools
import math

import numpy as np
import jax
import jax.numpy as jnp
from jax import lax
from jax.experimental import pallas as pl
from jax.experimental.pallas import tpu as pltpu

F32 = jnp.float32
BF16 = jnp.bfloat16

D_MODEL = 1024
SSD_HEADS = 8
SSD_HEAD_DIM = 64
SSD_INNER = SSD_HEADS * SSD_HEAD_DIM
SSD_GROUPS = 2
SSD_STATE = 128
SSD_CONV = 4
SSD_CHUNK = 256
SSD_XBC = SSD_INNER + 2 * SSD_GROUPS * SSD_STATE
SC_WIDTH = 256
SC_CONV = 3
NSA_HEADS = 4
NSA_HEAD_DIM = 64
NSA_WIDTH = NSA_HEADS * NSA_HEAD_DIM
CMP_BLOCK = 32
CMP_STRIDE = 16
CMP_HIDDEN = 128
SEL_BLOCK = 64
SEL_TOPK = 16
SEL_LOCAL = 2
WINDOW = 512
ROPE_THETA = 10000.0
D_FF = 2752
FFN_CONV = 3
EPS = 1e-6
NEG = -1e30
FORCE = 1e9
REMOVED = -3.0e38
LOG2E = 1.4426950408889634

LANES = 128
SUBLANES = 8
VMEM_LIMIT_BYTES = 56 * 1024 * 1024

D_FF_PAD = 2816
FF_TILE = 256
N_FF_TILES = D_FF_PAD // FF_TILE

COL_Z = 0
COL_XBC = COL_Z + SSD_INNER
COL_SC = COL_XBC + SSD_XBC
COL_Q = COL_SC + 3 * SC_WIDTH
COL_KV = COL_Q + NSA_WIDTH
COL_DTG = COL_KV + 6 * NSA_HEAD_DIM
N_PACK = COL_DTG + LANES
GATE_COL = SSD_HEADS

INPROJ_TM = 512
NSA_TQ = 256
NSA_TK = 512
NSA_WKEYS = WINDOW + NSA_TQ
FFN_TM = 256


def _dot(a, b):
    return jnp.dot(a, b, preferred_element_type=F32)


def _dot_nt(a, b):
    return lax.dot_general(a, b, (((1,), (1,)), ((), ())), preferred_element_type=F32)


def _sigmoid(x):
    return 1.0 / (1.0 + jnp.exp(-x))


def _softplus(x):
    return jnp.maximum(x, 0.0) + jnp.log1p(jnp.exp(-jnp.abs(x)))


def _split3(a):
    a1 = a.astype(BF16)
    r1 = a - a1.astype(F32)
    a2 = r1.astype(BF16)
    r2 = r1 - a2.astype(F32)
    return a1, a2, r2.astype(BF16)


def _shift_rows(cur, tail, k):
    if k == 0:
        return cur
    rc = pltpu.roll(cur, k, 0)
    rt = pltpu.roll(tail, k, 0)
    row = lax.broadcasted_iota(jnp.int32, tail.shape, 0)
    first = jnp.where(row < k, rt, rc[0:SUBLANES])
    return jnp.concatenate([first, rc[SUBLANES:]], axis=0)


def _inproj_kernel(x_ref, nw_ref, w_ref, rqc_ref, rqs_ref, rkc_ref, rks_ref,
                   z_ref, xbc_ref, sc_ref, q_ref, cpair_ref, ks_ref, kw_ref, vst_ref, vwt_ref, dtg_ref, wb_ref):
    tm = x_ref.shape[0]

    @pl.when(pl.program_id(0) == 0)
    def _():
        wb_ref[...] = w_ref[...].astype(BF16)

    x = x_ref[...]
    ms = jnp.mean(x * x, axis=-1, keepdims=True)
    h = (x * lax.rsqrt(ms + EPS) * nw_ref[...]).astype(BF16)

    def proj(a, b):
        return _dot_nt(h, wb_ref[a:b, :])

    z_ref[...] = proj(COL_Z, COL_XBC)
    xbc_ref[...] = proj(COL_XBC, COL_SC)
    sc_ref[...] = proj(COL_SC, COL_Q)
    dtg_ref[...] = proj(COL_DTG, N_PACK)

    lane = lax.broadcasted_iota(jnp.int32, (tm, LANES), 1)
    first_half = (lane % NSA_HEAD_DIM) < (NSA_HEAD_DIM // 2)

    def rope(v, c, s):
        partner = jnp.where(first_half, pltpu.roll(v, LANES - 32, 1), pltpu.roll(v, 32, 1))
        return v * c + partner * s

    rqc, rqs, rkc, rks = rqc_ref[...], rqs_ref[...], rkc_ref[...], rks_ref[...]
    q = proj(COL_Q, COL_KV)
    q_ref[:, 0:LANES] = rope(q[:, 0:LANES], rqc, rqs)
    q_ref[:, LANES:2 * LANES] = rope(q[:, LANES:2 * LANES], rqc, rqs)

    kv = proj(COL_KV, COL_DTG)
    cpair_ref[...] = rope(kv[:, 0:LANES], rkc, rks)
    row = lax.broadcasted_iota(jnp.int32, (tm, LANES), 0)
    block_in_tile = (row % NSA_TK) // SEL_BLOCK
    onehot = jnp.where(lane - NSA_HEAD_DIM == block_in_tile, 1.0, 0.0)
    k_lanes = lane < NSA_HEAD_DIM
    for pair, k_ref, vt_ref, fill in ((1, ks_ref, vst_ref, onehot), (2, kw_ref, vwt_ref, 0.0)):
        p = rope(kv[:, pair * LANES:(pair + 1) * LANES], rkc, rks)
        k_ref[...] = jnp.where(k_lanes, p, fill).astype(BF16)
        pt = p.T
        for j in range(tm // LANES):
            vt_ref[j] = pt[NSA_HEAD_DIM:LANES, j * LANES:(j + 1) * LANES].astype(BF16)


def _inproj(x2d, nw, w_pack, ropes, seq):
    t = x2d.shape[0]
    tm = INPROJ_TM
    nt = t // tm
    pos_blocks = seq // tm
    row = lambda i: (i, 0)
    pos = lambda i: (i % pos_blocks, 0)
    const = lambda i: (0, 0)
    out_shapes = (
        jax.ShapeDtypeStruct((t, SSD_INNER), F32),
        jax.ShapeDtypeStruct((t, SSD_XBC), F32),
        jax.ShapeDtypeStruct((t, 3 * SC_WIDTH), F32),
        jax.ShapeDtypeStruct((t, NSA_WIDTH), F32),
        jax.ShapeDtypeStruct((t, LANES), F32),
        jax.ShapeDtypeStruct((t, LANES), BF16),
        jax.ShapeDtypeStruct((t, LANES), BF16),
        jax.ShapeDtypeStruct((t // LANES, NSA_HEAD_DIM, LANES), BF16),
        jax.ShapeDtypeStruct((t // LANES, NSA_HEAD_DIM, LANES), BF16),
        jax.ShapeDtypeStruct((t, LANES), F32),
    )
    vt_spec = pl.BlockSpec((tm // LANES, NSA_HEAD_DIM, LANES), lambda i: (i, 0, 0))
    out_specs = (
        pl.BlockSpec((tm, SSD_INNER), row),
        pl.BlockSpec((tm, SSD_XBC), row),
        pl.BlockSpec((tm, 3 * SC_WIDTH), row),
        pl.BlockSpec((tm, NSA_WIDTH), row),
        pl.BlockSpec((tm, LANES), row),
        pl.BlockSpec((tm, LANES), row),
        pl.BlockSpec((tm, LANES), row),
        vt_spec,
        vt_spec,
        pl.BlockSpec((tm, LANES), row),
    )
    in_specs = [
        pl.BlockSpec((tm, D_MODEL), row),
        pl.BlockSpec((1, D_MODEL), const),
        pl.BlockSpec((N_PACK, D_MODEL), const, pipeline_mode=pl.Buffered(1)),
        pl.BlockSpec((tm, LANES), pos),
        pl.BlockSpec((tm, LANES), pos),
        pl.BlockSpec((tm, LANES), pos),
        pl.BlockSpec((tm, LANES), pos),
    ]
    return pl.pallas_call(
        _inproj_kernel,
        grid=(nt,),
        in_specs=in_specs,
        out_specs=out_specs,
        out_shape=out_shapes,
        scratch_shapes=[pltpu.VMEM((N_PACK, D_MODEL), BF16)],
        compiler_params=pltpu.CompilerParams(
            dimension_semantics=("arbitrary",), vmem_limit_bytes=VMEM_LIMIT_BYTES),
        name="inproj",
    )(x2d, nw, w_pack, *ropes)


def _compress_kernel(x2_ref, pos_ref, wc_ref, w2_ref, kc_ref, vct_ref):
    nc = x2_ref.shape[1]
    x2 = x2_ref[0].astype(BF16)
    wc = wc_ref[...]
    y = _dot(x2, wc)
    r = _dot(pos_ref[...].astype(BF16), wc)
    h = CMP_HIDDEN

    def pre(base):
        bias = r[0:1, base:base + h] + r[1:2, base + h:base + 2 * h]
        return y[:, base:base + h] + pltpu.roll(y[:, base + h:base + 2 * h], nc - 1, 0) + bias

    hid = jnp.concatenate([jax.nn.gelu(pre(0)), jax.nn.gelu(pre(2 * h))], axis=1).astype(BF16)
    o = _dot(hid, w2_ref[...])
    kc_ref[0] = o.astype(BF16)
    vct_ref[0] = o.T[NSA_HEAD_DIM:LANES, :].astype(BF16)


def _compress(x2p, pos2, wc, w2bd):
    b, nc, width = x2p.shape
    return pl.pallas_call(
        _compress_kernel,
        grid=(b,),
        in_specs=[
            pl.BlockSpec((1, nc, width), lambda i: (i, 0, 0)),
            pl.BlockSpec(pos2.shape, lambda i: (0, 0)),
            pl.BlockSpec(wc.shape, lambda i: (0, 0)),
            pl.BlockSpec(w2bd.shape, lambda i: (0, 0)),
        ],
        out_specs=(
            pl.BlockSpec((1, nc, LANES), lambda i: (i, 0, 0)),
            pl.BlockSpec((1, NSA_HEAD_DIM, nc), lambda i: (i, 0, 0)),
        ),
        out_shape=(
            jax.ShapeDtypeStruct((b, nc, LANES), BF16),
            jax.ShapeDtypeStruct((b, NSA_HEAD_DIM, nc), BF16),
        ),
        compiler_params=pltpu.CompilerParams(
            dimension_semantics=("arbitrary",), vmem_limit_bytes=VMEM_LIMIT_BYTES),
        name="compress",
    )(x2p, pos2, wc, w2bd)


def _ssd_kernel(z_ref, xbc_ref, sc_ref, dtg_ref, cw_ref, cb_ref, dtb_ref, aneg_ref, dsk_ref, nw_ref, scw_ref,
                y_ref, h_ref, xtail_ref, stail_ref):
    L = SSD_CHUNK
    half = SSD_HEAD_DIM

    @pl.when(pl.program_id(1) == 0)
    def _():
        h_ref[...] = jnp.zeros_like(h_ref)
        xtail_ref[...] = jnp.zeros_like(xtail_ref)
        stail_ref[...] = jnp.zeros_like(stail_ref)

    xraw = xbc_ref[0]
    xtail = xtail_ref[...]
    conv = cb_ref[...] + cw_ref[SSD_CONV - 1:SSD_CONV, :] * xraw
    for j in range(SSD_CONV - 1):
        conv = conv + cw_ref[j:j + 1, :] * _shift_rows(xraw, xtail, SSD_CONV - 1 - j)
    xtail_ref[...] = xraw[L - SUBLANES:L]
    xc = conv * _sigmoid(conv)

    dt = _softplus(dtg_ref[0] + dtb_ref[...])
    a = dt * aneg_ref[...]
    row = lax.broadcasted_iota(jnp.int32, (L, L), 0)
    col = lax.broadcasted_iota(jnp.int32, (L, L), 1)
    causal = row >= col
    tri = jnp.where(causal, 1.0, 0.0).astype(BF16)
    a1, a2, a3 = _split3(a)
    acs = _dot(tri, a1) + _dot(tri, a2) + _dot(tri, a3)
    acs_t = acs.T
    exp_acs = jnp.exp(acs)
    last = acs[L - 1:L, :]
    dte = jnp.exp(last - acs)
    chunk_decay = jnp.exp(last)

    lo = lax.broadcasted_iota(jnp.int32, (L, LANES), 1) < half
    lo_row = lax.broadcasted_iota(jnp.int32, (1, LANES), 1) < half

    def per_lane(m, ha, hb, mask):
        return jnp.where(mask, m[:, ha:ha + 1], m[:, hb:hb + 1])

    ys = []
    for g in range(SSD_GROUPS):
        b_g = xc[:, SSD_INNER + g * SSD_STATE:SSD_INNER + (g + 1) * SSD_STATE]
        c_g = xc[:, SSD_INNER + (SSD_GROUPS + g) * SSD_STATE:SSD_INNER + (SSD_GROUPS + g + 1) * SSD_STATE]
        b_bf = b_g.astype(BF16)
        c_bf = c_g.astype(BF16)
        cb = _dot_nt(c_bf, b_bf)
        bt_bf = b_g.T.astype(BF16)
        for pp in range(SSD_HEADS // SSD_GROUPS // 2):
            p = g * (SSD_HEADS // SSD_GROUPS // 2) + pp
            ha, hb = 2 * p, 2 * p + 1
            x_pair = xc[:, p * LANES:(p + 1) * LANES]
            xdt = x_pair * per_lane(dt, ha, hb, lo)
            xdt_bf = xdt.astype(BF16)
            yd = []
            for hd in (ha, hb):
                seg = acs[:, hd:hd + 1] - acs_t[hd:hd + 1, :]
                decay = jnp.exp(jnp.where(causal, seg, NEG))
                yd.append(_dot((decay * cb).astype(BF16), xdt_bf))
            y_diag = jnp.where(lo, yd[0], yd[1])
            st = _dot(bt_bf, (xdt * per_lane(dte, ha, hb, lo)).astype(BF16))
            h_prev = h_ref[p]
            y_off = _dot(c_bf, h_prev.astype(BF16)) * per_lane(exp_acs, ha, hb, lo)
            h_ref[p] = h_prev * per_lane(chunk_decay, ha, hb, lo_row) + st
            ys.append(y_diag + y_off + x_pair * dsk_ref[:, p * LANES:(p + 1) * LANES])
    y = jnp.concatenate(ys, axis=1)
    z = z_ref[0]
    y = y * (z * _sigmoid(z))
    ms = jnp.mean(y * y, axis=-1, keepdims=True)
    y_ref[0, :, 0:SSD_INNER] = y * lax.rsqrt(ms + EPS) * nw_ref[...]

    sc = sc_ref[0]
    u = sc[:, SC_WIDTH:2 * SC_WIDTH] * sc[:, 2 * SC_WIDTH:3 * SC_WIDTH]
    stail = stail_ref[...]
    cv = scw_ref[SC_CONV - 1:SC_CONV, :] * u
    for j in range(SC_CONV - 1):
        cv = cv + scw_ref[j:j + 1, :] * _shift_rows(u, stail, SC_CONV - 1 - j)
    stail_ref[...] = u[L - SUBLANES:L]
    y_ref[0, :, SSD_INNER:SSD_INNER + SC_WIDTH] = sc[:, 0:SC_WIDTH] * cv


def _ssd(z, xbc, sc, dtg, cw, cb, dtb, aneg, dsk, nw, scw):
    b, s, _ = z.shape
    L = SSD_CHUNK
    blk = lambda w: pl.BlockSpec((1, L, w), lambda i, c: (i, c, 0))
    par = lambda a: pl.BlockSpec(a.shape, lambda i, c: (0, 0))
    return pl.pallas_call(
        _ssd_kernel,
        grid=(b, s // L),
        in_specs=[blk(SSD_INNER), blk(SSD_XBC), blk(3 * SC_WIDTH), blk(LANES),
                  par(cw), par(cb), par(dtb), par(aneg), par(dsk), par(nw), par(scw)],
        out_specs=blk(SSD_INNER + SC_WIDTH),
        out_shape=jax.ShapeDtypeStruct((b, s, SSD_INNER + SC_WIDTH), F32),
        scratch_shapes=[
            pltpu.VMEM((SSD_HEADS // 2, SSD_STATE, LANES), F32),
            pltpu.VMEM((SUBLANES, SSD_XBC), F32),
            pltpu.VMEM((SUBLANES, SC_WIDTH), F32),
        ],
        compiler_params=pltpu.CompilerParams(
            dimension_semantics=("arbitrary", "arbitrary"), vmem_limit_bytes=VMEM_LIMIT_BYTES),
        name="ssd_sc",
    )(z, xbc, sc, dtg, cw, cb, dtb, aneg, dsk, nw, scw)


def _nsa_kernel(q_ref, dtg_ref, kc_ref, vct_ref, ks_ref, vst_ref, kw_ref, vwt_ref, ovt_ref,
                o_ref, sel_ref, qa_ref, acc_ref, m_ref, l_ref, s_ref, *, seq):
    tq, tk = NSA_TQ, NSA_TK
    hd = NSA_HEAD_DIM
    nh = NSA_HEADS
    nc = seq // CMP_STRIDE
    nb = seq // SEL_BLOCK
    q0 = pl.program_id(1) * tq
    tpos = q0 + lax.broadcasted_iota(jnp.int32, (1, tq), 1)

    def heads_on_lanes(a):
        return jnp.concatenate([a] * nh, axis=1)

    qt = q_ref[0].T
    qa_ref[0:hd, :] = jnp.concatenate([qt[h * hd:(h + 1) * hd, :] for h in range(nh)], axis=1).astype(BF16)
    qa_ref[hd:LANES, :] = jnp.zeros((LANES - hd, nh * tq), BF16)
    gates = _sigmoid(dtg_ref[0]).T

    n_io = lax.broadcasted_iota(jnp.int32, (nc, tq), 0)
    cbias = jnp.where((n_io * CMP_STRIDE + (CMP_BLOCK - 1)) <= tpos, 0.0, NEG)
    s = _dot(kc_ref[0], qa_ref[...]) + heads_on_lanes(cbias)
    p = jnp.exp2(s - jnp.max(s, axis=0, keepdims=True))
    has_key = heads_on_lanes(jnp.where(tpos >= CMP_BLOCK - 1, 1.0, 0.0))
    p = p * (has_key / jnp.sum(p, axis=0, keepdims=True))
    o_cmp = _dot(vct_ref[0], p.astype(BF16))
    psum = p[:, 0:tq]
    for h in range(1, nh):
        psum = psum + p[:, h * tq:(h + 1) * tq]

    def load_vt(ref, key0, width):
        first = key0 // LANES
        return jnp.concatenate([ref[0, first + i] for i in range(width // LANES)], axis=1)

    ovt = ovt_ref[...]
    p1, p2, p3 = _split3(psum)
    imp = _dot(ovt, p1) + _dot(ovt, p2) + _dot(ovt, p3)
    j_io = lax.broadcasted_iota(jnp.int32, (nb, tq), 0)
    cur = jnp.right_shift(tpos, int(math.log2(SEL_BLOCK)))
    valid = j_io <= cur
    forced = (j_io == 0) | (valid & (j_io > cur - SEL_LOCAL))
    v = jnp.where(forced, REMOVED, jnp.where(valid, imp, NEG))
    j_f = j_io.astype(F32)

    def extract(_, carry):
        v, sel = carry
        m = jnp.max(v, axis=0, keepdims=True)
        first = jnp.min(jnp.where(v == m, j_f, float(nb)), axis=0, keepdims=True)
        hit = j_f == first
        return jnp.where(hit, REMOVED, v), jnp.where(hit, 1.0, sel)

    _, sel = lax.fori_loop(0, SEL_TOPK - 1 - SEL_LOCAL, extract, (v, jnp.where(forced, 1.0, 0.0)), unroll=True)
    sel_ref[...] = sel

    wk = NSA_WKEYS
    w0 = pl.multiple_of(jnp.maximum(q0 - WINDOW, 0), tq)
    d = (w0 - q0) + (lax.broadcasted_iota(jnp.int32, (wk, tq), 0)
                     - lax.broadcasted_iota(jnp.int32, (wk, tq), 1))
    wbias = jnp.where(d <= 0, jnp.where(d > -WINDOW, 0.0, NEG), NEG)
    s = _dot(kw_ref[0, pl.ds(w0, wk), :], qa_ref[...]) + heads_on_lanes(wbias)
    p = jnp.exp2(s - jnp.max(s, axis=0, keepdims=True))
    o_win = _dot(load_vt(vwt_ref, w0, wk), p.astype(BF16)) / jnp.sum(p, axis=0, keepdims=True)

    blocks_per_tile = tk // SEL_BLOCK
    bias_rows = 2 * SUBLANES
    acc_ref[...] = jnp.zeros_like(acc_ref)
    m_ref[...] = jnp.full(m_ref.shape, NEG, F32)
    l_ref[...] = jnp.zeros_like(l_ref)

    def scores(buf, kt):
        k0 = pl.multiple_of(kt * tk, tk)
        chunk = sel_ref[pl.ds(pl.multiple_of(kt * blocks_per_tile, blocks_per_tile), blocks_per_tile), :]
        bias = jnp.concatenate([(1.0 - chunk) * NEG, jnp.zeros((bias_rows - blocks_per_tile, tq), F32)], axis=0)
        qa_ref[hd:hd + bias_rows, :] = heads_on_lanes(bias).astype(BF16)
        s_ref[buf] = _dot(ks_ref[0, pl.ds(k0, tk), :], qa_ref[...])

    def absorb(buf, kt, causal):
        k0 = pl.multiple_of(kt * tk, tk)
        s = s_ref[buf]
        if causal:
            kpos = k0 + lax.broadcasted_iota(jnp.int32, (tk, tq), 0)
            s = s + heads_on_lanes(jnp.where(kpos <= tpos, 0.0, NEG))
        m_old = m_ref[...]
        m_new = jnp.maximum(m_old, jnp.max(s, axis=0, keepdims=True))
        alpha = jnp.exp2(m_old - m_new)
        p = jnp.exp2(s - m_new)
        acc_ref[...] = alpha * acc_ref[...] + _dot(load_vt(vst_ref, k0, tk), p.astype(BF16))
        l_ref[...] = alpha * l_ref[...] + jnp.sum(p, axis=0, keepdims=True)
        m_ref[...] = m_new

    n_last = (q0 + tq + tk - 1) // tk - 1
    odd = n_last % 2

    @pl.when(odd == 1)
    def _():
        scores(0, 0)
        absorb(0, 0, False)

    scores(0, odd)

    def pair(j, _):
        kt = odd + 2 * j
        scores(1, kt + 1)
        absorb(0, kt, False)
        scores(0, kt + 2)
        absorb(1, kt + 1, False)
        return 0

    lax.fori_loop(0, (n_last - odd) // 2, pair, 0)
    absorb(0, n_last, True)
    o_sel = acc_ref[...] / l_ref[...]

    def gate_row(branch):
        return jnp.concatenate([gates[GATE_COL + 3 * h + branch:GATE_COL + 3 * h + branch + 1, :]
                                for h in range(nh)], axis=1)

    out = o_cmp * gate_row(0) + o_sel * gate_row(1) + o_win * gate_row(2)
    o_ref[0] = jnp.concatenate([out[:, h * tq:(h + 1) * tq] for h in range(nh)], axis=0).T


def _nsa(q, dtg, kc, vct, ks, vst, kw, vwt, ovt):
    b, s, _ = q.shape
    tq = NSA_TQ
    nc = s // CMP_STRIDE
    per_b3 = lambda shape: pl.BlockSpec((1,) + shape, lambda i, j: (i, 0, 0))
    per_b4 = lambda shape: pl.BlockSpec((1,) + shape, lambda i, j: (i, 0, 0, 0))
    qblk = lambda w: pl.BlockSpec((1, tq, w), lambda i, j: (i, j, 0))
    return pl.pallas_call(
        functools.partial(_nsa_kernel, seq=s),
        grid=(b, s // tq),
        in_specs=[
            qblk(NSA_WIDTH), qblk(LANES),
            per_b3((nc, LANES)), per_b3((NSA_HEAD_DIM, nc)),
            per_b3((s, LANES)), per_b4((s // LANES, NSA_HEAD_DIM, LANES)),
            per_b3((s, LANES)), per_b4((s // LANES, NSA_HEAD_DIM, LANES)),
            pl.BlockSpec(ovt.shape, lambda i, j: (0, 0)),
        ],
        out_specs=qblk(NSA_WIDTH),
        out_shape=jax.ShapeDtypeStruct((b, s, NSA_WIDTH), F32),
        scratch_shapes=[
            pltpu.VMEM((s // SEL_BLOCK, tq), F32),
            pltpu.VMEM((LANES, NSA_HEADS * tq), BF16),
            pltpu.VMEM((NSA_HEAD_DIM, NSA_HEADS * tq), F32),
            pltpu.VMEM((1, NSA_HEADS * tq), F32),
            pltpu.VMEM((1, NSA_HEADS * tq), F32),
            pltpu.VMEM((2, NSA_TK, NSA_HEADS * tq), F32),
        ],
        compiler_params=pltpu.CompilerParams(
            dimension_semantics=("arbitrary", "arbitrary"), vmem_limit_bytes=VMEM_LIMIT_BYTES),
        name="nsa",
    )(q, dtg, kc, vct, ks, vst, kw, vwt, ovt)


def _ffn_kernel(x_ref, ya_ref, yn_ref, wo_ref, nw_ref, wu_ref, cw_ref, cb_ref, wd_ref, fw_ref,
                o_ref, tail_ref, act_ref, *, tiles_per_seq, final_norm):
    tm = x_ref.shape[0]
    na = ya_ref.shape[1]

    @pl.when(pl.program_id(0) % tiles_per_seq == 0)
    def _():
        tail_ref[...] = jnp.zeros_like(tail_ref)

    x1 = (x_ref[...] + _dot(ya_ref[...].astype(BF16), wo_ref[0:na, :])
          + _dot(yn_ref[...].astype(BF16), wo_ref[na:, :]))
    ms = jnp.mean(x1 * x1, axis=-1, keepdims=True)
    h = (x1 * lax.rsqrt(ms + EPS) * nw_ref[...]).astype(BF16)

    def conv_cols(c0):
        cols = slice(c0, c0 + FF_TILE)
        u = _dot(h, wu_ref[:, cols])
        tail = tail_ref[:, cols]
        cv = cb_ref[:, cols] + cw_ref[FFN_CONV - 1:FFN_CONV, cols] * u
        for t in range(FFN_CONV - 1):
            cv = cv + cw_ref[t:t + 1, cols] * _shift_rows(u, tail, FFN_CONV - 1 - t)
        tail_ref[:, cols] = u[tm - SUBLANES:tm]
        return cv

    for j in range(N_FF_TILES):
        gate = conv_cols(j * FF_TILE)
        val = conv_cols(D_FF_PAD + j * FF_TILE)
        act_ref[:, j * FF_TILE:(j + 1) * FF_TILE] = (gate * _sigmoid(gate) * val).astype(BF16)

    x2 = x1 + _dot(act_ref[...], wd_ref[...])
    if final_norm:
        ms2 = jnp.mean(x2 * x2, axis=-1, keepdims=True)
        x2 = x2 * lax.rsqrt(ms2 + EPS) * fw_ref[...]
    o_ref[...] = x2


def _ffn(x2d, ya, yn, wo, nw, wu_t, cw_t, cb_t, wd, fw, seq, final_norm):
    t = x2d.shape[0]
    tm = FFN_TM
    row = lambda i: (i, 0)
    c2 = lambda a: pl.BlockSpec(a.shape, lambda i: (0, 0))
    return pl.pallas_call(
        functools.partial(_ffn_kernel, tiles_per_seq=seq // tm, final_norm=final_norm),
        grid=(t // tm,),
        in_specs=[
            pl.BlockSpec((tm, D_MODEL), row),
            pl.BlockSpec((tm, ya.shape[1]), row),
            pl.BlockSpec((tm, yn.shape[1]), row),
            c2(wo), c2(nw), c2(wu_t), c2(cw_t), c2(cb_t), c2(wd), c2(fw),
        ],
        out_specs=pl.BlockSpec((tm, D_MODEL), row),
        out_shape=jax.ShapeDtypeStruct((t, D_MODEL), F32),
        scratch_shapes=[
            pltpu.VMEM((SUBLANES, 2 * D_FF_PAD), F32),
            pltpu.VMEM((tm, D_FF_PAD), BF16),
        ],
        compiler_params=pltpu.CompilerParams(
            dimension_semantics=("arbitrary",), vmem_limit_bytes=VMEM_LIMIT_BYTES),
        name="outproj_ffn",
    )(x2d, ya, yn, wo, nw, wu_t, cw_t, cb_t, wd, fw)


def _pack_w_in(w):
    wt = jnp.transpose(w, (0, 2, 1))
    dt0 = SSD_INNER + SSD_XBC
    dt1 = dt0 + SSD_HEADS
    g0 = dt1 + 3 * SC_WIDTH + NSA_WIDTH + 6 * NSA_HEAD_DIM
    g1 = g0 + 3 * NSA_HEADS
    pad = jnp.zeros((w.shape[0], LANES - SSD_HEADS - 3 * NSA_HEADS, w.shape[1]), w.dtype)
    return jnp.concatenate([wt[:, :dt0], wt[:, dt1:g0], wt[:, dt0:dt1], wt[:, g0:g1], pad], axis=1)


def _rope_tables(seq):
    half = NSA_HEAD_DIM // 2
    inv = 1.0 / (ROPE_THETA ** (jnp.arange(half, dtype=F32) / half))
    ang = jnp.arange(seq, dtype=F32)[:, None] * inv[None, :]
    cos, sin = jnp.cos(ang), jnp.sin(ang)
    scale = NSA_HEAD_DIM ** -0.5 * LOG2E
    one = jnp.ones_like(cos)
    zero = jnp.zeros_like(cos)
    rqc = jnp.concatenate([cos, cos, cos, cos], axis=1) * scale
    rqs = jnp.concatenate([-sin, sin, -sin, sin], axis=1) * scale
    rkc = jnp.concatenate([cos, cos, one, one], axis=1)
    rks = jnp.concatenate([-sin, sin, zero, zero], axis=1)
    return rqc, rqs, rkc, rks


def _compress_weights(kw1, kw2, vw1, vw2):
    half_tokens = CMP_BLOCK // 2
    hd, hid = NSA_HEAD_DIM, CMP_HIDDEN
    kw1r = kw1.reshape(2, half_tokens, hd, hid)
    vw1r = vw1.reshape(2, half_tokens, hd, hid)
    zeros = jnp.zeros((half_tokens, hd, hid), kw1.dtype)
    cols = []
    for w1r, is_k in ((kw1r, True), (vw1r, False)):
        for part in range(2):
            blk = w1r[part]
            rows = jnp.concatenate([blk, zeros] if is_k else [zeros, blk], axis=1)
            cols.append(rows.reshape(half_tokens * LANES, hid))
    wc = jnp.concatenate(cols, axis=1).astype(BF16)
    zk = jnp.zeros((hid, hd), kw2.dtype)
    w2bd = jnp.concatenate([jnp.concatenate([kw2, zk], axis=1),
                            jnp.concatenate([zk, vw2], axis=1)], axis=0).astype(BF16)
    return wc, w2bd


def _overlap_t(seq):
    nc = seq // CMP_STRIDE
    nb = seq // SEL_BLOCK
    cmp_start = np.arange(nc) * CMP_STRIDE
    slc_start = np.arange(nb) * SEL_BLOCK
    ov = ((cmp_start[None, :] < slc_start[:, None] + SEL_BLOCK)
          & (cmp_start[None, :] + CMP_BLOCK > slc_start[:, None])
          & (np.arange(nc)[None, :] < nc - 1))
    return jnp.asarray(ov.astype(np.float32), dtype=BF16)


def _pad_rows(a, rows):
    return jnp.concatenate([a, jnp.zeros((rows - a.shape[0],) + a.shape[1:], a.dtype)], axis=0)


def _pad_cols(a, cols):
    return jnp.concatenate([a, jnp.zeros(a.shape[:-1] + (cols - a.shape[-1],), a.dtype)], axis=-1)


def _ff_tiles(a):
    return jnp.concatenate([_pad_cols(a[:, :D_FF], D_FF_PAD), _pad_cols(a[:, D_FF:], D_FF_PAD)], axis=1)


def kernel(x, attn_norm_w, w_in, ssd_conv_w, ssd_conv_b, ssd_dt_bias, ssd_a_log, ssd_d, ssd_norm_w, sc_conv_w,
           cmp_k_pos, cmp_k_w1, cmp_k_w2, cmp_v_pos, cmp_v_w1, cmp_v_w2, w_out, ffn_norm_w, ffn_w_up,
           ffn_conv_w, ffn_conv_b, ffn_w_down, final_norm_w):
    b, s, d = x.shape
    depth = w_in.shape[0]
    assert d == D_MODEL and s % NSA_TK == 0 and s // SEL_BLOCK >= SEL_TOPK and s % INPROJ_TM == 0
    assert INPROJ_TM % NSA_TK == 0 and s >= NSA_WKEYS and NSA_TK // SEL_BLOCK <= SUBLANES
    t = b * s
    ropes = _rope_tables(s)
    ovt = _overlap_t(s)
    nc = s // CMP_STRIDE
    half_tokens = CMP_BLOCK // 2
    x2d = x.reshape(t, d)
    w_pack_all = _pack_w_in(w_in)
    for l in range(depth):
        w_pack = w_pack_all[l]
        cw = _pad_rows(ssd_conv_w[l], SUBLANES)
        cb = ssd_conv_b[l][None, :]
        dtb = _pad_cols(ssd_dt_bias[l][None, :], LANES)
        aneg = _pad_cols(-jnp.exp(ssd_a_log[l].astype(F32))[None, :], LANES)
        dsk = jnp.repeat(ssd_d[l].astype(F32), SSD_HEAD_DIM)[None, :]
        scw = _pad_rows(sc_conv_w[l], SUBLANES)
        wc, w2bd = _compress_weights(cmp_k_w1[l], cmp_k_w2[l], cmp_v_w1[l], cmp_v_w2[l])
        pos2 = _pad_rows(jnp.concatenate([cmp_k_pos[l], cmp_v_pos[l]], axis=1).reshape(2, half_tokens * LANES),
                         SUBLANES)
        wo = w_out[l].astype(BF16)
        wu_t = _ff_tiles(ffn_w_up[l]).astype(BF16)
        cw_t = _ff_tiles(_pad_rows(ffn_conv_w[l], SUBLANES))
        cb_t = _ff_tiles(ffn_conv_b[l][None, :])
        wd = _pad_rows(ffn_w_down[l], D_FF_PAD).astype(BF16)

        z, xbc, sc, q, cpair, ks, kw, vst, vwt, dtg = _inproj(x2d, attn_norm_w[l][None, :], w_pack, ropes, s)
        kc, vct = _compress(cpair.reshape(b, nc, half_tokens * LANES), pos2, wc, w2bd)
        r3 = lambda a: a.reshape(b, s, a.shape[-1])
        ya = _ssd(r3(z), r3(xbc), r3(sc), r3(dtg), cw, cb, dtb, aneg, dsk, ssd_norm_w[l][None, :], scw)
        vt4 = lambda a: a.reshape(b, s // LANES, NSA_HEAD_DIM, LANES)
        yn = _nsa(r3(q), r3(dtg), kc, vct, r3(ks), vt4(vst), r3(kw), vt4(vwt), ovt)
        x2d = _ffn(x2d, ya.reshape(t, -1), yn.reshape(t, -1), wo, ffn_norm_w[l][None, :], wu_t, cw_t, cb_t, wd,
                   final_norm_w[None, :], s, l == depth - 1)
    return x2d.reshape(b, s, d)
```

```python
import functools
import math

import numpy as np
import jax
import jax.numpy as jnp
from jax import lax
from jax.experimental import pallas as pl
from jax.experimental.pallas import tpu as pltpu

F32 = jnp.float32
BF16 = jnp.bfloat16

D_MODEL = 1024
SSD_HEADS = 8
SSD_HEAD_DIM = 64
SSD_INNER = SSD_HEADS * SSD_HEAD_DIM
SSD_GROUPS = 2
SSD_STATE = 128
SSD_CONV = 4
SSD_CHUNK = 256
SSD_XBC = SSD_INNER + 2 * SSD_GROUPS * SSD_STATE
SC_WIDTH = 256
SC_CONV = 3
NSA_HEADS = 4
NSA_HEAD_DIM = 64
NSA_WIDTH = NSA_HEADS * NSA_HEAD_DIM
CMP_BLOCK = 32
CMP_STRIDE = 16
CMP_HIDDEN = 128
SEL_BLOCK = 64
SEL_TOPK = 16
SEL_LOCAL = 2
WINDOW = 512
ROPE_THETA = 10000.0
D_FF = 2752
FFN_CONV = 3
EPS = 1e-6
NEG = -1e30
FORCE = 1e9
REMOVED = -3.0e38
LOG2E = 1.4426950408889634

LANES = 128
SUBLANES = 8
VMEM_LIMIT_BYTES = 56 * 1024 * 1024

D_FF_PAD = 2816
FF_TILE = 256
N_FF_TILES = D_FF_PAD // FF_TILE

COL_Z = 0
COL_XBC = COL_Z + SSD_INNER
COL_SC = COL_XBC + SSD_XBC
COL_Q = COL_SC + 3 * SC_WIDTH
COL_KV = COL_Q + NSA_WIDTH
COL_DTG = COL_KV + 6 * NSA_HEAD_DIM
N_PACK = COL_DTG + LANES
GATE_COL = SSD_HEADS

INPROJ_TM = 512
NSA_TQ = 256
NSA_TK = 512
NSA_WKEYS = WINDOW + NSA_TQ
FFN_TM = 256


def _dot(a, b):
    return jnp.dot(a, b, preferred_element_type=F32)


def _dot_nt(a, b):
    return lax.dot_general(a, b, (((1,), (1,)), ((), ())), preferred_element_type=F32)


def _sigmoid(x):
    return 1.0 / (1.0 + jnp.exp(-x))


def _silu(x):
    h = 0.5 * x
    return h + h * jnp.tanh(h)


def _softplus(x):
    return jnp.maximum(x, 0.0) + jnp.log1p(jnp.exp(-jnp.abs(x)))


def _split3(a):
    a1 = a.astype(BF16)
    r1 = a - a1.astype(F32)
    a2 = r1.astype(BF16)
    r2 = r1 - a2.astype(F32)
    return a1, a2, r2.astype(BF16)


def _shift_rows(cur, tail, k):
    if k == 0:
        return cur
    rc = pltpu.roll(cur, k, 0)
    rt = pltpu.roll(tail, k, 0)
    row = lax.broadcasted_iota(jnp.int32, tail.shape, 0)
    first = jnp.where(row < k, rt, rc[0:SUBLANES])
    return jnp.concatenate([first, rc[SUBLANES:]], axis=0)


def _inproj_kernel(x_ref, nw_ref, w_ref, rqc_ref, rqs_ref, rkc_ref, rks_ref,
                   z_ref, xbc_ref, sc_ref, q_ref, cpair_ref, ks_ref, kw_ref, vst_ref, vwt_ref, dtg_ref, wb_ref):
    tm = x_ref.shape[0]

    @pl.when(pl.program_id(0) == 0)
    def _():
        wb_ref[...] = w_ref[...].astype(BF16)

    x = x_ref[...]
    ms = jnp.mean(x * x, axis=-1, keepdims=True)
    h = (x * lax.rsqrt(ms + EPS) * nw_ref[...]).astype(BF16)

    def proj(a, b):
        return _dot_nt(h, wb_ref[a:b, :])

    z_ref[...] = proj(COL_Z, COL_XBC)
    xbc_ref[...] = proj(COL_XBC, COL_SC)
    sc_ref[...] = proj(COL_SC, COL_Q)
    dtg_ref[...] = proj(COL_DTG, N_PACK)

    lane = lax.broadcasted_iota(jnp.int32, (tm, LANES), 1)
    first_half = (lane % NSA_HEAD_DIM) < (NSA_HEAD_DIM // 2)

    def rope(v, c, s):
        partner = jnp.where(first_half, pltpu.roll(v, LANES - 32, 1), pltpu.roll(v, 32, 1))
        return v * c + partner * s

    rqc, rqs, rkc, rks = rqc_ref[...], rqs_ref[...], rkc_ref[...], rks_ref[...]
    q = proj(COL_Q, COL_KV)
    q_ref[:, 0:LANES] = rope(q[:, 0:LANES], rqc, rqs)
    q_ref[:, LANES:2 * LANES] = rope(q[:, LANES:2 * LANES], rqc, rqs)

    kv = proj(COL_KV, COL_DTG)
    cpair_ref[...] = rope(kv[:, 0:LANES], rkc, rks)
    row = lax.broadcasted_iota(jnp.int32, (tm, LANES), 0)
    block_in_tile = (row % NSA_TK) // SEL_BLOCK
    onehot = jnp.where(lane - NSA_HEAD_DIM == block_in_tile, 1.0, 0.0)
    k_lanes = lane < NSA_HEAD_DIM
    for pair, k_ref, vt_ref, fill in ((1, ks_ref, vst_ref, onehot), (2, kw_ref, vwt_ref, 0.0)):
        p = rope(kv[:, pair * LANES:(pair + 1) * LANES], rkc, rks)
        k_ref[...] = jnp.where(k_lanes, p, fill).astype(BF16)
        pt = p.T
        for j in range(tm // LANES):
            vt_ref[j] = pt[NSA_HEAD_DIM:LANES, j * LANES:(j + 1) * LANES].astype(BF16)


def _inproj(x2d, nw, w_pack, ropes, seq):
    t = x2d.shape[0]
    tm = INPROJ_TM
    nt = t // tm
    pos_blocks = seq // tm
    row = lambda i: (i, 0)
    pos = lambda i: (i % pos_blocks, 0)
    const = lambda i: (0, 0)
    out_shapes = (
        jax.ShapeDtypeStruct((t, SSD_INNER), F32),
        jax.ShapeDtypeStruct((t, SSD_XBC), F32),
        jax.ShapeDtypeStruct((t, 3 * SC_WIDTH), F32),
        jax.ShapeDtypeStruct((t, NSA_WIDTH), F32),
        jax.ShapeDtypeStruct((t, LANES), F32),
        jax.ShapeDtypeStruct((t, LANES), BF16),
        jax.ShapeDtypeStruct((t, LANES), BF16),
        jax.ShapeDtypeStruct((t // LANES, NSA_HEAD_DIM, LANES), BF16),
        jax.ShapeDtypeStruct((t // LANES, NSA_HEAD_DIM, LANES), BF16),
        jax.ShapeDtypeStruct((t, LANES), F32),
    )
    vt_spec = pl.BlockSpec((tm // LANES, NSA_HEAD_DIM, LANES), lambda i: (i, 0, 0))
    out_specs = (
        pl.BlockSpec((tm, SSD_INNER), row),
        pl.BlockSpec((tm, SSD_XBC), row),
        pl.BlockSpec((tm, 3 * SC_WIDTH), row),
        pl.BlockSpec((tm, NSA_WIDTH), row),
        pl.BlockSpec((tm, LANES), row),
        pl.BlockSpec((tm, LANES), row),
        pl.BlockSpec((tm, LANES), row),
        vt_spec,
        vt_spec,
        pl.BlockSpec((tm, LANES), row),
    )
    in_specs = [
        pl.BlockSpec((tm, D_MODEL), row),
        pl.BlockSpec((1, D_MODEL), const),
        pl.BlockSpec((N_PACK, D_MODEL), const, pipeline_mode=pl.Buffered(1)),
        pl.BlockSpec((tm, LANES), pos),
        pl.BlockSpec((tm, LANES), pos),
        pl.BlockSpec((tm, LANES), pos),
        pl.BlockSpec((tm, LANES), pos),
    ]
    return pl.pallas_call(
        _inproj_kernel,
        grid=(nt,),
        in_specs=in_specs,
        out_specs=out_specs,
        out_shape=out_shapes,
        scratch_shapes=[pltpu.VMEM((N_PACK, D_MODEL), BF16)],
        compiler_params=pltpu.CompilerParams(
            dimension_semantics=("arbitrary",), vmem_limit_bytes=VMEM_LIMIT_BYTES),
        name="inproj",
    )(x2d, nw, w_pack, *ropes)


def _compress_kernel(x2_ref, pos_ref, wc_ref, w2_ref, kc_ref, vct_ref):
    nc = x2_ref.shape[1]
    x2 = x2_ref[0].astype(BF16)
    wc = wc_ref[...]
    y = _dot(x2, wc)
    r = _dot(pos_ref[...].astype(BF16), wc)
    h = CMP_HIDDEN

    def pre(base):
        bias = r[0:1, base:base + h] + r[1:2, base + h:base + 2 * h]
        return y[:, base:base + h] + pltpu.roll(y[:, base + h:base + 2 * h], nc - 1, 0) + bias

    hid = jnp.concatenate([jax.nn.gelu(pre(0)), jax.nn.gelu(pre(2 * h))], axis=1).astype(BF16)
    o = _dot(hid, w2_ref[...])
    kc_ref[0] = o.astype(BF16)
    vct_ref[0] = o.T[NSA_HEAD_DIM:LANES, :].astype(BF16)


def _compress(x2p, pos2, wc, w2bd):
    b, nc, width = x2p.shape
    return pl.pallas_call(
        _compress_kernel,
        grid=(b,),
        in_specs=[
            pl.BlockSpec((1, nc, width), lambda i: (i, 0, 0)),
            pl.BlockSpec(pos2.shape, lambda i: (0, 0)),
            pl.BlockSpec(wc.shape, lambda i: (0, 0)),
            pl.BlockSpec(w2bd.shape, lambda i: (0, 0)),
        ],
        out_specs=(
            pl.BlockSpec((1, nc, LANES), lambda i: (i, 0, 0)),
            pl.BlockSpec((1, NSA_HEAD_DIM, nc), lambda i: (i, 0, 0)),
        ),
        out_shape=(
            jax.ShapeDtypeStruct((b, nc, LANES), BF16),
            jax.ShapeDtypeStruct((b, NSA_HEAD_DIM, nc), BF16),
        ),
        compiler_params=pltpu.CompilerParams(
            dimension_semantics=("arbitrary",), vmem_limit_bytes=VMEM_LIMIT_BYTES),
        name="compress",
    )(x2p, pos2, wc, w2bd)


def _ssd_kernel(z_ref, xbc_ref, sc_ref, dtg_ref, cw_ref, cb_ref, dtb_ref, aneg_ref, dsk_ref, nw_ref, scw_ref,
                y_ref, h_ref, xtail_ref, stail_ref):
    L = SSD_CHUNK
    half = SSD_HEAD_DIM

    @pl.when(pl.program_id(1) == 0)
    def _():
        h_ref[...] = jnp.zeros_like(h_ref)
        xtail_ref[...] = jnp.zeros_like(xtail_ref)
        stail_ref[...] = jnp.zeros_like(stail_ref)

    xraw = xbc_ref[0]
    xtail = xtail_ref[...]
    conv = cb_ref[...] + cw_ref[SSD_CONV - 1:SSD_CONV, :] * xraw
    for j in range(SSD_CONV - 1):
        conv = conv + cw_ref[j:j + 1, :] * _shift_rows(xraw, xtail, SSD_CONV - 1 - j)
    xtail_ref[...] = xraw[L - SUBLANES:L]
    xc = _silu(conv)

    dt = _softplus(dtg_ref[0] + dtb_ref[...])
    a = dt * aneg_ref[...]
    row = lax.broadcasted_iota(jnp.int32, (L, L), 0)
    col = lax.broadcasted_iota(jnp.int32, (L, L), 1)
    causal = row >= col
    tri = jnp.where(causal, 1.0, 0.0).astype(BF16)
    a1, a2, a3 = _split3(a)
    acs = _dot(tri, a1) + _dot(tri, a2) + _dot(tri, a3)
    acs_t = acs.T
    exp_acs = jnp.exp2(acs)
    last = acs[L - 1:L, :]
    dte = jnp.exp2(last - acs)
    chunk_decay = jnp.exp2(last)

    lo = lax.broadcasted_iota(jnp.int32, (L, LANES), 1) < half
    lo_row = lax.broadcasted_iota(jnp.int32, (1, LANES), 1) < half

    def per_lane(m, ha, hb, mask):
        return jnp.where(mask, m[:, ha:ha + 1], m[:, hb:hb + 1])

    ys = []
    for g in range(SSD_GROUPS):
        b_g = xc[:, SSD_INNER + g * SSD_STATE:SSD_INNER + (g + 1) * SSD_STATE]
        c_g = xc[:, SSD_INNER + (SSD_GROUPS + g) * SSD_STATE:SSD_INNER + (SSD_GROUPS + g + 1) * SSD_STATE]
        b_bf = b_g.astype(BF16)
        c_bf = c_g.astype(BF16)
        cb = _dot_nt(c_bf, b_bf)
        bt_bf = b_g.T.astype(BF16)
        for pp in range(SSD_HEADS // SSD_GROUPS // 2):
            p = g * (SSD_HEADS // SSD_GROUPS // 2) + pp
            ha, hb = 2 * p, 2 * p + 1
            x_pair = xc[:, p * LANES:(p + 1) * LANES]
            xdt = x_pair * per_lane(dt, ha, hb, lo)
            xdt_bf = xdt.astype(BF16)
            yd = []
            for hd in (ha, hb):
                seg = acs[:, hd:hd + 1] - acs_t[hd:hd + 1, :]
                decay = jnp.exp2(jnp.where(causal, seg, NEG))
                yd.append(_dot((decay * cb).astype(BF16), xdt_bf))
            y_diag = jnp.where(lo, yd[0], yd[1])
            st = _dot(bt_bf, (xdt * per_lane(dte, ha, hb, lo)).astype(BF16))
            h_prev = h_ref[p]
            y_off = _dot(c_bf, h_prev.astype(BF16)) * per_lane(exp_acs, ha, hb, lo)
            h_ref[p] = h_prev * per_lane(chunk_decay, ha, hb, lo_row) + st
            ys.append(y_diag + y_off + x_pair * dsk_ref[:, p * LANES:(p + 1) * LANES])
    y = jnp.concatenate(ys, axis=1)
    z = z_ref[0]
    y = y * _silu(z)
    ms = jnp.mean(y * y, axis=-1, keepdims=True)
    y_ref[0, :, 0:SSD_INNER] = y * lax.rsqrt(ms + EPS) * nw_ref[...]

    sc = sc_ref[0]
    u = sc[:, SC_WIDTH:2 * SC_WIDTH] * sc[:, 2 * SC_WIDTH:3 * SC_WIDTH]
    stail = stail_ref[...]
    cv = scw_ref[SC_CONV - 1:SC_CONV, :] * u
    for j in range(SC_CONV - 1):
        cv = cv + scw_ref[j:j + 1, :] * _shift_rows(u, stail, SC_CONV - 1 - j)
    stail_ref[...] = u[L - SUBLANES:L]
    y_ref[0, :, SSD_INNER:SSD_INNER + SC_WIDTH] = sc[:, 0:SC_WIDTH] * cv


def _ssd(z, xbc, sc, dtg, cw, cb, dtb, aneg, dsk, nw, scw):
    b, s, _ = z.shape
    L = SSD_CHUNK
    blk = lambda w: pl.BlockSpec((1, L, w), lambda i, c: (i, c, 0))
    par = lambda a: pl.BlockSpec(a.shape, lambda i, c: (0, 0))
    return pl.pallas_call(
        _ssd_kernel,
        grid=(b, s // L),
        in_specs=[blk(SSD_INNER), blk(SSD_XBC), blk(3 * SC_WIDTH), blk(LANES),
                  par(cw), par(cb), par(dtb), par(aneg), par(dsk), par(nw), par(scw)],
        out_specs=blk(SSD_INNER + SC_WIDTH),
        out_shape=jax.ShapeDtypeStruct((b, s, SSD_INNER + SC_WIDTH), F32),
        scratch_shapes=[
            pltpu.VMEM((SSD_HEADS // 2, SSD_STATE, LANES), F32),
            pltpu.VMEM((SUBLANES, SSD_XBC), F32),
            pltpu.VMEM((SUBLANES, SC_WIDTH), F32),
        ],
        compiler_params=pltpu.CompilerParams(
            dimension_semantics=("arbitrary", "arbitrary"), vmem_limit_bytes=VMEM_LIMIT_BYTES),
        name="ssd_sc",
    )(z, xbc, sc, dtg, cw, cb, dtb, aneg, dsk, nw, scw)


def _nsa_kernel(q_ref, dtg_ref, kc_ref, vct_ref, ks_ref, vst_ref, kw_ref, vwt_ref, ovt_ref,
                o_ref, sel_ref, qa_ref, acc_ref, m_ref, accw_ref, mw_ref, s_ref, *, seq):
    tq, tk = NSA_TQ, NSA_TK
    hd = NSA_HEAD_DIM
    nh = NSA_HEADS
    nc = seq // CMP_STRIDE
    nb = seq // SEL_BLOCK
    q0 = pl.program_id(1) * tq
    tpos = q0 + lax.broadcasted_iota(jnp.int32, (1, tq), 1)

    def heads_on_lanes(a):
        return jnp.concatenate([a] * nh, axis=1)

    qt = q_ref[0].T
    qa_ref[0:hd, :] = jnp.concatenate([qt[h * hd:(h + 1) * hd, :] for h in range(nh)], axis=1).astype(BF16)
    qa_ref[hd:LANES, :] = jnp.zeros((LANES - hd, nh * tq), BF16)
    gates = _sigmoid(dtg_ref[0]).T

    n_io = lax.broadcasted_iota(jnp.int32, (nc, tq), 0)
    cbias = jnp.where((n_io * CMP_STRIDE + (CMP_BLOCK - 1)) <= tpos, 0.0, NEG)
    s = _dot(kc_ref[0], qa_ref[...]) + heads_on_lanes(cbias)
    p = jnp.exp2(s - jnp.max(s, axis=0, keepdims=True))
    has_key = heads_on_lanes(jnp.where(tpos >= CMP_BLOCK - 1, 1.0, 0.0))
    p = p * (has_key / jnp.sum(p, axis=0, keepdims=True))
    o_cmp = _dot(vct_ref[0], p.astype(BF16))
    psum = p[:, 0:tq]
    for h in range(1, nh):
        psum = psum + p[:, h * tq:(h + 1) * tq]

    def load_vt(ref, key0, width):
        first = key0 // LANES
        return jnp.concatenate([ref[0, first + i] for i in range(width // LANES)], axis=1)

    ovt = ovt_ref[...]
    p1, p2, p3 = _split3(psum)
    imp = _dot(ovt, p1) + _dot(ovt, p2) + _dot(ovt, p3)
    j_io = lax.broadcasted_iota(jnp.int32, (nb, tq), 0)
    cur = jnp.right_shift(tpos, int(math.log2(SEL_BLOCK)))
    valid = j_io <= cur
    forced = (j_io == 0) | (valid & (j_io > cur - SEL_LOCAL))
    v = jnp.where(forced, REMOVED, jnp.where(valid, imp, NEG))
    j_f = j_io.astype(F32)

    def extract(_, carry):
        v, sel = carry
        m = jnp.max(v, axis=0, keepdims=True)
        first = jnp.min(jnp.where(v == m, j_f, float(nb)), axis=0, keepdims=True)
        hit = j_f == first
        return jnp.where(hit, REMOVED, v), jnp.where(hit, 1.0, sel)

    _, sel = lax.fori_loop(0, SEL_TOPK - 1 - SEL_LOCAL, extract, (v, jnp.where(forced, 1.0, 0.0)), unroll=True)
    sel_ref[...] = sel

    wt = NSA_WKEYS // 2
    w0 = pl.multiple_of(jnp.maximum(q0 - WINDOW, 0), tq)
    blocks_per_tile = tk // SEL_BLOCK
    bias_rows = 2 * SUBLANES
    for ref in (acc_ref, accw_ref):
        ref[...] = jnp.zeros_like(ref)
    for ref in (m_ref, mw_ref):
        ref[...] = jnp.full(ref.shape, NEG, F32)

    def update(buf, rows, vt, bias, m_r, acc_r):
        s = s_ref[buf, 0:rows, :]
        if bias is not None:
            s = s + heads_on_lanes(bias)
        m_old = m_r[...]
        m_new = jnp.maximum(m_old, jnp.max(s, axis=0, keepdims=True))
        p = jnp.exp2(s - m_new).astype(BF16)
        ones = jnp.where(lax.broadcasted_iota(jnp.int32, (bias_rows, rows), 0) == 0, 1.0, 0.0).astype(BF16)
        acc_r[...] = jnp.exp2(m_old - m_new) * acc_r[...] + _dot(jnp.concatenate([vt, ones], axis=0), p)
        m_r[...] = m_new

    def win_scores(buf, i):
        k0 = pl.multiple_of(w0 + i * wt, LANES)
        s_ref[buf, 0:wt, :] = _dot(kw_ref[0, pl.ds(k0, wt), :], qa_ref[...])

    def win_update(buf, i):
        k0 = pl.multiple_of(w0 + i * wt, LANES)
        d = (k0 - q0) + (lax.broadcasted_iota(jnp.int32, (wt, tq), 0)
                         - lax.broadcasted_iota(jnp.int32, (wt, tq), 1))
        bias = jnp.where(d <= 0, jnp.where(d > -WINDOW, 0.0, NEG), NEG)
        update(buf, wt, load_vt(vwt_ref, k0, wt), bias, mw_ref, accw_ref)

    def sel_scores(buf, kt):
        k0 = pl.multiple_of(kt * tk, tk)
        chunk = sel_ref[pl.ds(pl.multiple_of(kt * blocks_per_tile, blocks_per_tile), blocks_per_tile), :]
        bias = jnp.concatenate([(1.0 - chunk) * NEG, jnp.zeros((bias_rows - blocks_per_tile, tq), F32)], axis=0)
        qa_ref[hd:hd + bias_rows, :] = heads_on_lanes(bias).astype(BF16)
        s_ref[buf] = _dot(ks_ref[0, pl.ds(k0, tk), :], qa_ref[...])

    def sel_update(buf, kt, causal):
        k0 = pl.multiple_of(kt * tk, tk)
        bias = None
        if causal:
            bias = jnp.where(k0 + lax.broadcasted_iota(jnp.int32, (tk, tq), 0) <= tpos, 0.0, NEG)
        update(buf, tk, load_vt(vst_ref, k0, tk), bias, m_ref, acc_ref)

    win_scores(0, 0)
    win_scores(1, 1)
    win_update(0, 0)
    sel_scores(0, 0)
    win_update(1, 1)

    n_last = (q0 + tq + tk - 1) // tk - 1

    def pair(j, _):
        sel_scores(1, 2 * j + 1)
        sel_update(0, 2 * j, False)
        sel_scores(0, 2 * j + 2)
        sel_update(1, 2 * j + 1, False)
        return 0

    lax.fori_loop(0, n_last // 2, pair, 0)

    @pl.when(n_last % 2 == 0)
    def _():
        sel_update(0, n_last, True)

    @pl.when(n_last % 2 == 1)
    def _():
        sel_scores(1, n_last)
        sel_update(0, n_last - 1, False)
        sel_update(1, n_last, True)

    o_sel = acc_ref[0:hd, :] / acc_ref[hd:hd + 1, :]
    o_win = accw_ref[0:hd, :] / accw_ref[hd:hd + 1, :]

    def gate_row(branch):
        return jnp.concatenate([gates[GATE_COL + 3 * h + branch:GATE_COL + 3 * h + branch + 1, :]
                                for h in range(nh)], axis=1)

    out = o_cmp * gate_row(0) + o_sel * gate_row(1) + o_win * gate_row(2)
    o_ref[0] = jnp.concatenate([out[:, h * tq:(h + 1) * tq] for h in range(nh)], axis=0).T


def _nsa(q, dtg, kc, vct, ks, vst, kw, vwt, ovt):
    b, s, _ = q.shape
    tq = NSA_TQ
    nc = s // CMP_STRIDE
    per_b3 = lambda shape: pl.BlockSpec((1,) + shape, lambda i, j: (i, 0, 0))
    per_b4 = lambda shape: pl.BlockSpec((1,) + shape, lambda i, j: (i, 0, 0, 0))
    qblk = lambda w: pl.BlockSpec((1, tq, w), lambda i, j: (i, j, 0))
    return pl.pallas_call(
        functools.partial(_nsa_kernel, seq=s),
        grid=(b, s // tq),
        in_specs=[
            qblk(NSA_WIDTH), qblk(LANES),
            per_b3((nc, LANES)), per_b3((NSA_HEAD_DIM, nc)),
            per_b3((s, LANES)), per_b4((s // LANES, NSA_HEAD_DIM, LANES)),
            per_b3((s, LANES)), per_b4((s // LANES, NSA_HEAD_DIM, LANES)),
            pl.BlockSpec(ovt.shape, lambda i, j: (0, 0)),
        ],
        out_specs=qblk(NSA_WIDTH),
        out_shape=jax.ShapeDtypeStruct((b, s, NSA_WIDTH), F32),
        scratch_shapes=[
            pltpu.VMEM((s // SEL_BLOCK, tq), F32),
            pltpu.VMEM((LANES, NSA_HEADS * tq), BF16),
            pltpu.VMEM((NSA_HEAD_DIM + 2 * SUBLANES, NSA_HEADS * tq), F32),
            pltpu.VMEM((1, NSA_HEADS * tq), F32),
            pltpu.VMEM((NSA_HEAD_DIM + 2 * SUBLANES, NSA_HEADS * tq), F32),
            pltpu.VMEM((1, NSA_HEADS * tq), F32),
            pltpu.VMEM((2, NSA_TK, NSA_HEADS * tq), F32),
        ],
        compiler_params=pltpu.CompilerParams(
            dimension_semantics=("arbitrary", "arbitrary"), vmem_limit_bytes=VMEM_LIMIT_BYTES),
        name="nsa",
    )(q, dtg, kc, vct, ks, vst, kw, vwt, ovt)


def _ffn_kernel(x_ref, ya_ref, yn_ref, wo_ref, nw_ref, wu_ref, cw_ref, cb_ref, wd_ref, fw_ref,
                o_ref, tail_ref, act_ref, *, tiles_per_seq, final_norm):
    tm = x_ref.shape[0]
    na = ya_ref.shape[1]

    @pl.when(pl.program_id(0) % tiles_per_seq == 0)
    def _():
        tail_ref[...] = jnp.zeros_like(tail_ref)

    x1 = (x_ref[...] + _dot(ya_ref[...].astype(BF16), wo_ref[0:na, :])
          + _dot(yn_ref[...].astype(BF16), wo_ref[na:, :]))
    ms = jnp.mean(x1 * x1, axis=-1, keepdims=True)
    h = (x1 * lax.rsqrt(ms + EPS) * nw_ref[...]).astype(BF16)

    def conv_cols(c0):
        cols = slice(c0, c0 + FF_TILE)
        u = _dot(h, wu_ref[:, cols])
        tail = tail_ref[:, cols]
        cv = cb_ref[:, cols] + cw_ref[FFN_CONV - 1:FFN_CONV, cols] * u
        for t in range(FFN_CONV - 1):
            cv = cv + cw_ref[t:t + 1, cols] * _shift_rows(u, tail, FFN_CONV - 1 - t)
        tail_ref[:, cols] = u[tm - SUBLANES:tm]
        return cv

    for j in range(N_FF_TILES):
        gate = conv_cols(j * FF_TILE)
        val = conv_cols(D_FF_PAD + j * FF_TILE)
        act_ref[:, j * FF_TILE:(j + 1) * FF_TILE] = (_silu(gate) * val).astype(BF16)

    x2 = x1 + _dot(act_ref[...], wd_ref[...])
    if final_norm:
        ms2 = jnp.mean(x2 * x2, axis=-1, keepdims=True)
        x2 = x2 * lax.rsqrt(ms2 + EPS) * fw_ref[...]
    o_ref[...] = x2


def _ffn(x2d, ya, yn, wo, nw, wu_t, cw_t, cb_t, wd, fw, seq, final_norm):
    t = x2d.shape[0]
    tm = FFN_TM
    row = lambda i: (i, 0)
    c2 = lambda a: pl.BlockSpec(a.shape, lambda i: (0, 0))
    return pl.pallas_call(
        functools.partial(_ffn_kernel, tiles_per_seq=seq // tm, final_norm=final_norm),
        grid=(t // tm,),
        in_specs=[
            pl.BlockSpec((tm, D_MODEL), row),
            pl.BlockSpec((tm, ya.shape[1]), row),
            pl.BlockSpec((tm, yn.shape[1]), row),
            c2(wo), c2(nw), c2(wu_t), c2(cw_t), c2(cb_t), c2(wd), c2(fw),
        ],
        out_specs=pl.BlockSpec((tm, D_MODEL), row),
        out_shape=jax.ShapeDtypeStruct((t, D_MODEL), F32),
        scratch_shapes=[
            pltpu.VMEM((SUBLANES, 2 * D_FF_PAD), F32),
            pltpu.VMEM((tm, D_FF_PAD), BF16),
        ],
        compiler_params=pltpu.CompilerParams(
            dimension_semantics=("arbitrary",), vmem_limit_bytes=VMEM_LIMIT_BYTES),
        name="outproj_ffn",
    )(x2d, ya, yn, wo, nw, wu_t, cw_t, cb_t, wd, fw)


def _pack_w_in(w):
    wt = jnp.transpose(w, (0, 2, 1))
    dt0 = SSD_INNER + SSD_XBC
    dt1 = dt0 + SSD_HEADS
    g0 = dt1 + 3 * SC_WIDTH + NSA_WIDTH + 6 * NSA_HEAD_DIM
    g1 = g0 + 3 * NSA_HEADS
    pad = jnp.zeros((w.shape[0], LANES - SSD_HEADS - 3 * NSA_HEADS, w.shape[1]), w.dtype)
    return jnp.concatenate([wt[:, :dt0], wt[:, dt1:g0], wt[:, dt0:dt1], wt[:, g0:g1], pad], axis=1)


def _rope_tables(seq):
    half = NSA_HEAD_DIM // 2
    inv = 1.0 / (ROPE_THETA ** (jnp.arange(half, dtype=F32) / half))
    ang = jnp.arange(seq, dtype=F32)[:, None] * inv[None, :]
    cos, sin = jnp.cos(ang), jnp.sin(ang)
    scale = NSA_HEAD_DIM ** -0.5 * LOG2E
    one = jnp.ones_like(cos)
    zero = jnp.zeros_like(cos)
    rqc = jnp.concatenate([cos, cos, cos, cos], axis=1) * scale
    rqs = jnp.concatenate([-sin, sin, -sin, sin], axis=1) * scale
    rkc = jnp.concatenate([cos, cos, one, one], axis=1)
    rks = jnp.concatenate([-sin, sin, zero, zero], axis=1)
    return rqc, rqs, rkc, rks


def _compress_weights(kw1, kw2, vw1, vw2):
    half_tokens = CMP_BLOCK // 2
    hd, hid = NSA_HEAD_DIM, CMP_HIDDEN
    kw1r = kw1.reshape(2, half_tokens, hd, hid)
    vw1r = vw1.reshape(2, half_tokens, hd, hid)
    zeros = jnp.zeros((half_tokens, hd, hid), kw1.dtype)
    cols = []
    for w1r, is_k in ((kw1r, True), (vw1r, False)):
        for part in range(2):
            blk = w1r[part]
            rows = jnp.concatenate([blk, zeros] if is_k else [zeros, blk], axis=1)
            cols.append(rows.reshape(half_tokens * LANES, hid))
    wc = jnp.concatenate(cols, axis=1).astype(BF16)
    zk = jnp.zeros((hid, hd), kw2.dtype)
    w2bd = jnp.concatenate([jnp.concatenate([kw2, zk], axis=1),
                            jnp.concatenate([zk, vw2], axis=1)], axis=0).astype(BF16)
    return wc, w2bd


def _overlap_t(seq):
    nc = seq // CMP_STRIDE
    nb = seq // SEL_BLOCK
    cmp_start = np.arange(nc) * CMP_STRIDE
    slc_start = np.arange(nb) * SEL_BLOCK
    ov = ((cmp_start[None, :] < slc_start[:, None] + SEL_BLOCK)
          & (cmp_start[None, :] + CMP_BLOCK > slc_start[:, None])
          & (np.arange(nc)[None, :] < nc - 1))
    return jnp.asarray(ov.astype(np.float32), dtype=BF16)


def _pad_rows(a, rows):
    return jnp.concatenate([a, jnp.zeros((rows - a.shape[0],) + a.shape[1:], a.dtype)], axis=0)


def _pad_cols(a, cols):
    return jnp.concatenate([a, jnp.zeros(a.shape[:-1] + (cols - a.shape[-1],), a.dtype)], axis=-1)


def _ff_tiles(a):
    return jnp.concatenate([_pad_cols(a[:, :D_FF], D_FF_PAD), _pad_cols(a[:, D_FF:], D_FF_PAD)], axis=1)


def kernel(x, attn_norm_w, w_in, ssd_conv_w, ssd_conv_b, ssd_dt_bias, ssd_a_log, ssd_d, ssd_norm_w, sc_conv_w,
           cmp_k_pos, cmp_k_w1, cmp_k_w2, cmp_v_pos, cmp_v_w1, cmp_v_w2, w_out, ffn_norm_w, ffn_w_up,
           ffn_conv_w, ffn_conv_b, ffn_w_down, final_norm_w):
    b, s, d = x.shape
    depth = w_in.shape[0]
    assert d == D_MODEL and s % NSA_TK == 0 and s // SEL_BLOCK >= SEL_TOPK and s % INPROJ_TM == 0
    assert INPROJ_TM % NSA_TK == 0 and s >= NSA_WKEYS and NSA_TK // SEL_BLOCK <= SUBLANES
    t = b * s
    ropes = _rope_tables(s)
    ovt = _overlap_t(s)
    nc = s // CMP_STRIDE
    half_tokens = CMP_BLOCK // 2
    x2d = x.reshape(t, d)
    w_pack_all = _pack_w_in(w_in)
    for l in range(depth):
        w_pack = w_pack_all[l]
        cw = _pad_rows(ssd_conv_w[l], SUBLANES)
        cb = ssd_conv_b[l][None, :]
        dtb = _pad_cols(ssd_dt_bias[l][None, :], LANES)
        aneg = _pad_cols(-jnp.exp(ssd_a_log[l].astype(F32))[None, :] * LOG2E, LANES)
        dsk = jnp.repeat(ssd_d[l].astype(F32), SSD_HEAD_DIM)[None, :]
        scw = _pad_rows(sc_conv_w[l], SUBLANES)
        wc, w2bd = _compress_weights(cmp_k_w1[l], cmp_k_w2[l], cmp_v_w1[l], cmp_v_w2[l])
        pos2 = _pad_rows(jnp.concatenate([cmp_k_pos[l], cmp_v_pos[l]], axis=1).reshape(2, half_tokens * LANES),
                         SUBLANES)
        wo = w_out[l].astype(BF16)
        wu_t = _ff_tiles(ffn_w_up[l]).astype(BF16)
        cw_t = _ff_tiles(_pad_rows(ffn_conv_w[l], SUBLANES))
        cb_t = _ff_tiles(ffn_conv_b[l][None, :])
        wd = _pad_rows(ffn_w_down[l], D_FF_PAD).astype(BF16)

        z, xbc, sc, q, cpair, ks, kw, vst, vwt, dtg = _inproj(x2d, attn_norm_w[l][None, :], w_pack, ropes, s)
        kc, vct = _compress(cpair.reshape(b, nc, half_tokens * LANES), pos2, wc, w2bd)
        r3 = lambda a: a.reshape(b, s, a.shape[-1])
        ya = _ssd(r3(z), r3(xbc), r3(sc), r3(dtg), cw, cb, dtb, aneg, dsk, ssd_norm_w[l][None, :], scw)
        vt4 = lambda a: a.reshape(b, s // LANES, NSA_HEAD_DIM, LANES)
        yn = _nsa(r3(q), r3(dtg), kc, vct, r3(ks), vt4(vst), r3(kw), vt4(vwt), ovt)
        x2d = _ffn(x2d, ya.reshape(t, -1), yn.reshape(t, -1), wo, ffn_norm_w[l][None, :], wu_t, cw_t, cb_t, wd,
                   final_norm_w[None, :], s, l == depth - 1)
    return x2d.reshape(b, s, d)
```

```python
import functools
import math

import numpy as np
import jax
import jax.numpy as jnp
from jax import lax
from jax.experimental import pallas as pl
from jax.experimental.pallas import tpu as pltpu

F32 = jnp.float32
BF16 = jnp.bfloat16

D_MODEL = 1024
SSD_HEADS = 8
SSD_HEAD_DIM = 64
SSD_INNER = SSD_HEADS * SSD_HEAD_DIM
SSD_GROUPS = 2
SSD_STATE = 128
SSD_CONV = 4
SSD_CHUNK = 256
SSD_XBC = SSD_INNER + 2 * SSD_GROUPS * SSD_STATE
SC_WIDTH = 256
SC_CONV = 3
NSA_HEADS = 4
NSA_HEAD_DIM = 64
NSA_WIDTH = NSA_HEADS * NSA_HEAD_DIM
CMP_BLOCK = 32
CMP_STRIDE = 16
CMP_HIDDEN = 128
SEL_BLOCK = 64
SEL_TOPK = 16
SEL_LOCAL = 2
WINDOW = 512
ROPE_THETA = 10000.0
D_FF = 2752
FFN_CONV = 3
EPS = 1e-6
NEG = -1e30
FORCE = 1e9
REMOVED = -3.0e38
LOG2E = 1.4426950408889634

LANES = 128
SUBLANES = 8
VMEM_LIMIT_BYTES = 56 * 1024 * 1024

D_FF_PAD = 2816
FF_TILE = 256
N_FF_TILES = D_FF_PAD // FF_TILE

COL_Z = 0
COL_XBC = COL_Z + SSD_INNER
COL_SC = COL_XBC + SSD_XBC
COL_Q = COL_SC + 3 * SC_WIDTH
COL_KV = COL_Q + NSA_WIDTH
COL_DTG = COL_KV + 6 * NSA_HEAD_DIM
N_PACK = COL_DTG + LANES
GATE_COL = SSD_HEADS
IN_DT0 = SSD_INNER + SSD_XBC
IN_DT1 = IN_DT0 + SSD_HEADS
IN_G0 = IN_DT1 + 3 * SC_WIDTH + NSA_WIDTH + 6 * NSA_HEAD_DIM
IN_G1 = IN_G0 + 3 * NSA_HEADS

INPROJ_TM = 512
NSA_TQ = 256
NSA_TK = 512
NSA_WKEYS = WINDOW + NSA_TQ
FFN_TM = 256


def _dot(a, b):
    return jnp.dot(a, b, preferred_element_type=F32)


def _dot_nt(a, b):
    return lax.dot_general(a, b, (((1,), (1,)), ((), ())), preferred_element_type=F32)


def _sigmoid(x):
    return 1.0 / (1.0 + jnp.exp(-x))


def _silu(x):
    h = 0.5 * x
    return h + h * jnp.tanh(h)


def _softplus(x):
    return jnp.maximum(x, 0.0) + jnp.log1p(jnp.exp(-jnp.abs(x)))


def _split3(a):
    a1 = a.astype(BF16)
    r1 = a - a1.astype(F32)
    a2 = r1.astype(BF16)
    r2 = r1 - a2.astype(F32)
    return a1, a2, r2.astype(BF16)


def _shift_rows(cur, tail, k):
    if k == 0:
        return cur
    rc = pltpu.roll(cur, k, 0)
    rt = pltpu.roll(tail, k, 0)
    row = lax.broadcasted_iota(jnp.int32, tail.shape, 0)
    first = jnp.where(row < k, rt, rc[0:SUBLANES])
    return jnp.concatenate([first, rc[SUBLANES:]], axis=0)


def _inproj_kernel(x_ref, nw_ref, w_ref, rqc_ref, rqs_ref, rkc_ref, rks_ref,
                   z_ref, xbc_ref, sc_ref, q_ref, cpair_ref, ks_ref, kw_ref, vst_ref, vwt_ref, dtg_ref, wb_ref):
    tm = x_ref.shape[0]

    @pl.when(pl.program_id(0) == 0)
    def _():
        def cast_rows(dst, src, n):
            for r in range(0, n, LANES):
                m = min(LANES, n - r)
                wb_ref[dst + r:dst + r + m, :] = w_ref[0, src + r:src + r + m, :].astype(BF16)

        cast_rows(0, 0, IN_DT0)
        cast_rows(IN_DT0, IN_DT1, IN_G0 - IN_DT1)
        tail = jnp.concatenate([w_ref[0, IN_DT0:IN_DT1, :], w_ref[0, IN_G0:IN_G1, :],
                                jnp.zeros((LANES - (IN_DT1 - IN_DT0) - (IN_G1 - IN_G0), D_MODEL), F32)], axis=0)
        wb_ref[COL_DTG:N_PACK, :] = tail.astype(BF16)

    x = x_ref[...]
    ms = jnp.mean(x * x, axis=-1, keepdims=True)
    h = (x * lax.rsqrt(ms + EPS) * nw_ref[...]).astype(BF16)

    def proj(a, b):
        return _dot_nt(h, wb_ref[a:b, :])

    z_ref[...] = proj(COL_Z, COL_XBC)
    xbc_ref[...] = proj(COL_XBC, COL_SC)
    sc_ref[...] = proj(COL_SC, COL_Q)
    dtg_ref[...] = proj(COL_DTG, N_PACK)

    lane = lax.broadcasted_iota(jnp.int32, (tm, LANES), 1)
    first_half = (lane % NSA_HEAD_DIM) < (NSA_HEAD_DIM // 2)

    def rope(v, c, s):
        partner = jnp.where(first_half, pltpu.roll(v, LANES - 32, 1), pltpu.roll(v, 32, 1))
        return v * c + partner * s

    rqc, rqs, rkc, rks = rqc_ref[...], rqs_ref[...], rkc_ref[...], rks_ref[...]
    q = proj(COL_Q, COL_KV)
    q_ref[:, 0:LANES] = rope(q[:, 0:LANES], rqc, rqs)
    q_ref[:, LANES:2 * LANES] = rope(q[:, LANES:2 * LANES], rqc, rqs)

    kv = proj(COL_KV, COL_DTG)
    cpair_ref[...] = rope(kv[:, 0:LANES], rkc, rks)
    row = lax.broadcasted_iota(jnp.int32, (tm, LANES), 0)
    block_in_tile = (row % NSA_TK) // SEL_BLOCK
    onehot = jnp.where(lane - NSA_HEAD_DIM == block_in_tile, 1.0, 0.0)
    k_lanes = lane < NSA_HEAD_DIM
    for pair, k_ref, vt_ref, fill in ((1, ks_ref, vst_ref, onehot), (2, kw_ref, vwt_ref, 0.0)):
        p = rope(kv[:, pair * LANES:(pair + 1) * LANES], rkc, rks)
        k_ref[...] = jnp.where(k_lanes, p, fill).astype(BF16)
        pt = p.T
        for j in range(tm // LANES):
            vt_ref[j] = pt[NSA_HEAD_DIM:LANES, j * LANES:(j + 1) * LANES].astype(BF16)


def _inproj(x2d, nw, w_t, layer, ropes, seq):
    t = x2d.shape[0]
    tm = INPROJ_TM
    nt = t // tm
    pos_blocks = seq // tm
    row = lambda i: (i, 0)
    pos = lambda i: (i % pos_blocks, 0)
    const = lambda i: (0, 0)
    out_shapes = (
        jax.ShapeDtypeStruct((t, SSD_INNER), F32),
        jax.ShapeDtypeStruct((t, SSD_XBC), F32),
        jax.ShapeDtypeStruct((t, 3 * SC_WIDTH), F32),
        jax.ShapeDtypeStruct((t, NSA_WIDTH), F32),
        jax.ShapeDtypeStruct((t, LANES), F32),
        jax.ShapeDtypeStruct((t, LANES), BF16),
        jax.ShapeDtypeStruct((t, LANES), BF16),
        jax.ShapeDtypeStruct((t // LANES, NSA_HEAD_DIM, LANES), BF16),
        jax.ShapeDtypeStruct((t // LANES, NSA_HEAD_DIM, LANES), BF16),
        jax.ShapeDtypeStruct((t, LANES), F32),
    )
    vt_spec = pl.BlockSpec((tm // LANES, NSA_HEAD_DIM, LANES), lambda i: (i, 0, 0))
    out_specs = (
        pl.BlockSpec((tm, SSD_INNER), row),
        pl.BlockSpec((tm, SSD_XBC), row),
        pl.BlockSpec((tm, 3 * SC_WIDTH), row),
        pl.BlockSpec((tm, NSA_WIDTH), row),
        pl.BlockSpec((tm, LANES), row),
        pl.BlockSpec((tm, LANES), row),
        pl.BlockSpec((tm, LANES), row),
        vt_spec,
        vt_spec,
        pl.BlockSpec((tm, LANES), row),
    )
    in_specs = [
        pl.BlockSpec((tm, D_MODEL), row),
        pl.BlockSpec((1, D_MODEL), const),
        pl.BlockSpec((1, IN_G1, D_MODEL), lambda i: (layer, 0, 0), pipeline_mode=pl.Buffered(1)),
        pl.BlockSpec((tm, LANES), pos),
        pl.BlockSpec((tm, LANES), pos),
        pl.BlockSpec((tm, LANES), pos),
        pl.BlockSpec((tm, LANES), pos),
    ]
    return pl.pallas_call(
        _inproj_kernel,
        grid=(nt,),
        in_specs=in_specs,
        out_specs=out_specs,
        out_shape=out_shapes,
        scratch_shapes=[pltpu.VMEM((N_PACK, D_MODEL), BF16)],
        compiler_params=pltpu.CompilerParams(
            dimension_semantics=("arbitrary",), vmem_limit_bytes=VMEM_LIMIT_BYTES),
        name="inproj",
    )(x2d, nw, w_t, *ropes)


def _compress_kernel(x2_ref, pos_ref, wc_ref, w2_ref, kc_ref, vct_ref):
    nc = x2_ref.shape[1]
    x2 = x2_ref[0].astype(BF16)
    wc = wc_ref[...]
    y = _dot(x2, wc)
    r = _dot(pos_ref[...].astype(BF16), wc)
    h = CMP_HIDDEN

    def pre(base):
        bias = r[0:1, base:base + h] + r[1:2, base + h:base + 2 * h]
        return y[:, base:base + h] + pltpu.roll(y[:, base + h:base + 2 * h], nc - 1, 0) + bias

    hid = jnp.concatenate([jax.nn.gelu(pre(0)), jax.nn.gelu(pre(2 * h))], axis=1).astype(BF16)
    o = _dot(hid, w2_ref[...])
    kc_ref[0] = o.astype(BF16)
    vct_ref[0] = o.T[NSA_HEAD_DIM:LANES, :].astype(BF16)


def _compress(x2p, pos2, wc, w2bd):
    b, nc, width = x2p.shape
    return pl.pallas_call(
        _compress_kernel,
        grid=(b,),
        in_specs=[
            pl.BlockSpec((1, nc, width), lambda i: (i, 0, 0)),
            pl.BlockSpec(pos2.shape, lambda i: (0, 0)),
            pl.BlockSpec(wc.shape, lambda i: (0, 0)),
            pl.BlockSpec(w2bd.shape, lambda i: (0, 0)),
        ],
        out_specs=(
            pl.BlockSpec((1, nc, LANES), lambda i: (i, 0, 0)),
            pl.BlockSpec((1, NSA_HEAD_DIM, nc), lambda i: (i, 0, 0)),
        ),
        out_shape=(
            jax.ShapeDtypeStruct((b, nc, LANES), BF16),
            jax.ShapeDtypeStruct((b, NSA_HEAD_DIM, nc), BF16),
        ),
        compiler_params=pltpu.CompilerParams(
            dimension_semantics=("arbitrary",), vmem_limit_bytes=VMEM_LIMIT_BYTES),
        name="compress",
    )(x2p, pos2, wc, w2bd)


def _ssd_kernel(z_ref, xbc_ref, sc_ref, dtg_ref, cw_ref, cb_ref, dtb_ref, aneg_ref, dsk_ref, nw_ref, scw_ref,
                y_ref, h_ref, xtail_ref, stail_ref):
    L = SSD_CHUNK
    half = SSD_HEAD_DIM

    @pl.when(pl.program_id(1) == 0)
    def _():
        h_ref[...] = jnp.zeros_like(h_ref)
        xtail_ref[...] = jnp.zeros_like(xtail_ref)
        stail_ref[...] = jnp.zeros_like(stail_ref)

    xraw = xbc_ref[0]
    xtail = xtail_ref[...]
    conv = cb_ref[...] + cw_ref[SSD_CONV - 1:SSD_CONV, :] * xraw
    for j in range(SSD_CONV - 1):
        conv = conv + cw_ref[j:j + 1, :] * _shift_rows(xraw, xtail, SSD_CONV - 1 - j)
    xtail_ref[...] = xraw[L - SUBLANES:L]
    xc = _silu(conv)

    dt = _softplus(dtg_ref[0] + dtb_ref[...])
    a = dt * aneg_ref[...]
    row = lax.broadcasted_iota(jnp.int32, (L, L), 0)
    col = lax.broadcasted_iota(jnp.int32, (L, L), 1)
    causal = row >= col
    tri = jnp.where(causal, 1.0, 0.0).astype(BF16)
    a1, a2, a3 = _split3(a)
    acs = _dot(tri, a1) + _dot(tri, a2) + _dot(tri, a3)
    acs_t = acs.T
    exp_acs = jnp.exp2(acs)
    last = acs[L - 1:L, :]
    dte = jnp.exp2(last - acs)
    chunk_decay = jnp.exp2(last)

    lo = lax.broadcasted_iota(jnp.int32, (L, LANES), 1) < half
    lo_row = lax.broadcasted_iota(jnp.int32, (1, LANES), 1) < half

    def per_lane(m, ha, hb, mask):
        return jnp.where(mask, m[:, ha:ha + 1], m[:, hb:hb + 1])

    ys = []
    for g in range(SSD_GROUPS):
        b_g = xc[:, SSD_INNER + g * SSD_STATE:SSD_INNER + (g + 1) * SSD_STATE]
        c_g = xc[:, SSD_INNER + (SSD_GROUPS + g) * SSD_STATE:SSD_INNER + (SSD_GROUPS + g + 1) * SSD_STATE]
        b_bf = b_g.astype(BF16)
        c_bf = c_g.astype(BF16)
        cb = _dot_nt(c_bf, b_bf)
        bt_bf = b_g.T.astype(BF16)
        for pp in range(SSD_HEADS // SSD_GROUPS // 2):
            p = g * (SSD_HEADS // SSD_GROUPS // 2) + pp
            ha, hb = 2 * p, 2 * p + 1
            x_pair = xc[:, p * LANES:(p + 1) * LANES]
            xdt = x_pair * per_lane(dt, ha, hb, lo)
            xdt_bf = xdt.astype(BF16)
            yd = []
            for hd in (ha, hb):
                seg = acs[:, hd:hd + 1] - acs_t[hd:hd + 1, :]
                decay = jnp.exp2(jnp.where(causal, seg, NEG))
                yd.append(_dot((decay * cb).astype(BF16), xdt_bf))
            y_diag = jnp.where(lo, yd[0], yd[1])
            st = _dot(bt_bf, (xdt * per_lane(dte, ha, hb, lo)).astype(BF16))
            h_prev = h_ref[p]
            y_off = _dot(c_bf, h_prev.astype(BF16)) * per_lane(exp_acs, ha, hb, lo)
            h_ref[p] = h_prev * per_lane(chunk_decay, ha, hb, lo_row) + st
            ys.append(y_diag + y_off + x_pair * dsk_ref[:, p * LANES:(p + 1) * LANES])
    y = jnp.concatenate(ys, axis=1)
    z = z_ref[0]
    y = y * _silu(z)
    ms = jnp.mean(y * y, axis=-1, keepdims=True)
    y_ref[0, :, 0:SSD_INNER] = y * lax.rsqrt(ms + EPS) * nw_ref[...]

    sc = sc_ref[0]
    u = sc[:, SC_WIDTH:2 * SC_WIDTH] * sc[:, 2 * SC_WIDTH:3 * SC_WIDTH]
    stail = stail_ref[...]
    cv = scw_ref[SC_CONV - 1:SC_CONV, :] * u
    for j in range(SC_CONV - 1):
        cv = cv + scw_ref[j:j + 1, :] * _shift_rows(u, stail, SC_CONV - 1 - j)
    stail_ref[...] = u[L - SUBLANES:L]
    y_ref[0, :, SSD_INNER:SSD_INNER + SC_WIDTH] = sc[:, 0:SC_WIDTH] * cv


def _ssd(z, xbc, sc, dtg, cw, cb, dtb, aneg, dsk, nw, scw):
    b, s, _ = z.shape
    L = SSD_CHUNK
    blk = lambda w: pl.BlockSpec((1, L, w), lambda i, c: (i, c, 0))
    par = lambda a: pl.BlockSpec(a.shape, lambda i, c: (0, 0))
    return pl.pallas_call(
        _ssd_kernel,
        grid=(b, s // L),
        in_specs=[blk(SSD_INNER), blk(SSD_XBC), blk(3 * SC_WIDTH), blk(LANES),
                  par(cw), par(cb), par(dtb), par(aneg), par(dsk), par(nw), par(scw)],
        out_specs=blk(SSD_INNER + SC_WIDTH),
        out_shape=jax.ShapeDtypeStruct((b, s, SSD_INNER + SC_WIDTH), F32),
        scratch_shapes=[
            pltpu.VMEM((SSD_HEADS // 2, SSD_STATE, LANES), F32),
            pltpu.VMEM((SUBLANES, SSD_XBC), F32),
            pltpu.VMEM((SUBLANES, SC_WIDTH), F32),
        ],
        compiler_params=pltpu.CompilerParams(
            dimension_semantics=("arbitrary", "arbitrary"), vmem_limit_bytes=VMEM_LIMIT_BYTES),
        name="ssd_sc",
    )(z, xbc, sc, dtg, cw, cb, dtb, aneg, dsk, nw, scw)


def _nsa_kernel(q_ref, dtg_ref, kc_ref, vct_ref, ks_ref, vst_ref, kw_ref, vwt_ref, ovt_ref,
                o_ref, sel_ref, qa_ref, acc_ref, m_ref, accw_ref, mw_ref, s_ref, *, seq):
    tq, tk = NSA_TQ, NSA_TK
    hd = NSA_HEAD_DIM
    nh = NSA_HEADS
    nc = seq // CMP_STRIDE
    nb = seq // SEL_BLOCK
    q0 = pl.program_id(1) * tq
    tpos = q0 + lax.broadcasted_iota(jnp.int32, (1, tq), 1)

    def heads_on_lanes(a):
        return jnp.concatenate([a] * nh, axis=1)

    qt = q_ref[0].T
    qa_ref[0:hd, :] = jnp.concatenate([qt[h * hd:(h + 1) * hd, :] for h in range(nh)], axis=1).astype(BF16)
    qa_ref[hd:LANES, :] = jnp.zeros((LANES - hd, nh * tq), BF16)
    gates = _sigmoid(dtg_ref[0]).T

    n_io = lax.broadcasted_iota(jnp.int32, (nc, tq), 0)
    cbias = jnp.where((n_io * CMP_STRIDE + (CMP_BLOCK - 1)) <= tpos, 0.0, NEG)
    s = _dot(kc_ref[0], qa_ref[...]) + heads_on_lanes(cbias)
    p = jnp.exp2(s - jnp.max(s, axis=0, keepdims=True))
    has_key = heads_on_lanes(jnp.where(tpos >= CMP_BLOCK - 1, 1.0, 0.0))
    p = p * (has_key / jnp.sum(p, axis=0, keepdims=True))
    o_cmp = _dot(vct_ref[0], p.astype(BF16))
    psum = p[:, 0:tq]
    for h in range(1, nh):
        psum = psum + p[:, h * tq:(h + 1) * tq]

    def load_vt(ref, key0, width):
        first = key0 // LANES
        return jnp.concatenate([ref[0, first + i] for i in range(width // LANES)], axis=1)

    ovt = ovt_ref[...]
    p1, p2, p3 = _split3(psum)
    imp = _dot(ovt, p1) + _dot(ovt, p2) + _dot(ovt, p3)
    j_io = lax.broadcasted_iota(jnp.int32, (nb, tq), 0)
    cur = jnp.right_shift(tpos, int(math.log2(SEL_BLOCK)))
    valid = j_io <= cur
    forced = (j_io == 0) | (valid & (j_io > cur - SEL_LOCAL))
    v = jnp.where(forced, REMOVED, jnp.where(valid, imp, NEG))
    j_f = j_io.astype(F32)

    def extract(_, carry):
        v, sel = carry
        m = jnp.max(v, axis=0, keepdims=True)
        first = jnp.min(jnp.where(v == m, j_f, float(nb)), axis=0, keepdims=True)
        hit = j_f == first
        return jnp.where(hit, REMOVED, v), jnp.where(hit, 1.0, sel)

    _, sel = lax.fori_loop(0, SEL_TOPK - 1 - SEL_LOCAL, extract, (v, jnp.where(forced, 1.0, 0.0)), unroll=True)
    sel_ref[...] = sel

    wt = NSA_WKEYS // 2
    w0 = pl.multiple_of(jnp.maximum(q0 - WINDOW, 0), tq)
    blocks_per_tile = tk // SEL_BLOCK
    bias_rows = 2 * SUBLANES
    for ref in (acc_ref, accw_ref):
        ref[...] = jnp.zeros_like(ref)
    for ref in (m_ref, mw_ref):
        ref[...] = jnp.full(ref.shape, NEG, F32)

    def scores_of(buf, rows, k_tile):
        def one_head(h):
            lanes = slice(h * tq, (h + 1) * tq)
            s_ref[buf, 0:rows, lanes] = _dot(k_tile, qa_ref[:, lanes])
        return one_head

    def update_of(buf, rows, vt, bias, m_r, acc_r):
        ones = jnp.where(lax.broadcasted_iota(jnp.int32, (bias_rows, rows), 0) == 0, 1.0, 0.0).astype(BF16)
        vt_aug = jnp.concatenate([vt, ones], axis=0)

        def one_head(h):
            lanes = slice(h * tq, (h + 1) * tq)
            s = s_ref[buf, 0:rows, lanes]
            if bias is not None:
                s = s + bias
            m_old = m_r[:, lanes]
            m_new = jnp.maximum(m_old, jnp.max(s, axis=0, keepdims=True))
            p = jnp.exp2(s - m_new).astype(BF16)
            acc_r[:, lanes] = jnp.exp2(m_old - m_new) * acc_r[:, lanes] + _dot(vt_aug, p)
            m_r[:, lanes] = m_new
        return one_head

    def stage(score_fn, update_fn):
        for h in range(nh):
            if score_fn is not None:
                score_fn(h)
            if update_fn is not None:
                update_fn(h)

    def win_scores(buf, i):
        k0 = pl.multiple_of(w0 + i * wt, LANES)
        return scores_of(buf, wt, kw_ref[0, pl.ds(k0, wt), :])

    def win_update(buf, i):
        k0 = pl.multiple_of(w0 + i * wt, LANES)
        d = (k0 - q0) + (lax.broadcasted_iota(jnp.int32, (wt, tq), 0)
                         - lax.broadcasted_iota(jnp.int32, (wt, tq), 1))
        bias = jnp.where(d <= 0, jnp.where(d > -WINDOW, 0.0, NEG), NEG)
        return update_of(buf, wt, load_vt(vwt_ref, k0, wt), bias, mw_ref, accw_ref)

    def sel_scores(buf, kt):
        k0 = pl.multiple_of(kt * tk, tk)
        chunk = sel_ref[pl.ds(pl.multiple_of(kt * blocks_per_tile, blocks_per_tile), blocks_per_tile), :]
        bias = jnp.concatenate([(1.0 - chunk) * NEG, jnp.zeros((bias_rows - blocks_per_tile, tq), F32)], axis=0)
        qa_ref[hd:hd + bias_rows, :] = heads_on_lanes(bias).astype(BF16)
        return scores_of(buf, tk, ks_ref[0, pl.ds(k0, tk), :])

    def sel_update(buf, kt, causal):
        k0 = pl.multiple_of(kt * tk, tk)
        bias = None
        if causal:
            bias = jnp.where(k0 + lax.broadcasted_iota(jnp.int32, (tk, tq), 0) <= tpos, 0.0, NEG)
        return update_of(buf, tk, load_vt(vst_ref, k0, tk), bias, m_ref, acc_ref)

    stage(win_scores(0, 0), None)
    stage(win_scores(1, 1), win_update(0, 0))
    stage(sel_scores(0, 0), win_update(1, 1))

    n_last = (q0 + tq + tk - 1) // tk - 1

    def trip(first, tiles):
        for i in range(tiles):
            stage(sel_scores((i + 1) % 2, first + i + 1), sel_update(i % 2, first + i, False))

    def quad(j, _):
        trip(4 * j, 4)
        return 0

    lax.fori_loop(0, n_last // 4, quad, 0)
    done4 = (n_last // 4) * 4

    def pair(j, _):
        trip(done4 + 2 * j, 2)
        return 0

    lax.fori_loop(0, (n_last - done4) // 2, pair, 0)

    @pl.when(n_last % 2 == 0)
    def _():
        stage(None, sel_update(0, n_last, True))

    @pl.when(n_last % 2 == 1)
    def _():
        stage(sel_scores(1, n_last), sel_update(0, n_last - 1, False))
        stage(None, sel_update(1, n_last, True))

    o_sel = acc_ref[0:hd, :] / acc_ref[hd:hd + 1, :]
    o_win = accw_ref[0:hd, :] / accw_ref[hd:hd + 1, :]

    def gate_row(branch):
        return jnp.concatenate([gates[GATE_COL + 3 * h + branch:GATE_COL + 3 * h + branch + 1, :]
                                for h in range(nh)], axis=1)

    out = o_cmp * gate_row(0) + o_sel * gate_row(1) + o_win * gate_row(2)
    o_ref[0] = jnp.concatenate([out[:, h * tq:(h + 1) * tq] for h in range(nh)], axis=0).T


def _nsa(q, dtg, kc, vct, ks, vst, kw, vwt, ovt):
    b, s, _ = q.shape
    tq = NSA_TQ
    nc = s // CMP_STRIDE
    per_b3 = lambda shape: pl.BlockSpec((1,) + shape, lambda i, j: (i, 0, 0))
    per_b4 = lambda shape: pl.BlockSpec((1,) + shape, lambda i, j: (i, 0, 0, 0))
    qblk = lambda w: pl.BlockSpec((1, tq, w), lambda i, j: (i, j, 0))
    return pl.pallas_call(
        functools.partial(_nsa_kernel, seq=s),
        grid=(b, s // tq),
        in_specs=[
            qblk(NSA_WIDTH), qblk(LANES),
            per_b3((nc, LANES)), per_b3((NSA_HEAD_DIM, nc)),
            per_b3((s, LANES)), per_b4((s // LANES, NSA_HEAD_DIM, LANES)),
            per_b3((s, LANES)), per_b4((s // LANES, NSA_HEAD_DIM, LANES)),
            pl.BlockSpec(ovt.shape, lambda i, j: (0, 0)),
        ],
        out_specs=qblk(NSA_WIDTH),
        out_shape=jax.ShapeDtypeStruct((b, s, NSA_WIDTH), F32),
        scratch_shapes=[
            pltpu.VMEM((s // SEL_BLOCK, tq), F32),
            pltpu.VMEM((LANES, NSA_HEADS * tq), BF16),
            pltpu.VMEM((NSA_HEAD_DIM + 2 * SUBLANES, NSA_HEADS * tq), F32),
            pltpu.VMEM((1, NSA_HEADS * tq), F32),
            pltpu.VMEM((NSA_HEAD_DIM + 2 * SUBLANES, NSA_HEADS * tq), F32),
            pltpu.VMEM((1, NSA_HEADS * tq), F32),
            pltpu.VMEM((2, NSA_TK, NSA_HEADS * tq), F32),
        ],
        compiler_params=pltpu.CompilerParams(
            dimension_semantics=("arbitrary", "arbitrary"), vmem_limit_bytes=VMEM_LIMIT_BYTES),
        name="nsa",
    )(q, dtg, kc, vct, ks, vst, kw, vwt, ovt)


def _ffn_kernel(x_ref, ya_ref, yn_ref, xn_ref, yan_ref, ynn_ref, wo_ref, nw_ref, wu_ref, cw_ref, cb_ref, wd_ref,
                fw_ref, o_ref, tail_ref, act_ref, x1_ref, h_ref, *, tiles_per_seq, final_norm):
    tm = x_ref.shape[0]
    na = ya_ref.shape[1]

    def mixed_residual(xr, yar, ynr):
        x1 = (xr[...] + _dot(yar[...].astype(BF16), wo_ref[0:na, :])
              + _dot(ynr[...].astype(BF16), wo_ref[na:, :]))
        ms = jnp.mean(x1 * x1, axis=-1, keepdims=True)
        return x1, (x1 * lax.rsqrt(ms + EPS) * nw_ref[...]).astype(BF16)

    @pl.when(pl.program_id(0) % tiles_per_seq == 0)
    def _():
        tail_ref[...] = jnp.zeros_like(tail_ref)

    @pl.when(pl.program_id(0) == 0)
    def _():
        x1_0, h_0 = mixed_residual(x_ref, ya_ref, yn_ref)
        x1_ref[...] = x1_0
        h_ref[...] = h_0

    h = h_ref[...]

    def conv_cols(c0):
        cols = slice(c0, c0 + FF_TILE)
        u = _dot(h, wu_ref[:, cols])
        tail = tail_ref[:, cols]
        cv = cb_ref[:, cols] + cw_ref[FFN_CONV - 1:FFN_CONV, cols] * u
        for t in range(FFN_CONV - 1):
            cv = cv + cw_ref[t:t + 1, cols] * _shift_rows(u, tail, FFN_CONV - 1 - t)
        tail_ref[:, cols] = u[tm - SUBLANES:tm]
        return cv

    for j in range(N_FF_TILES):
        gate = conv_cols(j * FF_TILE)
        val = conv_cols(D_FF_PAD + j * FF_TILE)
        act_ref[:, j * FF_TILE:(j + 1) * FF_TILE] = (_silu(gate) * val).astype(BF16)

    x1_next, h_next = mixed_residual(xn_ref, yan_ref, ynn_ref)
    x2 = x1_ref[...] + _dot(act_ref[...], wd_ref[...])
    if final_norm:
        ms2 = jnp.mean(x2 * x2, axis=-1, keepdims=True)
        x2 = x2 * lax.rsqrt(ms2 + EPS) * fw_ref[...]
    o_ref[...] = x2
    x1_ref[...] = x1_next
    h_ref[...] = h_next


def _ffn(x2d, ya, yn, wo, nw, wu_t, cw_t, cb_t, wd, fw, seq, final_norm):
    t = x2d.shape[0]
    tm = FFN_TM
    last = t // tm - 1
    row = lambda i: (i, 0)
    nxt = lambda i: (jnp.minimum(i + 1, last), 0)
    c2 = lambda a: pl.BlockSpec(a.shape, lambda i: (0, 0))
    return pl.pallas_call(
        functools.partial(_ffn_kernel, tiles_per_seq=seq // tm, final_norm=final_norm),
        grid=(t // tm,),
        in_specs=[
            pl.BlockSpec((tm, D_MODEL), row),
            pl.BlockSpec((tm, ya.shape[1]), row),
            pl.BlockSpec((tm, yn.shape[1]), row),
            pl.BlockSpec((tm, D_MODEL), nxt),
            pl.BlockSpec((tm, ya.shape[1]), nxt),
            pl.BlockSpec((tm, yn.shape[1]), nxt),
            c2(wo), c2(nw), c2(wu_t), c2(cw_t), c2(cb_t), c2(wd), c2(fw),
        ],
        out_specs=pl.BlockSpec((tm, D_MODEL), row),
        out_shape=jax.ShapeDtypeStruct((t, D_MODEL), F32),
        scratch_shapes=[
            pltpu.VMEM((SUBLANES, 2 * D_FF_PAD), F32),
            pltpu.VMEM((tm, D_FF_PAD), BF16),
            pltpu.VMEM((tm, D_MODEL), F32),
            pltpu.VMEM((tm, D_MODEL), BF16),
        ],
        compiler_params=pltpu.CompilerParams(
            dimension_semantics=("arbitrary",), vmem_limit_bytes=VMEM_LIMIT_BYTES),
        name="outproj_ffn",
    )(x2d, ya, yn, x2d, ya, yn, wo, nw, wu_t, cw_t, cb_t, wd, fw)


def _rope_tables(seq):
    half = NSA_HEAD_DIM // 2
    inv = 1.0 / (ROPE_THETA ** (jnp.arange(half, dtype=F32) / half))
    ang = jnp.arange(seq, dtype=F32)[:, None] * inv[None, :]
    cos, sin = jnp.cos(ang), jnp.sin(ang)
    scale = NSA_HEAD_DIM ** -0.5 * LOG2E
    one = jnp.ones_like(cos)
    zero = jnp.zeros_like(cos)
    rqc = jnp.concatenate([cos, cos, cos, cos], axis=1) * scale
    rqs = jnp.concatenate([-sin, sin, -sin, sin], axis=1) * scale
    rkc = jnp.concatenate([cos, cos, one, one], axis=1)
    rks = jnp.concatenate([-sin, sin, zero, zero], axis=1)
    return rqc, rqs, rkc, rks


def _compress_weights(kw1, kw2, vw1, vw2):
    half_tokens = CMP_BLOCK // 2
    hd, hid = NSA_HEAD_DIM, CMP_HIDDEN
    kw1r = kw1.reshape(2, half_tokens, hd, hid)
    vw1r = vw1.reshape(2, half_tokens, hd, hid)
    zeros = jnp.zeros((half_tokens, hd, hid), kw1.dtype)
    cols = []
    for w1r, is_k in ((kw1r, True), (vw1r, False)):
        for part in range(2):
            blk = w1r[part]
            rows = jnp.concatenate([blk, zeros] if is_k else [zeros, blk], axis=1)
            cols.append(rows.reshape(half_tokens * LANES, hid))
    wc = jnp.concatenate(cols, axis=1).astype(BF16)
    zk = jnp.zeros((hid, hd), kw2.dtype)
    w2bd = jnp.concatenate([jnp.concatenate([kw2, zk], axis=1),
                            jnp.concatenate([zk, vw2], axis=1)], axis=0).astype(BF16)
    return wc, w2bd


def _overlap_t(seq):
    nc = seq // CMP_STRIDE
    nb = seq // SEL_BLOCK
    cmp_start = np.arange(nc) * CMP_STRIDE
    slc_start = np.arange(nb) * SEL_BLOCK
    ov = ((cmp_start[None, :] < slc_start[:, None] + SEL_BLOCK)
          & (cmp_start[None, :] + CMP_BLOCK > slc_start[:, None])
          & (np.arange(nc)[None, :] < nc - 1))
    return jnp.asarray(ov.astype(np.float32), dtype=BF16)


def _pad_rows(a, rows):
    return jnp.concatenate([a, jnp.zeros((rows - a.shape[0],) + a.shape[1:], a.dtype)], axis=0)


def _pad_cols(a, cols):
    return jnp.concatenate([a, jnp.zeros(a.shape[:-1] + (cols - a.shape[-1],), a.dtype)], axis=-1)


def _ff_tiles(a):
    return jnp.concatenate([_pad_cols(a[:, :D_FF], D_FF_PAD), _pad_cols(a[:, D_FF:], D_FF_PAD)], axis=1)


def kernel(x, attn_norm_w, w_in, ssd_conv_w, ssd_conv_b, ssd_dt_bias, ssd_a_log, ssd_d, ssd_norm_w, sc_conv_w,
           cmp_k_pos, cmp_k_w1, cmp_k_w2, cmp_v_pos, cmp_v_w1, cmp_v_w2, w_out, ffn_norm_w, ffn_w_up,
           ffn_conv_w, ffn_conv_b, ffn_w_down, final_norm_w):
    b, s, d = x.shape
    depth = w_in.shape[0]
    assert d == D_MODEL and s % NSA_TK == 0 and s // SEL_BLOCK >= SEL_TOPK and s % INPROJ_TM == 0
    assert INPROJ_TM % NSA_TK == 0 and s >= NSA_WKEYS and NSA_TK // SEL_BLOCK <= SUBLANES
    t = b * s
    ropes = _rope_tables(s)
    ovt = _overlap_t(s)
    nc = s // CMP_STRIDE
    half_tokens = CMP_BLOCK // 2
    x2d = x.reshape(t, d)
    w_in_t = jnp.transpose(w_in, (0, 2, 1))
    for l in range(depth):
        cw = _pad_rows(ssd_conv_w[l], SUBLANES)
        cb = ssd_conv_b[l][None, :]
        dtb = _pad_cols(ssd_dt_bias[l][None, :], LANES)
        aneg = _pad_cols(-jnp.exp(ssd_a_log[l].astype(F32))[None, :] * LOG2E, LANES)
        dsk = jnp.repeat(ssd_d[l].astype(F32), SSD_HEAD_DIM)[None, :]
        scw = _pad_rows(sc_conv_w[l], SUBLANES)
        wc, w2bd = _compress_weights(cmp_k_w1[l], cmp_k_w2[l], cmp_v_w1[l], cmp_v_w2[l])
        pos2 = _pad_rows(jnp.concatenate([cmp_k_pos[l], cmp_v_pos[l]], axis=1).reshape(2, half_tokens * LANES),
                         SUBLANES)
        wo = w_out[l].astype(BF16)
        wu_t = _ff_tiles(ffn_w_up[l]).astype(BF16)
        cw_t = _ff_tiles(_pad_rows(ffn_conv_w[l], SUBLANES))
        cb_t = _ff_tiles(ffn_conv_b[l][None, :])
        wd = _pad_rows(ffn_w_down[l], D_FF_PAD).astype(BF16)

        z, xbc, sc, q, cpair, ks, kw, vst, vwt, dtg = _inproj(x2d, attn_norm_w[l][None, :], w_in_t, l, ropes, s)
        kc, vct = _compress(cpair.reshape(b, nc, half_tokens * LANES), pos2, wc, w2bd)
        r3 = lambda a: a.reshape(b, s, a.shape[-1])
        ya = _ssd(r3(z), r3(xbc), r3(sc), r3(dtg), cw, cb, dtb, aneg, dsk, ssd_norm_w[l][None, :], scw)
        vt4 = lambda a: a.reshape(b, s // LANES, NSA_HEAD_DIM, LANES)
        yn = _nsa(r3(q), r3(dtg), kc, vct, r3(ks), vt4(vst), r3(kw), vt4(vwt), ovt)
        x2d = _ffn(x2d, ya.reshape(t, -1), yn.reshape(t, -1), wo, ffn_norm_w[l][None, :], wu_t, cw_t, cb_t, wd,
                   final_norm_w[None, :], s, l == depth - 1)
    return x2d.reshape(b, s, d)
```

```python
import functools
import math

import numpy as np
import jax
import jax.numpy as jnp
from jax import lax
from jax.experimental import pallas as pl
from jax.experimental.pallas import tpu as pltpu

F32 = jnp.float32
BF16 = jnp.bfloat16

D_MODEL = 1024
SSD_HEADS = 8
SSD_HEAD_DIM = 64
SSD_INNER = SSD_HEADS * SSD_HEAD_DIM
SSD_GROUPS = 2
SSD_STATE = 128
SSD_CONV = 4
SSD_CHUNK = 256
SSD_XBC = SSD_INNER + 2 * SSD_GROUPS * SSD_STATE
SC_WIDTH = 256
SC_CONV = 3
NSA_HEADS = 4
NSA_HEAD_DIM = 64
NSA_WIDTH = NSA_HEADS * NSA_HEAD_DIM
CMP_BLOCK = 32
CMP_STRIDE = 16
CMP_HIDDEN = 128
SEL_BLOCK = 64
SEL_TOPK = 16
SEL_LOCAL = 2
WINDOW = 512
ROPE_THETA = 10000.0
D_FF = 2752
FFN_CONV = 3
EPS = 1e-6
NEG = -1e30
FORCE = 1e9
REMOVED = -3.0e38
LOG2E = 1.4426950408889634

LANES = 128
SUBLANES = 8
VMEM_LIMIT_BYTES = 56 * 1024 * 1024

D_FF_PAD = 2816
FF_TILE = 256
N_FF_TILES = D_FF_PAD // FF_TILE

COL_Z = 0
COL_XBC = COL_Z + SSD_INNER
COL_SC = COL_XBC + SSD_XBC
COL_Q = COL_SC + 3 * SC_WIDTH
COL_KV = COL_Q + NSA_WIDTH
COL_DTG = COL_KV + 6 * NSA_HEAD_DIM
N_PACK = COL_DTG + LANES
GATE_COL = SSD_HEADS
IN_DT0 = SSD_INNER + SSD_XBC
IN_DT1 = IN_DT0 + SSD_HEADS
IN_G0 = IN_DT1 + 3 * SC_WIDTH + NSA_WIDTH + 6 * NSA_HEAD_DIM
IN_G1 = IN_G0 + 3 * NSA_HEADS

INPROJ_TM = 512
NSA_TQ = 256
NSA_TK = 512
NSA_WKEYS = WINDOW + NSA_TQ
FFN_TM = 256


def _dot(a, b):
    return jnp.dot(a, b, preferred_element_type=F32)


def _dot_nt(a, b):
    return lax.dot_general(a, b, (((1,), (1,)), ((), ())), preferred_element_type=F32)


def _sigmoid(x):
    return 1.0 / (1.0 + jnp.exp(-x))


def _silu(x):
    h = 0.5 * x
    return h + h * jnp.tanh(h)


def _softplus(x):
    return jnp.maximum(x, 0.0) + jnp.log1p(jnp.exp(-jnp.abs(x)))


def _split3(a):
    a1 = a.astype(BF16)
    r1 = a - a1.astype(F32)
    a2 = r1.astype(BF16)
    r2 = r1 - a2.astype(F32)
    return a1, a2, r2.astype(BF16)


def _shift_rows(cur, tail, k):
    if k == 0:
        return cur
    rc = pltpu.roll(cur, k, 0)
    rt = pltpu.roll(tail, k, 0)
    row = lax.broadcasted_iota(jnp.int32, tail.shape, 0)
    first = jnp.where(row < k, rt, rc[0:SUBLANES])
    return jnp.concatenate([first, rc[SUBLANES:]], axis=0)


def _inproj_kernel(x_ref, nw_ref, w_ref, rqc_ref, rqs_ref, rkc_ref, rks_ref, cw_ref, cb_ref, scw_ref,
                   z_ref, xbc_ref, ysc_ref, q_ref, cpair_ref, ks_ref, kw_ref, vst_ref, vwt_ref, dtg_ref,
                   wb_ref, xtail_ref, stail_ref, *, tiles_per_seq):
    tm = x_ref.shape[0]

    @pl.when(pl.program_id(0) % tiles_per_seq == 0)
    def _():
        xtail_ref[...] = jnp.zeros_like(xtail_ref)
        stail_ref[...] = jnp.zeros_like(stail_ref)

    @pl.when(pl.program_id(0) == 0)
    def _():
        def cast_rows(dst, src, n):
            for r in range(0, n, LANES):
                m = min(LANES, n - r)
                wb_ref[dst + r:dst + r + m, :] = w_ref[0, src + r:src + r + m, :].astype(BF16)

        cast_rows(0, 0, IN_DT0)
        cast_rows(IN_DT0, IN_DT1, IN_G0 - IN_DT1)
        tail = jnp.concatenate([w_ref[0, IN_DT0:IN_DT1, :], w_ref[0, IN_G0:IN_G1, :],
                                jnp.zeros((LANES - (IN_DT1 - IN_DT0) - (IN_G1 - IN_G0), D_MODEL), F32)], axis=0)
        wb_ref[COL_DTG:N_PACK, :] = tail.astype(BF16)

    x = x_ref[...]
    ms = jnp.mean(x * x, axis=-1, keepdims=True)
    h = (x * lax.rsqrt(ms + EPS) * nw_ref[...]).astype(BF16)

    def proj(a, b):
        return _dot_nt(h, wb_ref[a:b, :])

    def causal_conv(u, w_ref, width, tail_ref, cols=slice(None)):
        tail = tail_ref[:, cols]
        cv = w_ref[width - 1:width, cols] * u
        for j in range(width - 1):
            cv = cv + w_ref[j:j + 1, cols] * _shift_rows(u, tail, width - 1 - j)
        tail_ref[:, cols] = u[tm - SUBLANES:tm]
        return cv

    lane = lax.broadcasted_iota(jnp.int32, (tm, LANES), 1)
    first_half = (lane % NSA_HEAD_DIM) < (NSA_HEAD_DIM // 2)

    def rope(v, c, s):
        partner = jnp.where(first_half, pltpu.roll(v, LANES - 32, 1), pltpu.roll(v, 32, 1))
        return v * c + partner * s

    def finish_z(zv):
        z_ref[...] = _silu(zv)

    def finish_xbc(c0):
        cols = slice(c0, c0 + 2 * LANES)

        def fin(u):
            xbc_ref[:, cols] = _silu(causal_conv(u, cw_ref, SSD_CONV, xtail_ref, cols) + cb_ref[:, cols])
        return fin

    def finish_sc(sc):
        ysc_ref[...] = sc[:, 0:SC_WIDTH] * causal_conv(
            sc[:, SC_WIDTH:2 * SC_WIDTH] * sc[:, 2 * SC_WIDTH:3 * SC_WIDTH], scw_ref, SC_CONV, stail_ref)

    def finish_q(q):
        rqc, rqs = rqc_ref[...], rqs_ref[...]
        q_ref[:, 0:LANES] = rope(q[:, 0:LANES], rqc, rqs)
        q_ref[:, LANES:2 * LANES] = rope(q[:, LANES:2 * LANES], rqc, rqs)

    def finish_kv(kv):
        rkc, rks = rkc_ref[...], rks_ref[...]
        cpair_ref[...] = rope(kv[:, 0:LANES], rkc, rks)
        row = lax.broadcasted_iota(jnp.int32, (tm, LANES), 0)
        block_in_tile = (row % NSA_TK) // SEL_BLOCK
        onehot = jnp.where(lane - NSA_HEAD_DIM == block_in_tile, 1.0, 0.0)
        k_lanes = lane < NSA_HEAD_DIM
        for pair, k_ref, vt_ref, fill in ((1, ks_ref, vst_ref, onehot), (2, kw_ref, vwt_ref, 0.0)):
            p = rope(kv[:, pair * LANES:(pair + 1) * LANES], rkc, rks)
            k_ref[...] = jnp.where(k_lanes, p, fill).astype(BF16)
            pt = p.T
            for j in range(tm // LANES):
                vt_ref[j] = pt[NSA_HEAD_DIM:LANES, j * LANES:(j + 1) * LANES].astype(BF16)

    def finish_dtg(v):
        dtg_ref[...] = v

    groups = [(COL_Z, COL_XBC, finish_z)]
    groups += [(COL_XBC + c0, COL_XBC + c0 + 2 * LANES, finish_xbc(c0)) for c0 in range(0, SSD_XBC, 2 * LANES)]
    groups += [(COL_SC, COL_Q, finish_sc), (COL_Q, COL_KV, finish_q), (COL_KV, COL_DTG, finish_kv),
               (COL_DTG, N_PACK, finish_dtg)]
    pending = None
    for a, b, fin in groups:
        val = proj(a, b)
        if pending is not None:
            pending[0](pending[1])
        pending = (fin, val)
    pending[0](pending[1])


def _inproj(x2d, nw, w_t, layer, ropes, cw, cb, scw, seq):
    t = x2d.shape[0]
    tm = INPROJ_TM
    nt = t // tm
    pos_blocks = seq // tm
    row = lambda i: (i, 0)
    pos = lambda i: (i % pos_blocks, 0)
    const = lambda i: (0, 0)
    out_shapes = (
        jax.ShapeDtypeStruct((t, SSD_INNER), F32),
        jax.ShapeDtypeStruct((t, SSD_XBC), F32),
        jax.ShapeDtypeStruct((t, SC_WIDTH), F32),
        jax.ShapeDtypeStruct((t, NSA_WIDTH), F32),
        jax.ShapeDtypeStruct((t, LANES), F32),
        jax.ShapeDtypeStruct((t, LANES), BF16),
        jax.ShapeDtypeStruct((t, LANES), BF16),
        jax.ShapeDtypeStruct((t // LANES, NSA_HEAD_DIM, LANES), BF16),
        jax.ShapeDtypeStruct((t // LANES, NSA_HEAD_DIM, LANES), BF16),
        jax.ShapeDtypeStruct((t, LANES), F32),
    )
    vt_spec = pl.BlockSpec((tm // LANES, NSA_HEAD_DIM, LANES), lambda i: (i, 0, 0))
    out_specs = (
        pl.BlockSpec((tm, SSD_INNER), row),
        pl.BlockSpec((tm, SSD_XBC), row),
        pl.BlockSpec((tm, SC_WIDTH), row),
        pl.BlockSpec((tm, NSA_WIDTH), row),
        pl.BlockSpec((tm, LANES), row),
        pl.BlockSpec((tm, LANES), row),
        pl.BlockSpec((tm, LANES), row),
        vt_spec,
        vt_spec,
        pl.BlockSpec((tm, LANES), row),
    )
    in_specs = [
        pl.BlockSpec((tm, D_MODEL), row),
        pl.BlockSpec((1, D_MODEL), const),
        pl.BlockSpec((1, IN_G1, D_MODEL), lambda i: (layer, 0, 0), pipeline_mode=pl.Buffered(1)),
        pl.BlockSpec((tm, LANES), pos),
        pl.BlockSpec((tm, LANES), pos),
        pl.BlockSpec((tm, LANES), pos),
        pl.BlockSpec((tm, LANES), pos),
        pl.BlockSpec(cw.shape, const),
        pl.BlockSpec(cb.shape, const),
        pl.BlockSpec(scw.shape, const),
    ]
    return pl.pallas_call(
        functools.partial(_inproj_kernel, tiles_per_seq=pos_blocks),
        grid=(nt,),
        in_specs=in_specs,
        out_specs=out_specs,
        out_shape=out_shapes,
        scratch_shapes=[
            pltpu.VMEM((N_PACK, D_MODEL), BF16),
            pltpu.VMEM((SUBLANES, SSD_XBC), F32),
            pltpu.VMEM((SUBLANES, SC_WIDTH), F32),
        ],
        compiler_params=pltpu.CompilerParams(
            dimension_semantics=("arbitrary",), vmem_limit_bytes=VMEM_LIMIT_BYTES),
        name="inproj",
    )(x2d, nw, w_t, *ropes, cw, cb, scw)


def _compress_kernel(x2_ref, pos_ref, wc_ref, w2_ref, kc_ref, vct_ref):
    nc = x2_ref.shape[1]
    x2 = x2_ref[0].astype(BF16)
    wc = wc_ref[...]
    y = _dot(x2, wc)
    r = _dot(pos_ref[...].astype(BF16), wc)
    h = CMP_HIDDEN

    def pre(base):
        bias = r[0:1, base:base + h] + r[1:2, base + h:base + 2 * h]
        return y[:, base:base + h] + pltpu.roll(y[:, base + h:base + 2 * h], nc - 1, 0) + bias

    hid = jnp.concatenate([jax.nn.gelu(pre(0)), jax.nn.gelu(pre(2 * h))], axis=1).astype(BF16)
    o = _dot(hid, w2_ref[...])
    kc_ref[0] = o.astype(BF16)
    vct_ref[0] = o.T[NSA_HEAD_DIM:LANES, :].astype(BF16)


def _compress(x2p, pos2, wc, w2bd):
    b, nc, width = x2p.shape
    return pl.pallas_call(
        _compress_kernel,
        grid=(b,),
        in_specs=[
            pl.BlockSpec((1, nc, width), lambda i: (i, 0, 0)),
            pl.BlockSpec(pos2.shape, lambda i: (0, 0)),
            pl.BlockSpec(wc.shape, lambda i: (0, 0)),
            pl.BlockSpec(w2bd.shape, lambda i: (0, 0)),
        ],
        out_specs=(
            pl.BlockSpec((1, nc, LANES), lambda i: (i, 0, 0)),
            pl.BlockSpec((1, NSA_HEAD_DIM, nc), lambda i: (i, 0, 0)),
        ),
        out_shape=(
            jax.ShapeDtypeStruct((b, nc, LANES), BF16),
            jax.ShapeDtypeStruct((b, NSA_HEAD_DIM, nc), BF16),
        ),
        compiler_params=pltpu.CompilerParams(
            dimension_semantics=("arbitrary",), vmem_limit_bytes=VMEM_LIMIT_BYTES),
        name="compress",
    )(x2p, pos2, wc, w2bd)


def _ssd_kernel(z_ref, xbc_ref, dtg_ref, dtb_ref, aneg_ref, dsk_ref, nw_ref, y_ref, h_ref):
    L = SSD_CHUNK
    half = SSD_HEAD_DIM

    @pl.when(pl.program_id(1) == 0)
    def _():
        h_ref[...] = jnp.zeros_like(h_ref)

    xc = xbc_ref[0]

    dt = _softplus(dtg_ref[0] + dtb_ref[...])
    a = dt * aneg_ref[...]
    row = lax.broadcasted_iota(jnp.int32, (L, L), 0)
    col = lax.broadcasted_iota(jnp.int32, (L, L), 1)
    causal = row >= col
    tri = jnp.where(causal, 1.0, 0.0).astype(BF16)
    a1, a2, a3 = _split3(a)
    acs = _dot(tri, a1) + _dot(tri, a2) + _dot(tri, a3)
    acs_t = acs.T
    exp_acs = jnp.exp2(acs)
    last = acs[L - 1:L, :]
    dte = jnp.exp2(last - acs)
    chunk_decay = jnp.exp2(last)

    lo = lax.broadcasted_iota(jnp.int32, (L, LANES), 1) < half
    lo_row = lax.broadcasted_iota(jnp.int32, (1, LANES), 1) < half

    def per_lane(m, ha, hb, mask):
        return jnp.where(mask, m[:, ha:ha + 1], m[:, hb:hb + 1])

    ys = []
    for g in range(SSD_GROUPS):
        b_g = xc[:, SSD_INNER + g * SSD_STATE:SSD_INNER + (g + 1) * SSD_STATE]
        c_g = xc[:, SSD_INNER + (SSD_GROUPS + g) * SSD_STATE:SSD_INNER + (SSD_GROUPS + g + 1) * SSD_STATE]
        b_bf = b_g.astype(BF16)
        c_bf = c_g.astype(BF16)
        cb = _dot_nt(c_bf, b_bf)
        bt_bf = b_g.T.astype(BF16)
        for pp in range(SSD_HEADS // SSD_GROUPS // 2):
            p = g * (SSD_HEADS // SSD_GROUPS // 2) + pp
            ha, hb = 2 * p, 2 * p + 1
            x_pair = xc[:, p * LANES:(p + 1) * LANES]
            xdt = x_pair * per_lane(dt, ha, hb, lo)
            xdt_bf = xdt.astype(BF16)
            yd = []
            for hd in (ha, hb):
                seg = acs[:, hd:hd + 1] - acs_t[hd:hd + 1, :]
                decay = jnp.exp2(jnp.where(causal, seg, NEG))
                yd.append(_dot((decay * cb).astype(BF16), xdt_bf))
            y_diag = jnp.where(lo, yd[0], yd[1])
            st = _dot(bt_bf, (xdt * per_lane(dte, ha, hb, lo)).astype(BF16))
            h_prev = h_ref[p]
            y_off = _dot(c_bf, h_prev.astype(BF16)) * per_lane(exp_acs, ha, hb, lo)
            h_ref[p] = h_prev * per_lane(chunk_decay, ha, hb, lo_row) + st
            ys.append(y_diag + y_off + x_pair * dsk_ref[:, p * LANES:(p + 1) * LANES])
    y = jnp.concatenate(ys, axis=1)
    y = y * z_ref[0]
    ms = jnp.mean(y * y, axis=-1, keepdims=True)
    y_ref[0] = y * lax.rsqrt(ms + EPS) * nw_ref[...]


def _ssd(zs, xc, dtg, dtb, aneg, dsk, nw):
    b, s, _ = zs.shape
    L = SSD_CHUNK
    blk = lambda w: pl.BlockSpec((1, L, w), lambda i, c: (i, c, 0))
    par = lambda a: pl.BlockSpec(a.shape, lambda i, c: (0, 0))
    return pl.pallas_call(
        _ssd_kernel,
        grid=(b, s // L),
        in_specs=[blk(SSD_INNER), blk(SSD_XBC), blk(LANES), par(dtb), par(aneg), par(dsk), par(nw)],
        out_specs=blk(SSD_INNER),
        out_shape=jax.ShapeDtypeStruct((b, s, SSD_INNER), F32),
        scratch_shapes=[pltpu.VMEM((SSD_HEADS // 2, SSD_STATE, LANES), F32)],
        compiler_params=pltpu.CompilerParams(
            dimension_semantics=("arbitrary", "arbitrary"), vmem_limit_bytes=VMEM_LIMIT_BYTES),
        name="ssd",
    )(zs, xc, dtg, dtb, aneg, dsk, nw)


def _nsa_kernel(q_ref, dtg_ref, kc_ref, vct_ref, ks_ref, vst_ref, kw_ref, vwt_ref, ovt_ref,
                o_ref, sel_ref, qa_ref, acc_ref, m_ref, accw_ref, mw_ref, s_ref, *, seq):
    tq, tk = NSA_TQ, NSA_TK
    hd = NSA_HEAD_DIM
    nh = NSA_HEADS
    nc = seq // CMP_STRIDE
    nb = seq // SEL_BLOCK
    q0 = pl.program_id(1) * tq
    tpos = q0 + lax.broadcasted_iota(jnp.int32, (1, tq), 1)

    def heads_on_lanes(a):
        return jnp.concatenate([a] * nh, axis=1)

    qt = q_ref[0].T
    qa_ref[0:hd, :] = jnp.concatenate([qt[h * hd:(h + 1) * hd, :] for h in range(nh)], axis=1).astype(BF16)
    qa_ref[hd:LANES, :] = jnp.zeros((LANES - hd, nh * tq), BF16)
    gates = _sigmoid(dtg_ref[0]).T

    def load_vt(ref, key0, width):
        first = key0 // LANES
        return jnp.concatenate([ref[0, first + i] for i in range(width // LANES)], axis=1)

    def select_blocks(psum):
        p1, p2, p3 = _split3(psum)
        ovt = ovt_ref[...]
        imp = _dot(ovt, p1) + _dot(ovt, p2) + _dot(ovt, p3)
        j_io = lax.broadcasted_iota(jnp.int32, (nb, tq), 0)
        cur = jnp.right_shift(tpos, int(math.log2(SEL_BLOCK)))
        valid = j_io <= cur
        forced = (j_io == 0) | (valid & (j_io > cur - SEL_LOCAL))
        v = jnp.where(forced, REMOVED, jnp.where(valid, imp, NEG))
        j_f = j_io.astype(F32)

        def extract(_, carry):
            v, sel = carry
            m = jnp.max(v, axis=0, keepdims=True)
            first = jnp.min(jnp.where(v == m, j_f, float(nb)), axis=0, keepdims=True)
            hit = j_f == first
            return jnp.where(hit, REMOVED, v), jnp.where(hit, 1.0, sel)

        _, sel = lax.fori_loop(0, SEL_TOPK - 1 - SEL_LOCAL, extract, (v, jnp.where(forced, 1.0, 0.0)),
                               unroll=True)
        sel_ref[...] = sel

    n_io = lax.broadcasted_iota(jnp.int32, (nc, tq), 0)
    cbias = jnp.where((n_io * CMP_STRIDE + (CMP_BLOCK - 1)) <= tpos, 0.0, NEG)
    s = _dot(kc_ref[0], qa_ref[...]) + heads_on_lanes(cbias)
    p = jnp.exp2(s - jnp.max(s, axis=0, keepdims=True))
    has_key = heads_on_lanes(jnp.where(tpos >= CMP_BLOCK - 1, 1.0, 0.0))
    p = p * (has_key / jnp.sum(p, axis=0, keepdims=True))
    o_cmp = _dot(vct_ref[0], p.astype(BF16))
    psum = p[:, 0:tq]
    for h in range(1, nh):
        psum = psum + p[:, h * tq:(h + 1) * tq]
    select_blocks(psum)

    wt = NSA_WKEYS // 2
    w0 = pl.multiple_of(jnp.maximum(q0 - WINDOW, 0), tq)
    blocks_per_tile = tk // SEL_BLOCK
    bias_rows = 2 * SUBLANES
    for ref in (acc_ref, accw_ref):
        ref[...] = jnp.zeros_like(ref)
    for ref in (m_ref, mw_ref):
        ref[...] = jnp.full(ref.shape, NEG, F32)

    def scores_of(buf, rows, k_tile):
        def one_head(h):
            lanes = slice(h * tq, (h + 1) * tq)
            s_ref[buf, 0:rows, lanes] = _dot(k_tile, qa_ref[:, lanes])
        return one_head

    def update_of(buf, rows, vt, bias, m_r, acc_r):
        ones = jnp.where(lax.broadcasted_iota(jnp.int32, (bias_rows, rows), 0) == 0, 1.0, 0.0).astype(BF16)
        vt_aug = jnp.concatenate([vt, ones], axis=0)

        def one_head(h):
            lanes = slice(h * tq, (h + 1) * tq)
            s = s_ref[buf, 0:rows, lanes]
            if bias is not None:
                s = s + bias
            m_old = m_r[:, lanes]
            m_new = jnp.maximum(m_old, jnp.max(s, axis=0, keepdims=True))
            p = jnp.exp2(s - m_new).astype(BF16)
            acc_r[:, lanes] = jnp.exp2(m_old - m_new) * acc_r[:, lanes] + _dot(vt_aug, p)
            m_r[:, lanes] = m_new
        return one_head

    def stage(score_fn, update_fn):
        for h in range(nh):
            if score_fn is not None:
                score_fn(h)
            if update_fn is not None:
                update_fn(h)

    def win_scores(buf, i):
        k0 = pl.multiple_of(w0 + i * wt, LANES)
        return scores_of(buf, wt, kw_ref[0, pl.ds(k0, wt), :])

    def win_update(buf, i):
        k0 = pl.multiple_of(w0 + i * wt, LANES)
        d = (k0 - q0) + (lax.broadcasted_iota(jnp.int32, (wt, tq), 0)
                         - lax.broadcasted_iota(jnp.int32, (wt, tq), 1))
        bias = jnp.where(d <= 0, jnp.where(d > -WINDOW, 0.0, NEG), NEG)
        return update_of(buf, wt, load_vt(vwt_ref, k0, wt), bias, mw_ref, accw_ref)

    def sel_scores(buf, kt):
        k0 = pl.multiple_of(kt * tk, tk)
        chunk = sel_ref[pl.ds(pl.multiple_of(kt * blocks_per_tile, blocks_per_tile), blocks_per_tile), :]
        bias = jnp.concatenate([(1.0 - chunk) * NEG, jnp.zeros((bias_rows - blocks_per_tile, tq), F32)], axis=0)
        qa_ref[hd:hd + bias_rows, :] = heads_on_lanes(bias).astype(BF16)
        return scores_of(buf, tk, ks_ref[0, pl.ds(k0, tk), :])

    def sel_update(buf, kt, causal):
        k0 = pl.multiple_of(kt * tk, tk)
        bias = None
        if causal:
            bias = jnp.where(k0 + lax.broadcasted_iota(jnp.int32, (tk, tq), 0) <= tpos, 0.0, NEG)
        return update_of(buf, tk, load_vt(vst_ref, k0, tk), bias, m_ref, acc_ref)

    stage(win_scores(0, 0), None)
    stage(win_scores(1, 1), win_update(0, 0))
    stage(sel_scores(0, 0), win_update(1, 1))

    n_last = (q0 + tq + tk - 1) // tk - 1

    def trip(first, tiles):
        for i in range(tiles):
            stage(sel_scores((i + 1) % 2, first + i + 1), sel_update(i % 2, first + i, False))

    def quad(j, _):
        trip(4 * j, 4)
        return 0

    lax.fori_loop(0, n_last // 4, quad, 0)
    done4 = (n_last // 4) * 4

    def pair(j, _):
        trip(done4 + 2 * j, 2)
        return 0

    lax.fori_loop(0, (n_last - done4) // 2, pair, 0)

    @pl.when(n_last % 2 == 0)
    def _():
        stage(None, sel_update(0, n_last, True))

    @pl.when(n_last % 2 == 1)
    def _():
        stage(sel_scores(1, n_last), sel_update(0, n_last - 1, False))
        stage(None, sel_update(1, n_last, True))

    o_sel = acc_ref[0:hd, :] / acc_ref[hd:hd + 1, :]
    o_win = accw_ref[0:hd, :] / accw_ref[hd:hd + 1, :]

    def gate_row(branch):
        return jnp.concatenate([gates[GATE_COL + 3 * h + branch:GATE_COL + 3 * h + branch + 1, :]
                                for h in range(nh)], axis=1)

    out = o_cmp * gate_row(0) + o_sel * gate_row(1) + o_win * gate_row(2)
    o_ref[0] = jnp.concatenate([out[:, h * tq:(h + 1) * tq] for h in range(nh)], axis=0).T


def _nsa(q, dtg, kc, vct, ks, vst, kw, vwt, ovt):
    b, s, _ = q.shape
    tq = NSA_TQ
    nc = s // CMP_STRIDE
    per_b3 = lambda shape: pl.BlockSpec((1,) + shape, lambda i, j: (i, 0, 0))
    per_b4 = lambda shape: pl.BlockSpec((1,) + shape, lambda i, j: (i, 0, 0, 0))
    qblk = lambda w: pl.BlockSpec((1, tq, w), lambda i, j: (i, j, 0))
    return pl.pallas_call(
        functools.partial(_nsa_kernel, seq=s),
        grid=(b, s // tq),
        in_specs=[
            qblk(NSA_WIDTH), qblk(LANES),
            per_b3((nc, LANES)), per_b3((NSA_HEAD_DIM, nc)),
            per_b3((s, LANES)), per_b4((s // LANES, NSA_HEAD_DIM, LANES)),
            per_b3((s, LANES)), per_b4((s // LANES, NSA_HEAD_DIM, LANES)),
            pl.BlockSpec(ovt.shape, lambda i, j: (0, 0)),
        ],
        out_specs=qblk(NSA_WIDTH),
        out_shape=jax.ShapeDtypeStruct((b, s, NSA_WIDTH), F32),
        scratch_shapes=[
            pltpu.VMEM((s // SEL_BLOCK, tq), F32),
            pltpu.VMEM((LANES, NSA_HEADS * tq), BF16),
            pltpu.VMEM((NSA_HEAD_DIM + 2 * SUBLANES, NSA_HEADS * tq), F32),
            pltpu.VMEM((1, NSA_HEADS * tq), F32),
            pltpu.VMEM((NSA_HEAD_DIM + 2 * SUBLANES, NSA_HEADS * tq), F32),
            pltpu.VMEM((1, NSA_HEADS * tq), F32),
            pltpu.VMEM((2, NSA_TK, NSA_HEADS * tq), F32),
        ],
        compiler_params=pltpu.CompilerParams(
            dimension_semantics=("arbitrary", "arbitrary"), vmem_limit_bytes=VMEM_LIMIT_BYTES),
        name="nsa",
    )(q, dtg, kc, vct, ks, vst, kw, vwt, ovt)


def _ffn_kernel(x_ref, ya_ref, yc_ref, yn_ref, xn_ref, yan_ref, ycn_ref, ynn_ref, wo_ref, nw_ref, wu_ref, cw_ref,
                cb_ref, wd_ref, fw_ref, o_ref, tail_ref, act_ref, x1_ref, h_ref, *, tiles_per_seq, final_norm):
    tm = x_ref.shape[0]

    def mixed_residual(xr, mixers):
        x1 = xr[...]
        r0 = 0
        for mref in mixers:
            x1 = x1 + _dot(mref[...].astype(BF16), wo_ref[r0:r0 + mref.shape[1], :])
            r0 += mref.shape[1]
        ms = jnp.mean(x1 * x1, axis=-1, keepdims=True)
        return x1, (x1 * lax.rsqrt(ms + EPS) * nw_ref[...]).astype(BF16)

    @pl.when(pl.program_id(0) % tiles_per_seq == 0)
    def _():
        tail_ref[...] = jnp.zeros_like(tail_ref)

    @pl.when(pl.program_id(0) == 0)
    def _():
        x1_0, h_0 = mixed_residual(x_ref, (ya_ref, yc_ref, yn_ref))
        x1_ref[...] = x1_0
        h_ref[...] = h_0

    h = h_ref[...]

    def conv_cols(c0):
        cols = slice(c0, c0 + FF_TILE)
        u = _dot(h, wu_ref[:, cols])
        tail = tail_ref[:, cols]
        cv = cb_ref[:, cols] + cw_ref[FFN_CONV - 1:FFN_CONV, cols] * u
        for t in range(FFN_CONV - 1):
            cv = cv + cw_ref[t:t + 1, cols] * _shift_rows(u, tail, FFN_CONV - 1 - t)
        tail_ref[:, cols] = u[tm - SUBLANES:tm]
        return cv

    for j in range(N_FF_TILES):
        gate = conv_cols(j * FF_TILE)
        val = conv_cols(D_FF_PAD + j * FF_TILE)
        act_ref[:, j * FF_TILE:(j + 1) * FF_TILE] = (_silu(gate) * val).astype(BF16)

    x1_next, h_next = mixed_residual(xn_ref, (yan_ref, ycn_ref, ynn_ref))
    x2 = x1_ref[...] + _dot(act_ref[...], wd_ref[...])
    if final_norm:
        ms2 = jnp.mean(x2 * x2, axis=-1, keepdims=True)
        x2 = x2 * lax.rsqrt(ms2 + EPS) * fw_ref[...]
    o_ref[...] = x2
    x1_ref[...] = x1_next
    h_ref[...] = h_next


def _ffn(x2d, ya, yc, yn, wo, nw, wu_t, cw_t, cb_t, wd, fw, seq, final_norm):
    t = x2d.shape[0]
    tm = FFN_TM
    last = t // tm - 1
    row = lambda i: (i, 0)
    nxt = lambda i: (jnp.minimum(i + 1, last), 0)
    c2 = lambda a: pl.BlockSpec(a.shape, lambda i: (0, 0))
    return pl.pallas_call(
        functools.partial(_ffn_kernel, tiles_per_seq=seq // tm, final_norm=final_norm),
        grid=(t // tm,),
        in_specs=[
            pl.BlockSpec((tm, D_MODEL), row),
            pl.BlockSpec((tm, ya.shape[1]), row),
            pl.BlockSpec((tm, yc.shape[1]), row),
            pl.BlockSpec((tm, yn.shape[1]), row),
            pl.BlockSpec((tm, D_MODEL), nxt),
            pl.BlockSpec((tm, ya.shape[1]), nxt),
            pl.BlockSpec((tm, yc.shape[1]), nxt),
            pl.BlockSpec((tm, yn.shape[1]), nxt),
            c2(wo), c2(nw), c2(wu_t), c2(cw_t), c2(cb_t), c2(wd), c2(fw),
        ],
        out_specs=pl.BlockSpec((tm, D_MODEL), row),
        out_shape=jax.ShapeDtypeStruct((t, D_MODEL), F32),
        scratch_shapes=[
            pltpu.VMEM((SUBLANES, 2 * D_FF_PAD), F32),
            pltpu.VMEM((tm, D_FF_PAD), BF16),
            pltpu.VMEM((tm, D_MODEL), F32),
            pltpu.VMEM((tm, D_MODEL), BF16),
        ],
        compiler_params=pltpu.CompilerParams(
            dimension_semantics=("arbitrary",), vmem_limit_bytes=VMEM_LIMIT_BYTES),
        name="outproj_ffn",
    )(x2d, ya, yc, yn, x2d, ya, yc, yn, wo, nw, wu_t, cw_t, cb_t, wd, fw)


def _rope_tables(seq):
    half = NSA_HEAD_DIM // 2
    inv = 1.0 / (ROPE_THETA ** (jnp.arange(half, dtype=F32) / half))
    ang = jnp.arange(seq, dtype=F32)[:, None] * inv[None, :]
    cos, sin = jnp.cos(ang), jnp.sin(ang)
    scale = NSA_HEAD_DIM ** -0.5 * LOG2E
    one = jnp.ones_like(cos)
    zero = jnp.zeros_like(cos)
    rqc = jnp.concatenate([cos, cos, cos, cos], axis=1) * scale
    rqs = jnp.concatenate([-sin, sin, -sin, sin], axis=1) * scale
    rkc = jnp.concatenate([cos, cos, one, one], axis=1)
    rks = jnp.concatenate([-sin, sin, zero, zero], axis=1)
    return rqc, rqs, rkc, rks


def _compress_weights(kw1, kw2, vw1, vw2):
    half_tokens = CMP_BLOCK // 2
    hd, hid = NSA_HEAD_DIM, CMP_HIDDEN
    kw1r = kw1.reshape(2, half_tokens, hd, hid)
    vw1r = vw1.reshape(2, half_tokens, hd, hid)
    zeros = jnp.zeros((half_tokens, hd, hid), kw1.dtype)
    cols = []
    for w1r, is_k in ((kw1r, True), (vw1r, False)):
        for part in range(2):
            blk = w1r[part]
            rows = jnp.concatenate([blk, zeros] if is_k else [zeros, blk], axis=1)
            cols.append(rows.reshape(half_tokens * LANES, hid))
    wc = jnp.concatenate(cols, axis=1).astype(BF16)
    zk = jnp.zeros((hid, hd), kw2.dtype)
    w2bd = jnp.concatenate([jnp.concatenate([kw2, zk], axis=1),
                            jnp.concatenate([zk, vw2], axis=1)], axis=0).astype(BF16)
    return wc, w2bd


def _overlap_t(seq):
    nc = seq // CMP_STRIDE
    nb = seq // SEL_BLOCK
    cmp_start = np.arange(nc) * CMP_STRIDE
    slc_start = np.arange(nb) * SEL_BLOCK
    ov = ((cmp_start[None, :] < slc_start[:, None] + SEL_BLOCK)
          & (cmp_start[None, :] + CMP_BLOCK > slc_start[:, None])
          & (np.arange(nc)[None, :] < nc - 1))
    return jnp.asarray(ov.astype(np.float32), dtype=BF16)


def _pad_rows(a, rows):
    return jnp.concatenate([a, jnp.zeros((rows - a.shape[0],) + a.shape[1:], a.dtype)], axis=0)


def _pad_cols(a, cols):
    return jnp.concatenate([a, jnp.zeros(a.shape[:-1] + (cols - a.shape[-1],), a.dtype)], axis=-1)


def _ff_tiles(a):
    return jnp.concatenate([_pad_cols(a[:, :D_FF], D_FF_PAD), _pad_cols(a[:, D_FF:], D_FF_PAD)], axis=1)


def kernel(x, attn_norm_w, w_in, ssd_conv_w, ssd_conv_b, ssd_dt_bias, ssd_a_log, ssd_d, ssd_norm_w, sc_conv_w,
           cmp_k_pos, cmp_k_w1, cmp_k_w2, cmp_v_pos, cmp_v_w1, cmp_v_w2, w_out, ffn_norm_w, ffn_w_up,
           ffn_conv_w, ffn_conv_b, ffn_w_down, final_norm_w):
    b, s, d = x.shape
    depth = w_in.shape[0]
    assert d == D_MODEL and s % NSA_TK == 0 and s // SEL_BLOCK >= SEL_TOPK and s % INPROJ_TM == 0
    assert INPROJ_TM % NSA_TK == 0 and s >= NSA_WKEYS and NSA_TK // SEL_BLOCK <= SUBLANES
    assert s // CMP_STRIDE <= NSA_TK
    t = b * s
    ropes = _rope_tables(s)
    ovt = _overlap_t(s)
    nc = s // CMP_STRIDE
    half_tokens = CMP_BLOCK // 2
    x2d = x.reshape(t, d)
    w_in_t = jnp.transpose(w_in, (0, 2, 1))
    for l in range(depth):
        cw = _pad_rows(ssd_conv_w[l], SUBLANES)
        cb = ssd_conv_b[l][None, :]
        dtb = _pad_cols(ssd_dt_bias[l][None, :], LANES)
        aneg = _pad_cols(-jnp.exp(ssd_a_log[l].astype(F32))[None, :] * LOG2E, LANES)
        dsk = jnp.repeat(ssd_d[l].astype(F32), SSD_HEAD_DIM)[None, :]
        scw = _pad_rows(sc_conv_w[l], SUBLANES)
        wc, w2bd = _compress_weights(cmp_k_w1[l], cmp_k_w2[l], cmp_v_w1[l], cmp_v_w2[l])
        pos2 = _pad_rows(jnp.concatenate([cmp_k_pos[l], cmp_v_pos[l]], axis=1).reshape(2, half_tokens * LANES),
                         SUBLANES)
        wo = w_out[l].astype(BF16)
        wu_t = _ff_tiles(ffn_w_up[l]).astype(BF16)
        cw_t = _ff_tiles(_pad_rows(ffn_conv_w[l], SUBLANES))
        cb_t = _ff_tiles(ffn_conv_b[l][None, :])
        wd = _pad_rows(ffn_w_down[l], D_FF_PAD).astype(BF16)

        zs, xc, ysc, q, cpair, ks, kw, vst, vwt, dtg = _inproj(x2d, attn_norm_w[l][None, :], w_in_t, l, ropes,
                                                                cw, cb, scw, s)
        kc, vct = _compress(cpair.reshape(b, nc, half_tokens * LANES), pos2, wc, w2bd)
        r3 = lambda a: a.reshape(b, s, a.shape[-1])
        ya = _ssd(r3(zs), r3(xc), r3(dtg), dtb, aneg, dsk, ssd_norm_w[l][None, :])
        vt4 = lambda a: a.reshape(b, s // LANES, NSA_HEAD_DIM, LANES)
        yn = _nsa(r3(q), r3(dtg), kc, vct, r3(ks), vt4(vst), r3(kw), vt4(vwt), ovt)
        x2d = _ffn(x2d, ya.reshape(t, -1), ysc, yn.reshape(t, -1), wo, ffn_norm_w[l][None, :], wu_t, cw_t, cb_t, wd,
                   final_norm_w[None, :], s, l == depth - 1)
    return x2d.reshape(b, s, d)
```

```python
import functools
import math

import numpy as np
import jax
import jax.numpy as jnp
from jax import lax
from jax.experimental import pallas as pl
from jax.experimental.pallas import tpu as pltpu

F32 = jnp.float32
BF16 = jnp.bfloat16

D_MODEL = 1024
SSD_HEADS = 8
SSD_HEAD_DIM = 64
SSD_INNER = SSD_HEADS * SSD_HEAD_DIM
SSD_GROUPS = 2
SSD_STATE = 128
SSD_CONV = 4
SSD_CHUNK = 256
SSD_XBC = SSD_INNER + 2 * SSD_GROUPS * SSD_STATE
SC_WIDTH = 256
SC_CONV = 3
NSA_HEADS = 4
NSA_HEAD_DIM = 64
NSA_WIDTH = NSA_HEADS * NSA_HEAD_DIM
CMP_BLOCK = 32
CMP_STRIDE = 16
CMP_HIDDEN = 128
SEL_BLOCK = 64
SEL_TOPK = 16
SEL_LOCAL = 2
WINDOW = 512
ROPE_THETA = 10000.0
D_FF = 2752
FFN_CONV = 3
EPS = 1e-6
NEG = -1e30
FORCE = 1e9
REMOVED = -3.0e38
LOG2E = 1.4426950408889634

LANES = 128
SUBLANES = 8
VMEM_LIMIT_BYTES = 56 * 1024 * 1024

D_FF_PAD = 2816
FF_TILE = 256
N_FF_TILES = D_FF_PAD // FF_TILE

COL_Z = 0
COL_XBC = COL_Z + SSD_INNER
COL_SC = COL_XBC + SSD_XBC
COL_Q = COL_SC + 3 * SC_WIDTH
COL_KV = COL_Q + NSA_WIDTH
COL_DTG = COL_KV + 6 * NSA_HEAD_DIM
N_PACK = COL_DTG + LANES
GATE_COL = SSD_HEADS
IN_DT0 = SSD_INNER + SSD_XBC
IN_DT1 = IN_DT0 + SSD_HEADS
IN_G0 = IN_DT1 + 3 * SC_WIDTH + NSA_WIDTH + 6 * NSA_HEAD_DIM
IN_G1 = IN_G0 + 3 * NSA_HEADS

INPROJ_TM = 512
NSA_TQ = 512
NSA_TK = 512
NSA_WKEYS = WINDOW + NSA_TQ
FFN_TM = 256


def _dot(a, b):
    return jnp.dot(a, b, preferred_element_type=F32)


def _dot_nt(a, b):
    return lax.dot_general(a, b, (((1,), (1,)), ((), ())), preferred_element_type=F32)


def _sigmoid(x):
    return 1.0 / (1.0 + jnp.exp(-x))


def _silu(x):
    h = 0.5 * x
    return h + h * jnp.tanh(h)


def _softplus(x):
    return jnp.maximum(x, 0.0) + jnp.log1p(jnp.exp(-jnp.abs(x)))


def _split3(a):
    a1 = a.astype(BF16)
    r1 = a - a1.astype(F32)
    a2 = r1.astype(BF16)
    r2 = r1 - a2.astype(F32)
    return a1, a2, r2.astype(BF16)


def _shift_rows(cur, tail, k):
    if k == 0:
        return cur
    rc = pltpu.roll(cur, k, 0)
    rt = pltpu.roll(tail, k, 0)
    row = lax.broadcasted_iota(jnp.int32, tail.shape, 0)
    first = jnp.where(row < k, rt, rc[0:SUBLANES])
    return jnp.concatenate([first, rc[SUBLANES:]], axis=0)


def _inproj_kernel(x_ref, nw_ref, w_ref, rqc_ref, rqs_ref, rkc_ref, rks_ref,
                   z_ref, xbc_ref, sc_ref, q_ref, cpair_ref, ks_ref, kw_ref, vst_ref, vwt_ref, dtg_ref, wb_ref):
    tm = x_ref.shape[0]

    @pl.when(pl.program_id(0) == 0)
    def _():
        def cast_rows(dst, src, n):
            for r in range(0, n, LANES):
                m = min(LANES, n - r)
                wb_ref[dst + r:dst + r + m, :] = w_ref[0, src + r:src + r + m, :].astype(BF16)

        cast_rows(0, 0, IN_DT0)
        cast_rows(IN_DT0, IN_DT1, IN_G0 - IN_DT1)
        tail = jnp.concatenate([w_ref[0, IN_DT0:IN_DT1, :], w_ref[0, IN_G0:IN_G1, :],
                                jnp.zeros((LANES - (IN_DT1 - IN_DT0) - (IN_G1 - IN_G0), D_MODEL), F32)], axis=0)
        wb_ref[COL_DTG:N_PACK, :] = tail.astype(BF16)

    x = x_ref[...]
    ms = jnp.mean(x * x, axis=-1, keepdims=True)
    h = (x * lax.rsqrt(ms + EPS) * nw_ref[...]).astype(BF16)

    def proj(a, b):
        return _dot_nt(h, wb_ref[a:b, :])

    z_ref[...] = proj(COL_Z, COL_XBC)
    xbc_ref[...] = proj(COL_XBC, COL_SC)
    sc_ref[...] = proj(COL_SC, COL_Q)
    dtg_ref[...] = proj(COL_DTG, N_PACK)

    lane = lax.broadcasted_iota(jnp.int32, (tm, LANES), 1)
    first_half = (lane % NSA_HEAD_DIM) < (NSA_HEAD_DIM // 2)

    def rope(v, c, s):
        partner = jnp.where(first_half, pltpu.roll(v, LANES - 32, 1), pltpu.roll(v, 32, 1))
        return v * c + partner * s

    rqc, rqs, rkc, rks = rqc_ref[...], rqs_ref[...], rkc_ref[...], rks_ref[...]
    q = proj(COL_Q, COL_KV)
    q_ref[:, 0:LANES] = rope(q[:, 0:LANES], rqc, rqs)
    q_ref[:, LANES:2 * LANES] = rope(q[:, LANES:2 * LANES], rqc, rqs)

    kv = proj(COL_KV, COL_DTG)
    cpair_ref[...] = rope(kv[:, 0:LANES], rkc, rks)
    row = lax.broadcasted_iota(jnp.int32, (tm, LANES), 0)
    block_in_tile = (row % NSA_TK) // SEL_BLOCK
    onehot = jnp.where(lane - NSA_HEAD_DIM == block_in_tile, 1.0, 0.0)
    k_lanes = lane < NSA_HEAD_DIM
    for pair, k_ref, vt_ref, fill in ((1, ks_ref, vst_ref, onehot), (2, kw_ref, vwt_ref, 0.0)):
        p = rope(kv[:, pair * LANES:(pair + 1) * LANES], rkc, rks)
        k_ref[...] = jnp.where(k_lanes, p, fill).astype(BF16)
        pt = p.T
        for j in range(tm // LANES):
            vt_ref[j] = pt[NSA_HEAD_DIM:LANES, j * LANES:(j + 1) * LANES].astype(BF16)


def _inproj(x2d, nw, w_t, layer, ropes, seq):
    t = x2d.shape[0]
    tm = INPROJ_TM
    nt = t // tm
    pos_blocks = seq // tm
    row = lambda i: (i, 0)
    pos = lambda i: (i % pos_blocks, 0)
    const = lambda i: (0, 0)
    out_shapes = (
        jax.ShapeDtypeStruct((t, SSD_INNER), F32),
        jax.ShapeDtypeStruct((t, SSD_XBC), F32),
        jax.ShapeDtypeStruct((t, 3 * SC_WIDTH), F32),
        jax.ShapeDtypeStruct((t, NSA_WIDTH), F32),
        jax.ShapeDtypeStruct((t, LANES), F32),
        jax.ShapeDtypeStruct((t, LANES), BF16),
        jax.ShapeDtypeStruct((t, LANES), BF16),
        jax.ShapeDtypeStruct((t // LANES, NSA_HEAD_DIM, LANES), BF16),
        jax.ShapeDtypeStruct((t // LANES, NSA_HEAD_DIM, LANES), BF16),
        jax.ShapeDtypeStruct((t, LANES), F32),
    )
    vt_spec = pl.BlockSpec((tm // LANES, NSA_HEAD_DIM, LANES), lambda i: (i, 0, 0))
    out_specs = (
        pl.BlockSpec((tm, SSD_INNER), row),
        pl.BlockSpec((tm, SSD_XBC), row),
        pl.BlockSpec((tm, 3 * SC_WIDTH), row),
        pl.BlockSpec((tm, NSA_WIDTH), row),
        pl.BlockSpec((tm, LANES), row),
        pl.BlockSpec((tm, LANES), row),
        pl.BlockSpec((tm, LANES), row),
        vt_spec,
        vt_spec,
        pl.BlockSpec((tm, LANES), row),
    )
    in_specs = [
        pl.BlockSpec((tm, D_MODEL), row),
        pl.BlockSpec((1, D_MODEL), const),
        pl.BlockSpec((1, IN_G1, D_MODEL), lambda i: (layer, 0, 0), pipeline_mode=pl.Buffered(1)),
        pl.BlockSpec((tm, LANES), pos),
        pl.BlockSpec((tm, LANES), pos),
        pl.BlockSpec((tm, LANES), pos),
        pl.BlockSpec((tm, LANES), pos),
    ]
    return pl.pallas_call(
        _inproj_kernel,
        grid=(nt,),
        in_specs=in_specs,
        out_specs=out_specs,
        out_shape=out_shapes,
        scratch_shapes=[pltpu.VMEM((N_PACK, D_MODEL), BF16)],
        compiler_params=pltpu.CompilerParams(
            dimension_semantics=("arbitrary",), vmem_limit_bytes=VMEM_LIMIT_BYTES),
        name="inproj",
    )(x2d, nw, w_t, *ropes)


def _compress_kernel(x2_ref, pos_ref, wc_ref, w2_ref, kc_ref, vct_ref):
    nc = x2_ref.shape[1]
    x2 = x2_ref[0].astype(BF16)
    wc = wc_ref[...]
    y = _dot(x2, wc)
    r = _dot(pos_ref[...].astype(BF16), wc)
    h = CMP_HIDDEN

    def pre(base):
        bias = r[0:1, base:base + h] + r[1:2, base + h:base + 2 * h]
        return y[:, base:base + h] + pltpu.roll(y[:, base + h:base + 2 * h], nc - 1, 0) + bias

    hid = jnp.concatenate([jax.nn.gelu(pre(0)), jax.nn.gelu(pre(2 * h))], axis=1).astype(BF16)
    o = _dot(hid, w2_ref[...])
    kc_ref[0] = o.astype(BF16)
    vct_ref[0] = o.T[NSA_HEAD_DIM:LANES, :].astype(BF16)


def _compress(x2p, pos2, wc, w2bd):
    b, nc, width = x2p.shape
    return pl.pallas_call(
        _compress_kernel,
        grid=(b,),
        in_specs=[
            pl.BlockSpec((1, nc, width), lambda i: (i, 0, 0)),
            pl.BlockSpec(pos2.shape, lambda i: (0, 0)),
            pl.BlockSpec(wc.shape, lambda i: (0, 0)),
            pl.BlockSpec(w2bd.shape, lambda i: (0, 0)),
        ],
        out_specs=(
            pl.BlockSpec((1, nc, LANES), lambda i: (i, 0, 0)),
            pl.BlockSpec((1, NSA_HEAD_DIM, nc), lambda i: (i, 0, 0)),
        ),
        out_shape=(
            jax.ShapeDtypeStruct((b, nc, LANES), BF16),
            jax.ShapeDtypeStruct((b, NSA_HEAD_DIM, nc), BF16),
        ),
        compiler_params=pltpu.CompilerParams(
            dimension_semantics=("arbitrary",), vmem_limit_bytes=VMEM_LIMIT_BYTES),
        name="compress",
    )(x2p, pos2, wc, w2bd)


def _ssd_kernel(z_ref, xbc_ref, sc_ref, dtg_ref, cw_ref, cb_ref, dtb_ref, aneg_ref, dsk_ref, nw_ref, scw_ref,
                y_ref, h_ref, xtail_ref, stail_ref):
    L = SSD_CHUNK
    half = SSD_HEAD_DIM

    @pl.when(pl.program_id(1) == 0)
    def _():
        h_ref[...] = jnp.zeros_like(h_ref)
        xtail_ref[...] = jnp.zeros_like(xtail_ref)
        stail_ref[...] = jnp.zeros_like(stail_ref)

    xraw = xbc_ref[0]
    xtail = xtail_ref[...]
    conv = cb_ref[...] + cw_ref[SSD_CONV - 1:SSD_CONV, :] * xraw
    for j in range(SSD_CONV - 1):
        conv = conv + cw_ref[j:j + 1, :] * _shift_rows(xraw, xtail, SSD_CONV - 1 - j)
    xtail_ref[...] = xraw[L - SUBLANES:L]
    xc = _silu(conv)

    dt = _softplus(dtg_ref[0] + dtb_ref[...])
    a = dt * aneg_ref[...]
    row = lax.broadcasted_iota(jnp.int32, (L, L), 0)
    col = lax.broadcasted_iota(jnp.int32, (L, L), 1)
    causal = row >= col
    tri = jnp.where(causal, 1.0, 0.0).astype(BF16)
    a1, a2, a3 = _split3(a)
    acs = _dot(tri, a1) + _dot(tri, a2) + _dot(tri, a3)
    acs_t = acs.T
    exp_acs = jnp.exp2(acs)
    last = acs[L - 1:L, :]
    dte = jnp.exp2(last - acs)
    chunk_decay = jnp.exp2(last)

    lo = lax.broadcasted_iota(jnp.int32, (L, LANES), 1) < half
    lo_row = lax.broadcasted_iota(jnp.int32, (1, LANES), 1) < half

    def per_lane(m, ha, hb, mask):
        return jnp.where(mask, m[:, ha:ha + 1], m[:, hb:hb + 1])

    ys = []
    for g in range(SSD_GROUPS):
        b_g = xc[:, SSD_INNER + g * SSD_STATE:SSD_INNER + (g + 1) * SSD_STATE]
        c_g = xc[:, SSD_INNER + (SSD_GROUPS + g) * SSD_STATE:SSD_INNER + (SSD_GROUPS + g + 1) * SSD_STATE]
        b_bf = b_g.astype(BF16)
        c_bf = c_g.astype(BF16)
        cb = _dot_nt(c_bf, b_bf)
        bt_bf = b_g.T.astype(BF16)
        for pp in range(SSD_HEADS // SSD_GROUPS // 2):
            p = g * (SSD_HEADS // SSD_GROUPS // 2) + pp
            ha, hb = 2 * p, 2 * p + 1
            x_pair = xc[:, p * LANES:(p + 1) * LANES]
            xdt = x_pair * per_lane(dt, ha, hb, lo)
            xdt_bf = xdt.astype(BF16)
            yd = []
            for hd in (ha, hb):
                seg = acs[:, hd:hd + 1] - acs_t[hd:hd + 1, :]
                decay = jnp.exp2(jnp.where(causal, seg, NEG))
                yd.append(_dot((decay * cb).astype(BF16), xdt_bf))
            y_diag = jnp.where(lo, yd[0], yd[1])
            st = _dot(bt_bf, (xdt * per_lane(dte, ha, hb, lo)).astype(BF16))
            h_prev = h_ref[p]
            y_off = _dot(c_bf, h_prev.astype(BF16)) * per_lane(exp_acs, ha, hb, lo)
            h_ref[p] = h_prev * per_lane(chunk_decay, ha, hb, lo_row) + st
            ys.append(y_diag + y_off + x_pair * dsk_ref[:, p * LANES:(p + 1) * LANES])
    y = jnp.concatenate(ys, axis=1)
    z = z_ref[0]
    y = y * _silu(z)
    ms = jnp.mean(y * y, axis=-1, keepdims=True)
    y_ref[0, :, 0:SSD_INNER] = y * lax.rsqrt(ms + EPS) * nw_ref[...]

    sc = sc_ref[0]
    u = sc[:, SC_WIDTH:2 * SC_WIDTH] * sc[:, 2 * SC_WIDTH:3 * SC_WIDTH]
    stail = stail_ref[...]
    cv = scw_ref[SC_CONV - 1:SC_CONV, :] * u
    for j in range(SC_CONV - 1):
        cv = cv + scw_ref[j:j + 1, :] * _shift_rows(u, stail, SC_CONV - 1 - j)
    stail_ref[...] = u[L - SUBLANES:L]
    y_ref[0, :, SSD_INNER:SSD_INNER + SC_WIDTH] = sc[:, 0:SC_WIDTH] * cv


def _ssd(z, xbc, sc, dtg, cw, cb, dtb, aneg, dsk, nw, scw):
    b, s, _ = z.shape
    L = SSD_CHUNK
    blk = lambda w: pl.BlockSpec((1, L, w), lambda i, c: (i, c, 0))
    par = lambda a: pl.BlockSpec(a.shape, lambda i, c: (0, 0))
    return pl.pallas_call(
        _ssd_kernel,
        grid=(b, s // L),
        in_specs=[blk(SSD_INNER), blk(SSD_XBC), blk(3 * SC_WIDTH), blk(LANES),
                  par(cw), par(cb), par(dtb), par(aneg), par(dsk), par(nw), par(scw)],
        out_specs=blk(SSD_INNER + SC_WIDTH),
        out_shape=jax.ShapeDtypeStruct((b, s, SSD_INNER + SC_WIDTH), F32),
        scratch_shapes=[
            pltpu.VMEM((SSD_HEADS // 2, SSD_STATE, LANES), F32),
            pltpu.VMEM((SUBLANES, SSD_XBC), F32),
            pltpu.VMEM((SUBLANES, SC_WIDTH), F32),
        ],
        compiler_params=pltpu.CompilerParams(
            dimension_semantics=("arbitrary", "arbitrary"), vmem_limit_bytes=VMEM_LIMIT_BYTES),
        name="ssd_sc",
    )(z, xbc, sc, dtg, cw, cb, dtb, aneg, dsk, nw, scw)


def _nsa_kernel(q_ref, dtg_ref, kc_ref, vct_ref, ks_ref, vst_ref, kw_ref, vwt_ref, ovt_ref,
                o_ref, sel_ref, qa_ref, acc_ref, m_ref, accw_ref, mw_ref, s_ref, *, seq):
    tq, tk = NSA_TQ, NSA_TK
    hd = NSA_HEAD_DIM
    nh = NSA_HEADS
    nc = seq // CMP_STRIDE
    nb = seq // SEL_BLOCK
    q0 = pl.program_id(1) * tq
    tpos = q0 + lax.broadcasted_iota(jnp.int32, (1, tq), 1)

    def heads_on_lanes(a):
        return jnp.concatenate([a] * nh, axis=1)

    qt = q_ref[0].T
    qa_ref[0:hd, :] = jnp.concatenate([qt[h * hd:(h + 1) * hd, :] for h in range(nh)], axis=1).astype(BF16)
    qa_ref[hd:LANES, :] = jnp.zeros((LANES - hd, nh * tq), BF16)
    gates = _sigmoid(dtg_ref[0]).T

    n_io = lax.broadcasted_iota(jnp.int32, (nc, tq), 0)
    cbias = jnp.where((n_io * CMP_STRIDE + (CMP_BLOCK - 1)) <= tpos, 0.0, NEG)
    s = _dot(kc_ref[0], qa_ref[...]) + heads_on_lanes(cbias)
    p = jnp.exp2(s - jnp.max(s, axis=0, keepdims=True))
    has_key = heads_on_lanes(jnp.where(tpos >= CMP_BLOCK - 1, 1.0, 0.0))
    p = p * (has_key / jnp.sum(p, axis=0, keepdims=True))
    o_cmp = _dot(vct_ref[0], p.astype(BF16))
    psum = p[:, 0:tq]
    for h in range(1, nh):
        psum = psum + p[:, h * tq:(h + 1) * tq]

    def load_vt(ref, key0, width):
        first = key0 // LANES
        return jnp.concatenate([ref[0, first + i] for i in range(width // LANES)], axis=1)

    ovt = ovt_ref[...]
    p1, p2, p3 = _split3(psum)
    imp = _dot(ovt, p1) + _dot(ovt, p2) + _dot(ovt, p3)
    j_io = lax.broadcasted_iota(jnp.int32, (nb, tq), 0)
    cur = jnp.right_shift(tpos, int(math.log2(SEL_BLOCK)))
    valid = j_io <= cur
    forced = (j_io == 0) | (valid & (j_io > cur - SEL_LOCAL))
    v = jnp.where(forced, REMOVED, jnp.where(valid, imp, NEG))
    j_f = j_io.astype(F32)

    def extract(_, carry):
        v, sel = carry
        m = jnp.max(v, axis=0, keepdims=True)
        first = jnp.min(jnp.where(v == m, j_f, float(nb)), axis=0, keepdims=True)
        hit = j_f == first
        return jnp.where(hit, REMOVED, v), jnp.where(hit, 1.0, sel)

    _, sel = lax.fori_loop(0, SEL_TOPK - 1 - SEL_LOCAL, extract, (v, jnp.where(forced, 1.0, 0.0)), unroll=True)
    sel_ref[...] = sel

    wt = NSA_WKEYS // 2
    w0 = pl.multiple_of(jnp.maximum(q0 - WINDOW, 0), tq)
    blocks_per_tile = tk // SEL_BLOCK
    bias_rows = 2 * SUBLANES
    for ref in (acc_ref, accw_ref):
        ref[...] = jnp.zeros_like(ref)
    for ref in (m_ref, mw_ref):
        ref[...] = jnp.full(ref.shape, NEG, F32)

    def scores_of(buf, rows, k_tile):
        def one_head(h):
            lanes = slice(h * tq, (h + 1) * tq)
            s_ref[buf, 0:rows, lanes] = _dot(k_tile, qa_ref[:, lanes])
        return one_head

    def update_of(buf, rows, vt, bias, m_r, acc_r):
        ones = jnp.where(lax.broadcasted_iota(jnp.int32, (bias_rows, rows), 0) == 0, 1.0, 0.0).astype(BF16)
        vt_aug = jnp.concatenate([vt, ones], axis=0)

        def one_head(h):
            lanes = slice(h * tq, (h + 1) * tq)
            s = s_ref[buf, 0:rows, lanes]
            if bias is not None:
                s = s + bias
            m_old = m_r[:, lanes]
            m_new = jnp.maximum(m_old, jnp.max(s, axis=0, keepdims=True))
            p = jnp.exp2(s - m_new).astype(BF16)
            acc_r[:, lanes] = jnp.exp2(m_old - m_new) * acc_r[:, lanes] + _dot(vt_aug, p)
            m_r[:, lanes] = m_new
        return one_head

    def stage(score_fn, update_fn):
        for h in range(nh):
            if score_fn is not None:
                score_fn(h)
            if update_fn is not None:
                update_fn(h)

    def win_scores(buf, i):
        k0 = pl.multiple_of(w0 + i * wt, LANES)
        return scores_of(buf, wt, kw_ref[0, pl.ds(k0, wt), :])

    def win_update(buf, i):
        k0 = pl.multiple_of(w0 + i * wt, LANES)
        d = (k0 - q0) + (lax.broadcasted_iota(jnp.int32, (wt, tq), 0)
                         - lax.broadcasted_iota(jnp.int32, (wt, tq), 1))
        bias = jnp.where(d <= 0, jnp.where(d > -WINDOW, 0.0, NEG), NEG)
        return update_of(buf, wt, load_vt(vwt_ref, k0, wt), bias, mw_ref, accw_ref)

    def sel_scores(buf, kt):
        k0 = pl.multiple_of(kt * tk, tk)
        chunk = sel_ref[pl.ds(pl.multiple_of(kt * blocks_per_tile, blocks_per_tile), blocks_per_tile), :]
        bias = jnp.concatenate([(1.0 - chunk) * NEG, jnp.zeros((bias_rows - blocks_per_tile, tq), F32)], axis=0)
        qa_ref[hd:hd + bias_rows, :] = heads_on_lanes(bias).astype(BF16)
        return scores_of(buf, tk, ks_ref[0, pl.ds(k0, tk), :])

    def sel_update(buf, kt, causal):
        k0 = pl.multiple_of(kt * tk, tk)
        bias = None
        if causal:
            bias = jnp.where(k0 + lax.broadcasted_iota(jnp.int32, (tk, tq), 0) <= tpos, 0.0, NEG)
        return update_of(buf, tk, load_vt(vst_ref, k0, tk), bias, m_ref, acc_ref)

    stage(win_scores(0, 0), None)
    stage(win_scores(1, 1), win_update(0, 0))
    stage(sel_scores(0, 0), win_update(1, 1))

    n_last = (q0 + tq + tk - 1) // tk - 1

    def trip(first, tiles):
        for i in range(tiles):
            stage(sel_scores((i + 1) % 2, first + i + 1), sel_update(i % 2, first + i, False))

    def quad(j, _):
        trip(4 * j, 4)
        return 0

    lax.fori_loop(0, n_last // 4, quad, 0)
    done4 = (n_last // 4) * 4

    def pair(j, _):
        trip(done4 + 2 * j, 2)
        return 0

    lax.fori_loop(0, (n_last - done4) // 2, pair, 0)

    @pl.when(n_last % 2 == 0)
    def _():
        stage(None, sel_update(0, n_last, True))

    @pl.when(n_last % 2 == 1)
    def _():
        stage(sel_scores(1, n_last), sel_update(0, n_last - 1, False))
        stage(None, sel_update(1, n_last, True))

    o_sel = acc_ref[0:hd, :] / acc_ref[hd:hd + 1, :]
    o_win = accw_ref[0:hd, :] / accw_ref[hd:hd + 1, :]

    def gate_row(branch):
        return jnp.concatenate([gates[GATE_COL + 3 * h + branch:GATE_COL + 3 * h + branch + 1, :]
                                for h in range(nh)], axis=1)

    out = o_cmp * gate_row(0) + o_sel * gate_row(1) + o_win * gate_row(2)
    o_ref[0] = jnp.concatenate([out[:, h * tq:(h + 1) * tq] for h in range(nh)], axis=0).T


def _nsa(q, dtg, kc, vct, ks, vst, kw, vwt, ovt):
    b, s, _ = q.shape
    tq = NSA_TQ
    nc = s // CMP_STRIDE
    per_b3 = lambda shape: pl.BlockSpec((1,) + shape, lambda i, j: (i, 0, 0))
    per_b4 = lambda shape: pl.BlockSpec((1,) + shape, lambda i, j: (i, 0, 0, 0))
    qblk = lambda w: pl.BlockSpec((1, tq, w), lambda i, j: (i, j, 0))
    return pl.pallas_call(
        functools.partial(_nsa_kernel, seq=s),
        grid=(b, s // tq),
        in_specs=[
            qblk(NSA_WIDTH), qblk(LANES),
            per_b3((nc, LANES)), per_b3((NSA_HEAD_DIM, nc)),
            per_b3((s, LANES)), per_b4((s // LANES, NSA_HEAD_DIM, LANES)),
            per_b3((s, LANES)), per_b4((s // LANES, NSA_HEAD_DIM, LANES)),
            pl.BlockSpec(ovt.shape, lambda i, j: (0, 0)),
        ],
        out_specs=qblk(NSA_WIDTH),
        out_shape=jax.ShapeDtypeStruct((b, s, NSA_WIDTH), F32),
        scratch_shapes=[
            pltpu.VMEM((s // SEL_BLOCK, tq), F32),
            pltpu.VMEM((LANES, NSA_HEADS * tq), BF16),
            pltpu.VMEM((NSA_HEAD_DIM + 2 * SUBLANES, NSA_HEADS * tq), F32),
            pltpu.VMEM((1, NSA_HEADS * tq), F32),
            pltpu.VMEM((NSA_HEAD_DIM + 2 * SUBLANES, NSA_HEADS * tq), F32),
            pltpu.VMEM((1, NSA_HEADS * tq), F32),
            pltpu.VMEM((2, NSA_TK, NSA_HEADS * tq), F32),
        ],
        compiler_params=pltpu.CompilerParams(
            dimension_semantics=("arbitrary", "arbitrary"), vmem_limit_bytes=VMEM_LIMIT_BYTES),
        name="nsa",
    )(q, dtg, kc, vct, ks, vst, kw, vwt, ovt)


def _ffn_kernel(x_ref, ya_ref, yn_ref, xn_ref, yan_ref, ynn_ref, wo_ref, nw_ref, wu_ref, cw_ref, cb_ref, wd_ref,
                fw_ref, o_ref, tail_ref, act_ref, x1_ref, h_ref, *, tiles_per_seq, final_norm):
    tm = x_ref.shape[0]
    na = ya_ref.shape[1]

    def mixed_residual(xr, yar, ynr):
        x1 = (xr[...] + _dot(yar[...].astype(BF16), wo_ref[0:na, :])
              + _dot(ynr[...].astype(BF16), wo_ref[na:, :]))
        ms = jnp.mean(x1 * x1, axis=-1, keepdims=True)
        return x1, (x1 * lax.rsqrt(ms + EPS) * nw_ref[...]).astype(BF16)

    @pl.when(pl.program_id(0) % tiles_per_seq == 0)
    def _():
        tail_ref[...] = jnp.zeros_like(tail_ref)

    @pl.when(pl.program_id(0) == 0)
    def _():
        x1_0, h_0 = mixed_residual(x_ref, ya_ref, yn_ref)
        x1_ref[...] = x1_0
        h_ref[...] = h_0

    h = h_ref[...]

    def conv_cols(c0):
        cols = slice(c0, c0 + FF_TILE)
        u = _dot(h, wu_ref[:, cols])
        tail = tail_ref[:, cols]
        cv = cb_ref[:, cols] + cw_ref[FFN_CONV - 1:FFN_CONV, cols] * u
        for t in range(FFN_CONV - 1):
            cv = cv + cw_ref[t:t + 1, cols] * _shift_rows(u, tail, FFN_CONV - 1 - t)
        tail_ref[:, cols] = u[tm - SUBLANES:tm]
        return cv

    for j in range(N_FF_TILES):
        gate = conv_cols(j * FF_TILE)
        val = conv_cols(D_FF_PAD + j * FF_TILE)
        act_ref[:, j * FF_TILE:(j + 1) * FF_TILE] = (_silu(gate) * val).astype(BF16)

    x1_next, h_next = mixed_residual(xn_ref, yan_ref, ynn_ref)
    x2 = x1_ref[...] + _dot(act_ref[...], wd_ref[...])
    if final_norm:
        ms2 = jnp.mean(x2 * x2, axis=-1, keepdims=True)
        x2 = x2 * lax.rsqrt(ms2 + EPS) * fw_ref[...]
    o_ref[...] = x2
    x1_ref[...] = x1_next
    h_ref[...] = h_next


def _ffn(x2d, ya, yn, wo, nw, wu_t, cw_t, cb_t, wd, fw, seq, final_norm):
    t = x2d.shape[0]
    tm = FFN_TM
    last = t // tm - 1
    row = lambda i: (i, 0)
    nxt = lambda i: (jnp.minimum(i + 1, last), 0)
    c2 = lambda a: pl.BlockSpec(a.shape, lambda i: (0, 0))
    return pl.pallas_call(
        functools.partial(_ffn_kernel, tiles_per_seq=seq // tm, final_norm=final_norm),
        grid=(t // tm,),
        in_specs=[
            pl.BlockSpec((tm, D_MODEL), row),
            pl.BlockSpec((tm, ya.shape[1]), row),
            pl.BlockSpec((tm, yn.shape[1]), row),
            pl.BlockSpec((tm, D_MODEL), nxt),
            pl.BlockSpec((tm, ya.shape[1]), nxt),
            pl.BlockSpec((tm, yn.shape[1]), nxt),
            c2(wo), c2(nw), c2(wu_t), c2(cw_t), c2(cb_t), c2(wd), c2(fw),
        ],
        out_specs=pl.BlockSpec((tm, D_MODEL), row),
        out_shape=jax.ShapeDtypeStruct((t, D_MODEL), F32),
        scratch_shapes=[
            pltpu.VMEM((SUBLANES, 2 * D_FF_PAD), F32),
            pltpu.VMEM((tm, D_FF_PAD), BF16),
            pltpu.VMEM((tm, D_MODEL), F32),
            pltpu.VMEM((tm, D_MODEL), BF16),
        ],
        compiler_params=pltpu.CompilerParams(
            dimension_semantics=("arbitrary",), vmem_limit_bytes=VMEM_LIMIT_BYTES),
        name="outproj_ffn",
    )(x2d, ya, yn, x2d, ya, yn, wo, nw, wu_t, cw_t, cb_t, wd, fw)


def _rope_tables(seq):
    half = NSA_HEAD_DIM // 2
    inv = 1.0 / (ROPE_THETA ** (jnp.arange(half, dtype=F32) / half))
    ang = jnp.arange(seq, dtype=F32)[:, None] * inv[None, :]
    cos, sin = jnp.cos(ang), jnp.sin(ang)
    scale = NSA_HEAD_DIM ** -0.5 * LOG2E
    one = jnp.ones_like(cos)
    zero = jnp.zeros_like(cos)
    rqc = jnp.concatenate([cos, cos, cos, cos], axis=1) * scale
    rqs = jnp.concatenate([-sin, sin, -sin, sin], axis=1) * scale
    rkc = jnp.concatenate([cos, cos, one, one], axis=1)
    rks = jnp.concatenate([-sin, sin, zero, zero], axis=1)
    return rqc, rqs, rkc, rks


def _compress_weights(kw1, kw2, vw1, vw2):
    half_tokens = CMP_BLOCK // 2
    hd, hid = NSA_HEAD_DIM, CMP_HIDDEN
    kw1r = kw1.reshape(2, half_tokens, hd, hid)
    vw1r = vw1.reshape(2, half_tokens, hd, hid)
    zeros = jnp.zeros((half_tokens, hd, hid), kw1.dtype)
    cols = []
    for w1r, is_k in ((kw1r, True), (vw1r, False)):
        for part in range(2):
            blk = w1r[part]
            rows = jnp.concatenate([blk, zeros] if is_k else [zeros, blk], axis=1)
            cols.append(rows.reshape(half_tokens * LANES, hid))
    wc = jnp.concatenate(cols, axis=1).astype(BF16)
    zk = jnp.zeros((hid, hd), kw2.dtype)
    w2bd = jnp.concatenate([jnp.concatenate([kw2, zk], axis=1),
                            jnp.concatenate([zk, vw2], axis=1)], axis=0).astype(BF16)
    return wc, w2bd


def _overlap_t(seq):
    nc = seq // CMP_STRIDE
    nb = seq // SEL_BLOCK
    cmp_start = np.arange(nc) * CMP_STRIDE
    slc_start = np.arange(nb) * SEL_BLOCK
    ov = ((cmp_start[None, :] < slc_start[:, None] + SEL_BLOCK)
          & (cmp_start[None, :] + CMP_BLOCK > slc_start[:, None])
          & (np.arange(nc)[None, :] < nc - 1))
    return jnp.asarray(ov.astype(np.float32), dtype=BF16)


def _pad_rows(a, rows):
    return jnp.concatenate([a, jnp.zeros((rows - a.shape[0],) + a.shape[1:], a.dtype)], axis=0)


def _pad_cols(a, cols):
    return jnp.concatenate([a, jnp.zeros(a.shape[:-1] + (cols - a.shape[-1],), a.dtype)], axis=-1)


def _ff_tiles(a):
    return jnp.concatenate([_pad_cols(a[:, :D_FF], D_FF_PAD), _pad_cols(a[:, D_FF:], D_FF_PAD)], axis=1)


def kernel(x, attn_norm_w, w_in, ssd_conv_w, ssd_conv_b, ssd_dt_bias, ssd_a_log, ssd_d, ssd_norm_w, sc_conv_w,
           cmp_k_pos, cmp_k_w1, cmp_k_w2, cmp_v_pos, cmp_v_w1, cmp_v_w2, w_out, ffn_norm_w, ffn_w_up,
           ffn_conv_w, ffn_conv_b, ffn_w_down, final_norm_w):
    b, s, d = x.shape
    depth = w_in.shape[0]
    assert d == D_MODEL and s % NSA_TK == 0 and s // SEL_BLOCK >= SEL_TOPK and s % INPROJ_TM == 0
    assert INPROJ_TM % NSA_TK == 0 and s >= NSA_WKEYS and NSA_TK // SEL_BLOCK <= SUBLANES
    t = b * s
    ropes = _rope_tables(s)
    ovt = _overlap_t(s)
    nc = s // CMP_STRIDE
    half_tokens = CMP_BLOCK // 2
    x2d = x.reshape(t, d)
    w_in_t = jnp.transpose(w_in, (0, 2, 1))
    for l in range(depth):
        cw = _pad_rows(ssd_conv_w[l], SUBLANES)
        cb = ssd_conv_b[l][None, :]
        dtb = _pad_cols(ssd_dt_bias[l][None, :], LANES)
        aneg = _pad_cols(-jnp.exp(ssd_a_log[l].astype(F32))[None, :] * LOG2E, LANES)
        dsk = jnp.repeat(ssd_d[l].astype(F32), SSD_HEAD_DIM)[None, :]
        scw = _pad_rows(sc_conv_w[l], SUBLANES)
        wc, w2bd = _compress_weights(cmp_k_w1[l], cmp_k_w2[l], cmp_v_w1[l], cmp_v_w2[l])
        pos2 = _pad_rows(jnp.concatenate([cmp_k_pos[l], cmp_v_pos[l]], axis=1).reshape(2, half_tokens * LANES),
                         SUBLANES)
        wo = w_out[l].astype(BF16)
        wu_t = _ff_tiles(ffn_w_up[l]).astype(BF16)
        cw_t = _ff_tiles(_pad_rows(ffn_conv_w[l], SUBLANES))
        cb_t = _ff_tiles(ffn_conv_b[l][None, :])
        wd = _pad_rows(ffn_w_down[l], D_FF_PAD).astype(BF16)

        z, xbc, sc, q, cpair, ks, kw, vst, vwt, dtg = _inproj(x2d, attn_norm_w[l][None, :], w_in_t, l, ropes, s)
        kc, vct = _compress(cpair.reshape(b, nc, half_tokens * LANES), pos2, wc, w2bd)
        r3 = lambda a: a.reshape(b, s, a.shape[-1])
        ya = _ssd(r3(z), r3(xbc), r3(sc), r3(dtg), cw, cb, dtb, aneg, dsk, ssd_norm_w[l][None, :], scw)
        vt4 = lambda a: a.reshape(b, s // LANES, NSA_HEAD_DIM, LANES)
        yn = _nsa(r3(q), r3(dtg), kc, vct, r3(ks), vt4(vst), r3(kw), vt4(vwt), ovt)
        x2d = _ffn(x2d, ya.reshape(t, -1), yn.reshape(t, -1), wo, ffn_norm_w[l][None, :], wu_t, cw_t, cb_t, wd,
                   final_norm_w[None, :], s, l == depth - 1)
    return x2d.reshape(b, s, d)
```

```python
import functools
import math

import numpy as np
import jax
import jax.numpy as jnp
from jax import lax
from jax.experimental import pallas as pl
from jax.experimental.pallas import tpu as pltpu

F32 = jnp.float32
BF16 = jnp.bfloat16

D_MODEL = 1024
SSD_HEADS = 8
SSD_HEAD_DIM = 64
SSD_INNER = SSD_HEADS * SSD_HEAD_DIM
SSD_GROUPS = 2
SSD_STATE = 128
SSD_CONV = 4
SSD_CHUNK = 256
SSD_XBC = SSD_INNER + 2 * SSD_GROUPS * SSD_STATE
SC_WIDTH = 256
SC_CONV = 3
NSA_HEADS = 4
NSA_HEAD_DIM = 64
NSA_WIDTH = NSA_HEADS * NSA_HEAD_DIM
CMP_BLOCK = 32
CMP_STRIDE = 16
CMP_HIDDEN = 128
SEL_BLOCK = 64
SEL_TOPK = 16
SEL_LOCAL = 2
WINDOW = 512
ROPE_THETA = 10000.0
D_FF = 2752
FFN_CONV = 3
EPS = 1e-6
NEG = -1e30
FORCE = 1e9
REMOVED = -3.0e38
LOG2E = 1.4426950408889634

LANES = 128
SUBLANES = 8
VMEM_LIMIT_BYTES = 56 * 1024 * 1024

D_FF_PAD = 2816
FF_TILE = 256
N_FF_TILES = D_FF_PAD // FF_TILE

COL_Z = 0
COL_XBC = COL_Z + SSD_INNER
COL_SC = COL_XBC + SSD_XBC
COL_Q = COL_SC + 3 * SC_WIDTH
COL_KV = COL_Q + NSA_WIDTH
COL_DTG = COL_KV + 6 * NSA_HEAD_DIM
N_PACK = COL_DTG + LANES
GATE_COL = SSD_HEADS
IN_DT0 = SSD_INNER + SSD_XBC
IN_DT1 = IN_DT0 + SSD_HEADS
IN_G0 = IN_DT1 + 3 * SC_WIDTH + NSA_WIDTH + 6 * NSA_HEAD_DIM
IN_G1 = IN_G0 + 3 * NSA_HEADS

INPROJ_TM = 512
NSA_TQ = 512
NSA_TK = 512
NSA_WKEYS = WINDOW + NSA_TQ
FFN_TM = 256


def _dot(a, b):
    return jnp.dot(a, b, preferred_element_type=F32)


def _dot_nt(a, b):
    return lax.dot_general(a, b, (((1,), (1,)), ((), ())), preferred_element_type=F32)


def _sigmoid(x):
    return 1.0 / (1.0 + jnp.exp(-x))


def _silu(x):
    h = 0.5 * x
    return h + h * jnp.tanh(h)


def _softplus(x):
    return jnp.maximum(x, 0.0) + jnp.log1p(jnp.exp(-jnp.abs(x)))


def _split3(a):
    a1 = a.astype(BF16)
    r1 = a - a1.astype(F32)
    a2 = r1.astype(BF16)
    r2 = r1 - a2.astype(F32)
    return a1, a2, r2.astype(BF16)


def _shift_rows(cur, tail, k):
    if k == 0:
        return cur
    rc = pltpu.roll(cur, k, 0)
    rt = pltpu.roll(tail, k, 0)
    row = lax.broadcasted_iota(jnp.int32, tail.shape, 0)
    first = jnp.where(row < k, rt, rc[0:SUBLANES])
    return jnp.concatenate([first, rc[SUBLANES:]], axis=0)


def _inproj_kernel(x_ref, nw_ref, w_ref, rqc_ref, rqs_ref, rkc_ref, rks_ref,
                   z_ref, xbc_ref, sc_ref, q_ref, cpair_ref, ks_ref, kw_ref, vst_ref, vwt_ref, dtg_ref,
                   wb_ref, cstage_ref):
    tm = x_ref.shape[0]

    @pl.when(pl.program_id(0) == 0)
    def _():
        def cast_rows(dst, src, n):
            for r in range(0, n, LANES):
                m = min(LANES, n - r)
                wb_ref[dst + r:dst + r + m, :] = w_ref[0, src + r:src + r + m, :].astype(BF16)

        cast_rows(0, 0, IN_DT0)
        cast_rows(IN_DT0, IN_DT1, IN_G0 - IN_DT1)
        tail = jnp.concatenate([w_ref[0, IN_DT0:IN_DT1, :], w_ref[0, IN_G0:IN_G1, :],
                                jnp.zeros((LANES - (IN_DT1 - IN_DT0) - (IN_G1 - IN_G0), D_MODEL), F32)], axis=0)
        wb_ref[COL_DTG:N_PACK, :] = tail.astype(BF16)

    x = x_ref[...]
    ms = jnp.mean(x * x, axis=-1, keepdims=True)
    h = (x * lax.rsqrt(ms + EPS) * nw_ref[...]).astype(BF16)

    def proj(a, b):
        return _dot_nt(h, wb_ref[a:b, :])

    z_ref[...] = proj(COL_Z, COL_XBC)
    xbc_ref[...] = proj(COL_XBC, COL_SC)
    sc_ref[...] = proj(COL_SC, COL_Q)
    dtg_ref[...] = proj(COL_DTG, N_PACK)

    lane = lax.broadcasted_iota(jnp.int32, (tm, LANES), 1)
    first_half = (lane % NSA_HEAD_DIM) < (NSA_HEAD_DIM // 2)

    def rope(v, c, s):
        partner = jnp.where(first_half, pltpu.roll(v, LANES - 32, 1), pltpu.roll(v, 32, 1))
        return v * c + partner * s

    rqc, rqs, rkc, rks = rqc_ref[...], rqs_ref[...], rkc_ref[...], rks_ref[...]
    q = proj(COL_Q, COL_KV)
    q_ref[:, 0:LANES] = rope(q[:, 0:LANES], rqc, rqs)
    q_ref[:, LANES:2 * LANES] = rope(q[:, LANES:2 * LANES], rqc, rqs)

    kv = proj(COL_KV, COL_DTG)
    cstage_ref[...] = rope(kv[:, 0:LANES], rkc, rks)
    group = CMP_BLOCK // 2
    for r in range(group):
        cpair_ref[:, r * LANES:(r + 1) * LANES] = cstage_ref[pl.ds(r, tm // group, stride=group), :]
    row = lax.broadcasted_iota(jnp.int32, (tm, LANES), 0)
    block_in_tile = (row % NSA_TK) // SEL_BLOCK
    onehot = jnp.where(lane - NSA_HEAD_DIM == block_in_tile, 1.0, 0.0)
    k_lanes = lane < NSA_HEAD_DIM
    for pair, k_ref, vt_ref, fill in ((1, ks_ref, vst_ref, onehot), (2, kw_ref, vwt_ref, 0.0)):
        p = rope(kv[:, pair * LANES:(pair + 1) * LANES], rkc, rks)
        k_ref[...] = jnp.where(k_lanes, p, fill).astype(BF16)
        pt = p.T
        for j in range(tm // LANES):
            vt_ref[j] = pt[NSA_HEAD_DIM:LANES, j * LANES:(j + 1) * LANES].astype(BF16)


def _inproj(x2d, nw, w_t, layer, ropes, seq):
    t = x2d.shape[0]
    tm = INPROJ_TM
    nt = t // tm
    pos_blocks = seq // tm
    row = lambda i: (i, 0)
    pos = lambda i: (i % pos_blocks, 0)
    const = lambda i: (0, 0)
    out_shapes = (
        jax.ShapeDtypeStruct((t, SSD_INNER), F32),
        jax.ShapeDtypeStruct((t, SSD_XBC), F32),
        jax.ShapeDtypeStruct((t, 3 * SC_WIDTH), F32),
        jax.ShapeDtypeStruct((t, NSA_WIDTH), F32),
        jax.ShapeDtypeStruct((t // (CMP_BLOCK // 2), (CMP_BLOCK // 2) * LANES), F32),
        jax.ShapeDtypeStruct((t, LANES), BF16),
        jax.ShapeDtypeStruct((t, LANES), BF16),
        jax.ShapeDtypeStruct((t // LANES, NSA_HEAD_DIM, LANES), BF16),
        jax.ShapeDtypeStruct((t // LANES, NSA_HEAD_DIM, LANES), BF16),
        jax.ShapeDtypeStruct((t, LANES), F32),
    )
    vt_spec = pl.BlockSpec((tm // LANES, NSA_HEAD_DIM, LANES), lambda i: (i, 0, 0))
    out_specs = (
        pl.BlockSpec((tm, SSD_INNER), row),
        pl.BlockSpec((tm, SSD_XBC), row),
        pl.BlockSpec((tm, 3 * SC_WIDTH), row),
        pl.BlockSpec((tm, NSA_WIDTH), row),
        pl.BlockSpec((tm // (CMP_BLOCK // 2), (CMP_BLOCK // 2) * LANES), row),
        pl.BlockSpec((tm, LANES), row),
        pl.BlockSpec((tm, LANES), row),
        vt_spec,
        vt_spec,
        pl.BlockSpec((tm, LANES), row),
    )
    in_specs = [
        pl.BlockSpec((tm, D_MODEL), row),
        pl.BlockSpec((1, D_MODEL), const),
        pl.BlockSpec((1, IN_G1, D_MODEL), lambda i: (layer, 0, 0), pipeline_mode=pl.Buffered(1)),
        pl.BlockSpec((tm, LANES), pos),
        pl.BlockSpec((tm, LANES), pos),
        pl.BlockSpec((tm, LANES), pos),
        pl.BlockSpec((tm, LANES), pos),
    ]
    return pl.pallas_call(
        _inproj_kernel,
        grid=(nt,),
        in_specs=in_specs,
        out_specs=out_specs,
        out_shape=out_shapes,
        scratch_shapes=[pltpu.VMEM((N_PACK, D_MODEL), BF16), pltpu.VMEM((tm, LANES), F32)],
        compiler_params=pltpu.CompilerParams(
            dimension_semantics=("arbitrary",), vmem_limit_bytes=VMEM_LIMIT_BYTES),
        name="inproj",
    )(x2d, nw, w_t, *ropes)


def _compress_kernel(x2_ref, pos_ref, wc_ref, w2_ref, kc_ref, vct_ref):
    nc = x2_ref.shape[1]
    x2 = x2_ref[0].astype(BF16)
    wc = wc_ref[...]
    y = _dot(x2, wc)
    r = _dot(pos_ref[...].astype(BF16), wc)
    h = CMP_HIDDEN

    def pre(base):
        bias = r[0:1, base:base + h] + r[1:2, base + h:base + 2 * h]
        return y[:, base:base + h] + pltpu.roll(y[:, base + h:base + 2 * h], nc - 1, 0) + bias

    hid = jnp.concatenate([jax.nn.gelu(pre(0)), jax.nn.gelu(pre(2 * h))], axis=1).astype(BF16)
    o = _dot(hid, w2_ref[...])
    kc_ref[0] = o.astype(BF16)
    vct_ref[0] = o.T[NSA_HEAD_DIM:LANES, :].astype(BF16)


def _compress(x2p, pos2, wc, w2bd):
    b, nc, width = x2p.shape
    return pl.pallas_call(
        _compress_kernel,
        grid=(b,),
        in_specs=[
            pl.BlockSpec((1, nc, width), lambda i: (i, 0, 0)),
            pl.BlockSpec(pos2.shape, lambda i: (0, 0)),
            pl.BlockSpec(wc.shape, lambda i: (0, 0)),
            pl.BlockSpec(w2bd.shape, lambda i: (0, 0)),
        ],
        out_specs=(
            pl.BlockSpec((1, nc, LANES), lambda i: (i, 0, 0)),
            pl.BlockSpec((1, NSA_HEAD_DIM, nc), lambda i: (i, 0, 0)),
        ),
        out_shape=(
            jax.ShapeDtypeStruct((b, nc, LANES), BF16),
            jax.ShapeDtypeStruct((b, NSA_HEAD_DIM, nc), BF16),
        ),
        compiler_params=pltpu.CompilerParams(
            dimension_semantics=("arbitrary",), vmem_limit_bytes=VMEM_LIMIT_BYTES),
        name="compress",
    )(x2p, pos2, wc, w2bd)


def _ssd_kernel(z_ref, xbc_ref, sc_ref, dtg_ref, cw_ref, cb_ref, dtb_ref, aneg_ref, dsk_ref, nw_ref, scw_ref,
                y_ref, h_ref, xtail_ref, stail_ref):
    L = SSD_CHUNK
    half = SSD_HEAD_DIM

    @pl.when(pl.program_id(1) == 0)
    def _():
        h_ref[...] = jnp.zeros_like(h_ref)
        xtail_ref[...] = jnp.zeros_like(xtail_ref)
        stail_ref[...] = jnp.zeros_like(stail_ref)

    xraw = xbc_ref[0]
    xtail = xtail_ref[...]
    conv = cb_ref[...] + cw_ref[SSD_CONV - 1:SSD_CONV, :] * xraw
    for j in range(SSD_CONV - 1):
        conv = conv + cw_ref[j:j + 1, :] * _shift_rows(xraw, xtail, SSD_CONV - 1 - j)
    xtail_ref[...] = xraw[L - SUBLANES:L]
    xc = _silu(conv)

    dt = _softplus(dtg_ref[0] + dtb_ref[...])
    a = dt * aneg_ref[...]
    row = lax.broadcasted_iota(jnp.int32, (L, L), 0)
    col = lax.broadcasted_iota(jnp.int32, (L, L), 1)
    causal = row >= col
    tri = jnp.where(causal, 1.0, 0.0).astype(BF16)
    a1, a2, a3 = _split3(a)
    acs = _dot(tri, a1) + _dot(tri, a2) + _dot(tri, a3)
    acs_t = acs.T
    exp_acs = jnp.exp2(acs)
    last = acs[L - 1:L, :]
    dte = jnp.exp2(last - acs)
    chunk_decay = jnp.exp2(last)

    lo = lax.broadcasted_iota(jnp.int32, (L, LANES), 1) < half
    lo_row = lax.broadcasted_iota(jnp.int32, (1, LANES), 1) < half

    def per_lane(m, ha, hb, mask):
        return jnp.where(mask, m[:, ha:ha + 1], m[:, hb:hb + 1])

    ys = []
    for g in range(SSD_GROUPS):
        b_g = xc[:, SSD_INNER + g * SSD_STATE:SSD_INNER + (g + 1) * SSD_STATE]
        c_g = xc[:, SSD_INNER + (SSD_GROUPS + g) * SSD_STATE:SSD_INNER + (SSD_GROUPS + g + 1) * SSD_STATE]
        b_bf = b_g.astype(BF16)
        c_bf = c_g.astype(BF16)
        cb = _dot_nt(c_bf, b_bf)
        bt_bf = b_g.T.astype(BF16)
        for pp in range(SSD_HEADS // SSD_GROUPS // 2):
            p = g * (SSD_HEADS // SSD_GROUPS // 2) + pp
            ha, hb = 2 * p, 2 * p + 1
            x_pair = xc[:, p * LANES:(p + 1) * LANES]
            xdt = x_pair * per_lane(dt, ha, hb, lo)
            xdt_bf = xdt.astype(BF16)
            yd = []
            for hd in (ha, hb):
                seg = acs[:, hd:hd + 1] - acs_t[hd:hd + 1, :]
                decay = jnp.exp2(jnp.where(causal, seg, NEG))
                yd.append(_dot((decay * cb).astype(BF16), xdt_bf))
            y_diag = jnp.where(lo, yd[0], yd[1])
            st = _dot(bt_bf, (xdt * per_lane(dte, ha, hb, lo)).astype(BF16))
            h_prev = h_ref[p]
            y_off = _dot(c_bf, h_prev.astype(BF16)) * per_lane(exp_acs, ha, hb, lo)
            h_ref[p] = h_prev * per_lane(chunk_decay, ha, hb, lo_row) + st
            ys.append(y_diag + y_off + x_pair * dsk_ref[:, p * LANES:(p + 1) * LANES])
    y = jnp.concatenate(ys, axis=1)
    z = z_ref[0]
    y = y * _silu(z)
    ms = jnp.mean(y * y, axis=-1, keepdims=True)
    y_ref[0, :, 0:SSD_INNER] = y * lax.rsqrt(ms + EPS) * nw_ref[...]

    sc = sc_ref[0]
    u = sc[:, SC_WIDTH:2 * SC_WIDTH] * sc[:, 2 * SC_WIDTH:3 * SC_WIDTH]
    stail = stail_ref[...]
    cv = scw_ref[SC_CONV - 1:SC_CONV, :] * u
    for j in range(SC_CONV - 1):
        cv = cv + scw_ref[j:j + 1, :] * _shift_rows(u, stail, SC_CONV - 1 - j)
    stail_ref[...] = u[L - SUBLANES:L]
    y_ref[0, :, SSD_INNER:SSD_INNER + SC_WIDTH] = sc[:, 0:SC_WIDTH] * cv


def _ssd(z, xbc, sc, dtg, cw, cb, dtb, aneg, dsk, nw, scw):
    b, s, _ = z.shape
    L = SSD_CHUNK
    blk = lambda w: pl.BlockSpec((1, L, w), lambda i, c: (i, c, 0))
    par = lambda a: pl.BlockSpec(a.shape, lambda i, c: (0, 0))
    return pl.pallas_call(
        _ssd_kernel,
        grid=(b, s // L),
        in_specs=[blk(SSD_INNER), blk(SSD_XBC), blk(3 * SC_WIDTH), blk(LANES),
                  par(cw), par(cb), par(dtb), par(aneg), par(dsk), par(nw), par(scw)],
        out_specs=blk(SSD_INNER + SC_WIDTH),
        out_shape=jax.ShapeDtypeStruct((b, s, SSD_INNER + SC_WIDTH), F32),
        scratch_shapes=[
            pltpu.VMEM((SSD_HEADS // 2, SSD_STATE, LANES), F32),
            pltpu.VMEM((SUBLANES, SSD_XBC), F32),
            pltpu.VMEM((SUBLANES, SC_WIDTH), F32),
        ],
        compiler_params=pltpu.CompilerParams(
            dimension_semantics=("arbitrary", "arbitrary"), vmem_limit_bytes=VMEM_LIMIT_BYTES),
        name="ssd_sc",
    )(z, xbc, sc, dtg, cw, cb, dtb, aneg, dsk, nw, scw)


def _nsa_kernel(q_ref, dtg_ref, kc_ref, vct_ref, ks_ref, vst_ref, kw_ref, vwt_ref, ovt_ref,
                o_ref, sel_ref, qa_ref, acc_ref, m_ref, accw_ref, mw_ref, s_ref, *, seq):
    tq, tk = NSA_TQ, NSA_TK
    hd = NSA_HEAD_DIM
    nh = NSA_HEADS
    nc = seq // CMP_STRIDE
    nb = seq // SEL_BLOCK
    q0 = pl.program_id(1) * tq
    tpos = q0 + lax.broadcasted_iota(jnp.int32, (1, tq), 1)

    def heads_on_lanes(a):
        return jnp.concatenate([a] * nh, axis=1)

    qt = q_ref[0].T
    qa_ref[0:hd, :] = jnp.concatenate([qt[h * hd:(h + 1) * hd, :] for h in range(nh)], axis=1).astype(BF16)
    qa_ref[hd:LANES, :] = jnp.zeros((LANES - hd, nh * tq), BF16)
    gates = _sigmoid(dtg_ref[0]).T

    n_io = lax.broadcasted_iota(jnp.int32, (nc, tq), 0)
    cbias = jnp.where((n_io * CMP_STRIDE + (CMP_BLOCK - 1)) <= tpos, 0.0, NEG)
    s = _dot(kc_ref[0], qa_ref[...]) + heads_on_lanes(cbias)
    p = jnp.exp2(s - jnp.max(s, axis=0, keepdims=True))
    has_key = heads_on_lanes(jnp.where(tpos >= CMP_BLOCK - 1, 1.0, 0.0))
    p = p * (has_key / jnp.sum(p, axis=0, keepdims=True))
    o_cmp = _dot(vct_ref[0], p.astype(BF16))
    psum = p[:, 0:tq]
    for h in range(1, nh):
        psum = psum + p[:, h * tq:(h + 1) * tq]

    def load_vt(ref, key0, width):
        first = key0 // LANES
        return jnp.concatenate([ref[0, first + i] for i in range(width // LANES)], axis=1)

    ovt = ovt_ref[...]
    p1, p2, p3 = _split3(psum)
    imp = _dot(ovt, p1) + _dot(ovt, p2) + _dot(ovt, p3)
    j_io = lax.broadcasted_iota(jnp.int32, (nb, tq), 0)
    cur = jnp.right_shift(tpos, int(math.log2(SEL_BLOCK)))
    valid = j_io <= cur
    forced = (j_io == 0) | (valid & (j_io > cur - SEL_LOCAL))
    v = jnp.where(forced, REMOVED, jnp.where(valid, imp, NEG))
    j_f = j_io.astype(F32)

    def extract(_, carry):
        v, sel = carry
        m = jnp.max(v, axis=0, keepdims=True)
        first = jnp.min(jnp.where(v == m, j_f, float(nb)), axis=0, keepdims=True)
        hit = j_f == first
        return jnp.where(hit, REMOVED, v), jnp.where(hit, 1.0, sel)

    _, sel = lax.fori_loop(0, SEL_TOPK - 1 - SEL_LOCAL, extract, (v, jnp.where(forced, 1.0, 0.0)), unroll=True)
    sel_ref[...] = sel

    wt = NSA_WKEYS // 2
    w0 = pl.multiple_of(jnp.maximum(q0 - WINDOW, 0), tq)
    blocks_per_tile = tk // SEL_BLOCK
    bias_rows = 2 * SUBLANES
    for ref in (acc_ref, accw_ref):
        ref[...] = jnp.zeros_like(ref)
    for ref in (m_ref, mw_ref):
        ref[...] = jnp.full(ref.shape, NEG, F32)

    def scores_of(buf, rows, k_tile):
        def one_head(h):
            lanes = slice(h * tq, (h + 1) * tq)
            s_ref[buf, 0:rows, lanes] = _dot(k_tile, qa_ref[:, lanes])
        return one_head

    def update_of(buf, rows, vt, bias, m_r, acc_r):
        ones = jnp.where(lax.broadcasted_iota(jnp.int32, (bias_rows, rows), 0) == 0, 1.0, 0.0).astype(BF16)
        vt_aug = jnp.concatenate([vt, ones], axis=0)

        def one_head(h):
            lanes = slice(h * tq, (h + 1) * tq)
            s = s_ref[buf, 0:rows, lanes]
            if bias is not None:
                s = s + bias
            m_old = m_r[:, lanes]
            m_new = jnp.maximum(m_old, jnp.max(s, axis=0, keepdims=True))
            p = jnp.exp2(s - m_new).astype(BF16)
            acc_r[:, lanes] = jnp.exp2(m_old - m_new) * acc_r[:, lanes] + _dot(vt_aug, p)
            m_r[:, lanes] = m_new
        return one_head

    def stage(score_fn, update_fn):
        for h in range(nh):
            if score_fn is not None:
                score_fn(h)
            if update_fn is not None:
                update_fn(h)

    def win_scores(buf, i):
        k0 = pl.multiple_of(w0 + i * wt, LANES)
        return scores_of(buf, wt, kw_ref[0, pl.ds(k0, wt), :])

    def win_update(buf, i):
        k0 = pl.multiple_of(w0 + i * wt, LANES)
        d = (k0 - q0) + (lax.broadcasted_iota(jnp.int32, (wt, tq), 0)
                         - lax.broadcasted_iota(jnp.int32, (wt, tq), 1))
        bias = jnp.where(d <= 0, jnp.where(d > -WINDOW, 0.0, NEG), NEG)
        return update_of(buf, wt, load_vt(vwt_ref, k0, wt), bias, mw_ref, accw_ref)

    def sel_scores(buf, kt):
        k0 = pl.multiple_of(kt * tk, tk)
        chunk = sel_ref[pl.ds(pl.multiple_of(kt * blocks_per_tile, blocks_per_tile), blocks_per_tile), :]
        bias = jnp.concatenate([(1.0 - chunk) * NEG, jnp.zeros((bias_rows - blocks_per_tile, tq), F32)], axis=0)
        qa_ref[hd:hd + bias_rows, :] = heads_on_lanes(bias).astype(BF16)
        return scores_of(buf, tk, ks_ref[0, pl.ds(k0, tk), :])

    def sel_update(buf, kt, causal):
        k0 = pl.multiple_of(kt * tk, tk)
        bias = None
        if causal:
            bias = jnp.where(k0 + lax.broadcasted_iota(jnp.int32, (tk, tq), 0) <= tpos, 0.0, NEG)
        return update_of(buf, tk, load_vt(vst_ref, k0, tk), bias, m_ref, acc_ref)

    stage(win_scores(0, 0), None)
    stage(win_scores(1, 1), win_update(0, 0))
    stage(sel_scores(0, 0), win_update(1, 1))

    n_last = (q0 + tq + tk - 1) // tk - 1

    def trip(first, tiles):
        for i in range(tiles):
            stage(sel_scores((i + 1) % 2, first + i + 1), sel_update(i % 2, first + i, False))

    def quad(j, _):
        trip(4 * j, 4)
        return 0

    lax.fori_loop(0, n_last // 4, quad, 0)
    done4 = (n_last // 4) * 4

    def pair(j, _):
        trip(done4 + 2 * j, 2)
        return 0

    lax.fori_loop(0, (n_last - done4) // 2, pair, 0)

    @pl.when(n_last % 2 == 0)
    def _():
        stage(None, sel_update(0, n_last, True))

    @pl.when(n_last % 2 == 1)
    def _():
        stage(sel_scores(1, n_last), sel_update(0, n_last - 1, False))
        stage(None, sel_update(1, n_last, True))

    o_sel = acc_ref[0:hd, :] / acc_ref[hd:hd + 1, :]
    o_win = accw_ref[0:hd, :] / accw_ref[hd:hd + 1, :]

    def gate_row(branch):
        return jnp.concatenate([gates[GATE_COL + 3 * h + branch:GATE_COL + 3 * h + branch + 1, :]
                                for h in range(nh)], axis=1)

    out = o_cmp * gate_row(0) + o_sel * gate_row(1) + o_win * gate_row(2)
    o_ref[0] = jnp.concatenate([out[:, h * tq:(h + 1) * tq] for h in range(nh)], axis=0).T


def _nsa(q, dtg, kc, vct, ks, vst, kw, vwt, ovt):
    b, s, _ = q.shape
    tq = NSA_TQ
    nc = s // CMP_STRIDE
    per_b3 = lambda shape: pl.BlockSpec((1,) + shape, lambda i, j: (i, 0, 0))
    per_b4 = lambda shape: pl.BlockSpec((1,) + shape, lambda i, j: (i, 0, 0, 0))
    qblk = lambda w: pl.BlockSpec((1, tq, w), lambda i, j: (i, j, 0))
    return pl.pallas_call(
        functools.partial(_nsa_kernel, seq=s),
        grid=(b, s // tq),
        in_specs=[
            qblk(NSA_WIDTH), qblk(LANES),
            per_b3((nc, LANES)), per_b3((NSA_HEAD_DIM, nc)),
            per_b3((s, LANES)), per_b4((s // LANES, NSA_HEAD_DIM, LANES)),
            per_b3((s, LANES)), per_b4((s // LANES, NSA_HEAD_DIM, LANES)),
            pl.BlockSpec(ovt.shape, lambda i, j: (0, 0)),
        ],
        out_specs=qblk(NSA_WIDTH),
        out_shape=jax.ShapeDtypeStruct((b, s, NSA_WIDTH), F32),
        scratch_shapes=[
            pltpu.VMEM((s // SEL_BLOCK, tq), F32),
            pltpu.VMEM((LANES, NSA_HEADS * tq), BF16),
            pltpu.VMEM((NSA_HEAD_DIM + 2 * SUBLANES, NSA_HEADS * tq), F32),
            pltpu.VMEM((1, NSA_HEADS * tq), F32),
            pltpu.VMEM((NSA_HEAD_DIM + 2 * SUBLANES, NSA_HEADS * tq), F32),
            pltpu.VMEM((1, NSA_HEADS * tq), F32),
            pltpu.VMEM((2, NSA_TK, NSA_HEADS * tq), F32),
        ],
        compiler_params=pltpu.CompilerParams(
            dimension_semantics=("arbitrary", "arbitrary"), vmem_limit_bytes=VMEM_LIMIT_BYTES),
        name="nsa",
    )(q, dtg, kc, vct, ks, vst, kw, vwt, ovt)


def _ffn_kernel(x_ref, ya_ref, yn_ref, xn_ref, yan_ref, ynn_ref, wo_ref, nw_ref, wu_ref, cw_ref, cb_ref, wd_ref,
                fw_ref, o_ref, tail_ref, act_ref, x1_ref, h_ref, *, tiles_per_seq, final_norm):
    tm = x_ref.shape[0]
    na = ya_ref.shape[1]

    def mixed_residual(xr, yar, ynr):
        x1 = (xr[...] + _dot(yar[...].astype(BF16), wo_ref[0:na, :])
              + _dot(ynr[...].astype(BF16), wo_ref[na:, :]))
        ms = jnp.mean(x1 * x1, axis=-1, keepdims=True)
        return x1, (x1 * lax.rsqrt(ms + EPS) * nw_ref[...]).astype(BF16)

    @pl.when(pl.program_id(0) % tiles_per_seq == 0)
    def _():
        tail_ref[...] = jnp.zeros_like(tail_ref)

    @pl.when(pl.program_id(0) == 0)
    def _():
        x1_0, h_0 = mixed_residual(x_ref, ya_ref, yn_ref)
        x1_ref[...] = x1_0
        h_ref[...] = h_0

    h = h_ref[...]

    def conv_cols(c0):
        cols = slice(c0, c0 + FF_TILE)
        u = _dot(h, wu_ref[:, cols])
        tail = tail_ref[:, cols]
        cv = cb_ref[:, cols] + cw_ref[FFN_CONV - 1:FFN_CONV, cols] * u
        for t in range(FFN_CONV - 1):
            cv = cv + cw_ref[t:t + 1, cols] * _shift_rows(u, tail, FFN_CONV - 1 - t)
        tail_ref[:, cols] = u[tm - SUBLANES:tm]
        return cv

    for j in range(N_FF_TILES):
        gate = conv_cols(j * FF_TILE)
        val = conv_cols(D_FF_PAD + j * FF_TILE)
        act_ref[:, j * FF_TILE:(j + 1) * FF_TILE] = (_silu(gate) * val).astype(BF16)

    x1_next, h_next = mixed_residual(xn_ref, yan_ref, ynn_ref)
    x2 = x1_ref[...] + _dot(act_ref[...], wd_ref[...])
    if final_norm:
        ms2 = jnp.mean(x2 * x2, axis=-1, keepdims=True)
        x2 = x2 * lax.rsqrt(ms2 + EPS) * fw_ref[...]
    o_ref[...] = x2
    x1_ref[...] = x1_next
    h_ref[...] = h_next


def _ffn(x2d, ya, yn, wo, nw, wu_t, cw_t, cb_t, wd, fw, layer, seq, final_norm):
    t = x2d.shape[0]
    tm = FFN_TM
    last = t // tm - 1
    row = lambda i: (i, 0)
    nxt = lambda i: (jnp.minimum(i + 1, last), 0)
    c2 = lambda a: (pl.BlockSpec(a.shape, lambda i: (0, 0)) if a.ndim == 2
                    else pl.BlockSpec((None,) + a.shape[1:], lambda i: (layer, 0, 0)))
    return pl.pallas_call(
        functools.partial(_ffn_kernel, tiles_per_seq=seq // tm, final_norm=final_norm),
        grid=(t // tm,),
        in_specs=[
            pl.BlockSpec((tm, D_MODEL), row),
            pl.BlockSpec((tm, ya.shape[1]), row),
            pl.BlockSpec((tm, yn.shape[1]), row),
            pl.BlockSpec((tm, D_MODEL), nxt),
            pl.BlockSpec((tm, ya.shape[1]), nxt),
            pl.BlockSpec((tm, yn.shape[1]), nxt),
            c2(wo), c2(nw), c2(wu_t), c2(cw_t), c2(cb_t), c2(wd), c2(fw),
        ],
        out_specs=pl.BlockSpec((tm, D_MODEL), row),
        out_shape=jax.ShapeDtypeStruct((t, D_MODEL), F32),
        scratch_shapes=[
            pltpu.VMEM((SUBLANES, 2 * D_FF_PAD), F32),
            pltpu.VMEM((tm, D_FF_PAD), BF16),
            pltpu.VMEM((tm, D_MODEL), F32),
            pltpu.VMEM((tm, D_MODEL), BF16),
        ],
        compiler_params=pltpu.CompilerParams(
            dimension_semantics=("arbitrary",), vmem_limit_bytes=VMEM_LIMIT_BYTES),
        name="outproj_ffn",
    )(x2d, ya, yn, x2d, ya, yn, wo, nw, wu_t, cw_t, cb_t, wd, fw)


def _rope_tables(seq):
    half = NSA_HEAD_DIM // 2
    inv = 1.0 / (ROPE_THETA ** (jnp.arange(half, dtype=F32) / half))
    ang = jnp.arange(seq, dtype=F32)[:, None] * inv[None, :]
    cos, sin = jnp.cos(ang), jnp.sin(ang)
    scale = NSA_HEAD_DIM ** -0.5 * LOG2E
    one = jnp.ones_like(cos)
    zero = jnp.zeros_like(cos)
    rqc = jnp.concatenate([cos, cos, cos, cos], axis=1) * scale
    rqs = jnp.concatenate([-sin, sin, -sin, sin], axis=1) * scale
    rkc = jnp.concatenate([cos, cos, one, one], axis=1)
    rks = jnp.concatenate([-sin, sin, zero, zero], axis=1)
    return rqc, rqs, rkc, rks


def _compress_weights(kw1, kw2, vw1, vw2):
    half_tokens = CMP_BLOCK // 2
    hd, hid = NSA_HEAD_DIM, CMP_HIDDEN
    kw1r = kw1.reshape(2, half_tokens, hd, hid)
    vw1r = vw1.reshape(2, half_tokens, hd, hid)
    zeros = jnp.zeros((half_tokens, hd, hid), kw1.dtype)
    cols = []
    for w1r, is_k in ((kw1r, True), (vw1r, False)):
        for part in range(2):
            blk = w1r[part]
            rows = jnp.concatenate([blk, zeros] if is_k else [zeros, blk], axis=1)
            cols.append(rows.reshape(half_tokens * LANES, hid))
    wc = jnp.concatenate(cols, axis=1).astype(BF16)
    zk = jnp.zeros((hid, hd), kw2.dtype)
    w2bd = jnp.concatenate([jnp.concatenate([kw2, zk], axis=1),
                            jnp.concatenate([zk, vw2], axis=1)], axis=0).astype(BF16)
    return wc, w2bd


def _overlap_t(seq):
    nc = seq // CMP_STRIDE
    nb = seq // SEL_BLOCK
    cmp_start = np.arange(nc) * CMP_STRIDE
    slc_start = np.arange(nb) * SEL_BLOCK
    ov = ((cmp_start[None, :] < slc_start[:, None] + SEL_BLOCK)
          & (cmp_start[None, :] + CMP_BLOCK > slc_start[:, None])
          & (np.arange(nc)[None, :] < nc - 1))
    return jnp.asarray(ov.astype(np.float32), dtype=BF16)


def _pad_rows(a, rows):
    return jnp.concatenate([a, jnp.zeros((rows - a.shape[0],) + a.shape[1:], a.dtype)], axis=0)


def _pad_cols(a, cols):
    return jnp.concatenate([a, jnp.zeros(a.shape[:-1] + (cols - a.shape[-1],), a.dtype)], axis=-1)


def _ff_tiles(a):
    return jnp.concatenate([_pad_cols(a[:, :D_FF], D_FF_PAD), _pad_cols(a[:, D_FF:], D_FF_PAD)], axis=1)


def _up_weights_kernel(w_ref, o_ref):
    rows = w_ref.shape[1]
    w = w_ref[0]
    gap = jnp.zeros((rows, D_FF_PAD - D_FF), BF16)
    o_ref[0, :, 0:D_FF] = w[:, 0:D_FF].astype(BF16)
    o_ref[0, :, D_FF:D_FF_PAD] = gap
    o_ref[0, :, D_FF_PAD:D_FF_PAD + D_FF] = w[:, D_FF:2 * D_FF].astype(BF16)
    o_ref[0, :, D_FF_PAD + D_FF:2 * D_FF_PAD] = gap


def _up_weights(w_up):
    depth, d, n = w_up.shape
    rows = FFN_TM
    return pl.pallas_call(
        _up_weights_kernel,
        grid=(depth, d // rows),
        in_specs=[pl.BlockSpec((1, rows, n), lambda l, i: (l, i, 0))],
        out_specs=pl.BlockSpec((1, rows, 2 * D_FF_PAD), lambda l, i: (l, i, 0)),
        out_shape=jax.ShapeDtypeStruct((depth, d, 2 * D_FF_PAD), BF16),
        compiler_params=pltpu.CompilerParams(
            dimension_semantics=("arbitrary", "arbitrary"), vmem_limit_bytes=VMEM_LIMIT_BYTES),
        name="up_weights",
    )(w_up)


def kernel(x, attn_norm_w, w_in, ssd_conv_w, ssd_conv_b, ssd_dt_bias, ssd_a_log, ssd_d, ssd_norm_w, sc_conv_w,
           cmp_k_pos, cmp_k_w1, cmp_k_w2, cmp_v_pos, cmp_v_w1, cmp_v_w2, w_out, ffn_norm_w, ffn_w_up,
           ffn_conv_w, ffn_conv_b, ffn_w_down, final_norm_w):
    b, s, d = x.shape
    depth = w_in.shape[0]
    assert d == D_MODEL and s % NSA_TK == 0 and s // SEL_BLOCK >= SEL_TOPK and s % INPROJ_TM == 0
    assert INPROJ_TM % NSA_TK == 0 and s >= NSA_WKEYS and NSA_TK // SEL_BLOCK <= SUBLANES
    t = b * s
    ropes = _rope_tables(s)
    ovt = _overlap_t(s)
    nc = s // CMP_STRIDE
    half_tokens = CMP_BLOCK // 2
    x2d = x.reshape(t, d)
    w_in_t = jnp.transpose(w_in, (0, 2, 1))
    wo_all = w_out.astype(BF16)
    wu_all = _up_weights(ffn_w_up)
    wd_all = jnp.pad(ffn_w_down, ((0, 0), (0, D_FF_PAD - D_FF), (0, 0))).astype(BF16)
    for l in range(depth):
        cw = _pad_rows(ssd_conv_w[l], SUBLANES)
        cb = ssd_conv_b[l][None, :]
        dtb = _pad_cols(ssd_dt_bias[l][None, :], LANES)
        aneg = _pad_cols(-jnp.exp(ssd_a_log[l].astype(F32))[None, :] * LOG2E, LANES)
        dsk = jnp.repeat(ssd_d[l].astype(F32), SSD_HEAD_DIM)[None, :]
        scw = _pad_rows(sc_conv_w[l], SUBLANES)
        wc, w2bd = _compress_weights(cmp_k_w1[l], cmp_k_w2[l], cmp_v_w1[l], cmp_v_w2[l])
        pos2 = _pad_rows(jnp.concatenate([cmp_k_pos[l], cmp_v_pos[l]], axis=1).reshape(2, half_tokens * LANES),
                         SUBLANES)
        cw_t = _ff_tiles(_pad_rows(ffn_conv_w[l], SUBLANES))
        cb_t = _ff_tiles(ffn_conv_b[l][None, :])

        z, xbc, sc, q, cpair, ks, kw, vst, vwt, dtg = _inproj(x2d, attn_norm_w[l][None, :], w_in_t, l, ropes, s)
        kc, vct = _compress(cpair.reshape(b, nc, half_tokens * LANES), pos2, wc, w2bd)
        r3 = lambda a: a.reshape(b, s, a.shape[-1])
        ya = _ssd(r3(z), r3(xbc), r3(sc), r3(dtg), cw, cb, dtb, aneg, dsk, ssd_norm_w[l][None, :], scw)
        vt4 = lambda a: a.reshape(b, s // LANES, NSA_HEAD_DIM, LANES)
        yn = _nsa(r3(q), r3(dtg), kc, vct, r3(ks), vt4(vst), r3(kw), vt4(vwt), ovt)
        x2d = _ffn(x2d, ya.reshape(t, -1), yn.reshape(t, -1), wo_all, ffn_norm_w[l][None, :], wu_all, cw_t, cb_t,
                   wd_all, final_norm_w[None, :], l, s, l == depth - 1)
    return x2d.reshape(b, s, d)
```

```python
import functools
import math

import numpy as np
import jax
import jax.numpy as jnp
from jax import lax
from jax.experimental import pallas as pl
from jax.experimental.pallas import tpu as pltpu

F32 = jnp.float32
BF16 = jnp.bfloat16

D_MODEL = 1024
SSD_HEADS = 8
SSD_HEAD_DIM = 64
SSD_INNER = SSD_HEADS * SSD_HEAD_DIM
SSD_GROUPS = 2
SSD_STATE = 128
SSD_CONV = 4
SSD_CHUNK = 256
SSD_XBC = SSD_INNER + 2 * SSD_GROUPS * SSD_STATE
SC_WIDTH = 256
SC_CONV = 3
NSA_HEADS = 4
NSA_HEAD_DIM = 64
NSA_WIDTH = NSA_HEADS * NSA_HEAD_DIM
CMP_BLOCK = 32
CMP_STRIDE = 16
CMP_HIDDEN = 128
SEL_BLOCK = 64
SEL_TOPK = 16
SEL_LOCAL = 2
WINDOW = 512
ROPE_THETA = 10000.0
D_FF = 2752
FFN_CONV = 3
EPS = 1e-6
NEG = -1e30
FORCE = 1e9
REMOVED = -3.0e38
LOG2E = 1.4426950408889634

LANES = 128
SUBLANES = 8
VMEM_LIMIT_BYTES = 56 * 1024 * 1024

D_FF_PAD = 2816
FF_TILE = 256
N_FF_TILES = D_FF_PAD // FF_TILE

COL_Z = 0
COL_XBC = COL_Z + SSD_INNER
COL_SC = COL_XBC + SSD_XBC
COL_Q = COL_SC + 3 * SC_WIDTH
COL_KV = COL_Q + NSA_WIDTH
COL_DTG = COL_KV + 6 * NSA_HEAD_DIM
N_PACK = COL_DTG + LANES
GATE_COL = SSD_HEADS
IN_DT0 = SSD_INNER + SSD_XBC
IN_DT1 = IN_DT0 + SSD_HEADS
IN_G0 = IN_DT1 + 3 * SC_WIDTH + NSA_WIDTH + 6 * NSA_HEAD_DIM
IN_G1 = IN_G0 + 3 * NSA_HEADS

INPROJ_TM = 512
NSA_TQ = 512
NSA_TK = 512
NSA_WKEYS = WINDOW + NSA_TQ
FFN_TM = 256


def _dot(a, b):
    return jnp.dot(a, b, preferred_element_type=F32)


def _dot_nt(a, b):
    return lax.dot_general(a, b, (((1,), (1,)), ((), ())), preferred_element_type=F32)


def _sigmoid(x):
    return 1.0 / (1.0 + jnp.exp(-x))


def _silu(x):
    h = 0.5 * x
    return h + h * jnp.tanh(h)


def _softplus(x):
    return jnp.maximum(x, 0.0) + jnp.log1p(jnp.exp(-jnp.abs(x)))


def _split3(a):
    a1 = a.astype(BF16)
    r1 = a - a1.astype(F32)
    a2 = r1.astype(BF16)
    r2 = r1 - a2.astype(F32)
    return a1, a2, r2.astype(BF16)


def _shift_rows(cur, tail, k):
    if k == 0:
        return cur
    rc = pltpu.roll(cur, k, 0)
    rt = pltpu.roll(tail, k, 0)
    row = lax.broadcasted_iota(jnp.int32, tail.shape, 0)
    first = jnp.where(row < k, rt, rc[0:SUBLANES])
    return jnp.concatenate([first, rc[SUBLANES:]], axis=0)


def _inproj_kernel(x_ref, nw_ref, w_ref, rqc_ref, rqs_ref, rkc_ref, rks_ref,
                   z_ref, xbc_ref, sc_ref, q_ref, cpair_ref, ks_ref, kw_ref, vst_ref, vwt_ref, dtg_ref,
                   wb_ref, cstage_ref):
    tm = x_ref.shape[0]

    @pl.when(pl.program_id(0) == 0)
    def _():
        def cast_rows(dst, src, n):
            for r in range(0, n, LANES):
                m = min(LANES, n - r)
                wb_ref[dst + r:dst + r + m, :] = w_ref[0, src + r:src + r + m, :].astype(BF16)

        cast_rows(0, 0, IN_DT0)
        cast_rows(IN_DT0, IN_DT1, IN_G0 - IN_DT1)
        tail = jnp.concatenate([w_ref[0, IN_DT0:IN_DT1, :], w_ref[0, IN_G0:IN_G1, :],
                                jnp.zeros((LANES - (IN_DT1 - IN_DT0) - (IN_G1 - IN_G0), D_MODEL), F32)], axis=0)
        wb_ref[COL_DTG:N_PACK, :] = tail.astype(BF16)

    x = x_ref[...]
    ms = jnp.mean(x * x, axis=-1, keepdims=True)
    h = (x * lax.rsqrt(ms + EPS) * nw_ref[...]).astype(BF16)

    def proj(a, b):
        return _dot_nt(h, wb_ref[a:b, :])

    z_ref[...] = proj(COL_Z, COL_XBC)
    xbc_ref[...] = proj(COL_XBC, COL_SC)
    sc_ref[...] = proj(COL_SC, COL_Q)
    dtg_ref[...] = proj(COL_DTG, N_PACK)

    lane = lax.broadcasted_iota(jnp.int32, (tm, LANES), 1)
    first_half = (lane % NSA_HEAD_DIM) < (NSA_HEAD_DIM // 2)

    def rope(v, c, s):
        partner = jnp.where(first_half, pltpu.roll(v, LANES - 32, 1), pltpu.roll(v, 32, 1))
        return v * c + partner * s

    rqc, rqs, rkc, rks = rqc_ref[...], rqs_ref[...], rkc_ref[...], rks_ref[...]
    q = proj(COL_Q, COL_KV)
    q_ref[:, 0:LANES] = rope(q[:, 0:LANES], rqc, rqs)
    q_ref[:, LANES:2 * LANES] = rope(q[:, LANES:2 * LANES], rqc, rqs)

    kv = proj(COL_KV, COL_DTG)
    cstage_ref[...] = rope(kv[:, 0:LANES], rkc, rks)
    group = CMP_BLOCK // 2
    for r in range(group):
        cpair_ref[:, r * LANES:(r + 1) * LANES] = cstage_ref[pl.ds(r, tm // group, stride=group), :]
    row = lax.broadcasted_iota(jnp.int32, (tm, LANES), 0)
    block_in_tile = (row % NSA_TK) // SEL_BLOCK
    onehot = jnp.where(lane - NSA_HEAD_DIM == block_in_tile, 1.0, 0.0)
    k_lanes = lane < NSA_HEAD_DIM
    for pair, k_ref, vt_ref, fill in ((1, ks_ref, vst_ref, onehot), (2, kw_ref, vwt_ref, 0.0)):
        p = rope(kv[:, pair * LANES:(pair + 1) * LANES], rkc, rks)
        k_ref[...] = jnp.where(k_lanes, p, fill).astype(BF16)
        pt = p.T
        for j in range(tm // LANES):
            vt_ref[j] = pt[NSA_HEAD_DIM:LANES, j * LANES:(j + 1) * LANES].astype(BF16)


def _inproj(x2d, nw, w_t, layer, ropes, seq):
    t = x2d.shape[0]
    tm = INPROJ_TM
    nt = t // tm
    pos_blocks = seq // tm
    row = lambda i: (i, 0)
    pos = lambda i: (i % pos_blocks, 0)
    const = lambda i: (0, 0)
    out_shapes = (
        jax.ShapeDtypeStruct((t, SSD_INNER), F32),
        jax.ShapeDtypeStruct((t, SSD_XBC), F32),
        jax.ShapeDtypeStruct((t, 3 * SC_WIDTH), F32),
        jax.ShapeDtypeStruct((t, NSA_WIDTH), F32),
        jax.ShapeDtypeStruct((t // (CMP_BLOCK // 2), (CMP_BLOCK // 2) * LANES), F32),
        jax.ShapeDtypeStruct((t, LANES), BF16),
        jax.ShapeDtypeStruct((t, LANES), BF16),
        jax.ShapeDtypeStruct((t // LANES, NSA_HEAD_DIM, LANES), BF16),
        jax.ShapeDtypeStruct((t // LANES, NSA_HEAD_DIM, LANES), BF16),
        jax.ShapeDtypeStruct((t, LANES), F32),
    )
    vt_spec = pl.BlockSpec((tm // LANES, NSA_HEAD_DIM, LANES), lambda i: (i, 0, 0))
    out_specs = (
        pl.BlockSpec((tm, SSD_INNER), row),
        pl.BlockSpec((tm, SSD_XBC), row),
        pl.BlockSpec((tm, 3 * SC_WIDTH), row),
        pl.BlockSpec((tm, NSA_WIDTH), row),
        pl.BlockSpec((tm // (CMP_BLOCK // 2), (CMP_BLOCK // 2) * LANES), row),
        pl.BlockSpec((tm, LANES), row),
        pl.BlockSpec((tm, LANES), row),
        vt_spec,
        vt_spec,
        pl.BlockSpec((tm, LANES), row),
    )
    in_specs = [
        pl.BlockSpec((tm, D_MODEL), row),
        pl.BlockSpec((1, D_MODEL), const),
        pl.BlockSpec((1, IN_G1, D_MODEL), lambda i: (layer, 0, 0), pipeline_mode=pl.Buffered(1)),
        pl.BlockSpec((tm, LANES), pos),
        pl.BlockSpec((tm, LANES), pos),
        pl.BlockSpec((tm, LANES), pos),
        pl.BlockSpec((tm, LANES), pos),
    ]
    return pl.pallas_call(
        _inproj_kernel,
        grid=(nt,),
        in_specs=in_specs,
        out_specs=out_specs,
        out_shape=out_shapes,
        scratch_shapes=[pltpu.VMEM((N_PACK, D_MODEL), BF16), pltpu.VMEM((tm, LANES), F32)],
        compiler_params=pltpu.CompilerParams(
            dimension_semantics=("arbitrary",), vmem_limit_bytes=VMEM_LIMIT_BYTES),
        name="inproj",
    )(x2d, nw, w_t, *ropes)


def _compress_kernel(x2_ref, pos_ref, wc_ref, w2_ref, kc_ref, vct_ref):
    nc = x2_ref.shape[1]
    x2 = x2_ref[0].astype(BF16)
    wc = wc_ref[...]
    y = _dot(x2, wc)
    r = _dot(pos_ref[...].astype(BF16), wc)
    h = CMP_HIDDEN

    def pre(base):
        bias = r[0:1, base:base + h] + r[1:2, base + h:base + 2 * h]
        return y[:, base:base + h] + pltpu.roll(y[:, base + h:base + 2 * h], nc - 1, 0) + bias

    hid = jnp.concatenate([jax.nn.gelu(pre(0)), jax.nn.gelu(pre(2 * h))], axis=1).astype(BF16)
    o = _dot(hid, w2_ref[...])
    kc_ref[0] = o.astype(BF16)
    vct_ref[0] = o.T[NSA_HEAD_DIM:LANES, :].astype(BF16)


def _compress(x2p, pos2, wc, w2bd):
    b, nc, width = x2p.shape
    return pl.pallas_call(
        _compress_kernel,
        grid=(b,),
        in_specs=[
            pl.BlockSpec((1, nc, width), lambda i: (i, 0, 0)),
            pl.BlockSpec(pos2.shape, lambda i: (0, 0)),
            pl.BlockSpec(wc.shape, lambda i: (0, 0)),
            pl.BlockSpec(w2bd.shape, lambda i: (0, 0)),
        ],
        out_specs=(
            pl.BlockSpec((1, nc, LANES), lambda i: (i, 0, 0)),
            pl.BlockSpec((1, NSA_HEAD_DIM, nc), lambda i: (i, 0, 0)),
        ),
        out_shape=(
            jax.ShapeDtypeStruct((b, nc, LANES), BF16),
            jax.ShapeDtypeStruct((b, NSA_HEAD_DIM, nc), BF16),
        ),
        compiler_params=pltpu.CompilerParams(
            dimension_semantics=("arbitrary",), vmem_limit_bytes=VMEM_LIMIT_BYTES),
        name="compress",
    )(x2p, pos2, wc, w2bd)


def _ssd_kernel(z_ref, xbc_ref, sc_ref, dtg_ref, cw_ref, cb_ref, dtb_ref, aneg_ref, dsk_ref, nw_ref, scw_ref,
                y_ref, h_ref, xtail_ref, stail_ref):
    L = SSD_CHUNK
    half = SSD_HEAD_DIM

    @pl.when(pl.program_id(1) == 0)
    def _():
        h_ref[...] = jnp.zeros_like(h_ref)
        xtail_ref[...] = jnp.zeros_like(xtail_ref)
        stail_ref[...] = jnp.zeros_like(stail_ref)

    xraw = xbc_ref[0]
    xtail = xtail_ref[...]
    conv = cb_ref[...] + cw_ref[SSD_CONV - 1:SSD_CONV, :] * xraw
    for j in range(SSD_CONV - 1):
        conv = conv + cw_ref[j:j + 1, :] * _shift_rows(xraw, xtail, SSD_CONV - 1 - j)
    xtail_ref[...] = xraw[L - SUBLANES:L]
    xc = _silu(conv)

    dt = _softplus(dtg_ref[0] + dtb_ref[...])
    a = dt * aneg_ref[...]
    row = lax.broadcasted_iota(jnp.int32, (L, L), 0)
    col = lax.broadcasted_iota(jnp.int32, (L, L), 1)
    causal = row >= col
    tri = jnp.where(causal, 1.0, 0.0).astype(BF16)
    a1, a2, a3 = _split3(a)
    acs = _dot(tri, a1) + _dot(tri, a2) + _dot(tri, a3)
    acs_t = acs.T
    exp_acs = jnp.exp2(acs)
    last = acs[L - 1:L, :]
    dte = jnp.exp2(last - acs)
    chunk_decay = jnp.exp2(last)

    lo = lax.broadcasted_iota(jnp.int32, (L, LANES), 1) < half
    lo_row = lax.broadcasted_iota(jnp.int32, (1, LANES), 1) < half

    def per_lane(m, ha, hb, mask):
        return jnp.where(mask, m[:, ha:ha + 1], m[:, hb:hb + 1])

    ys = []
    for g in range(SSD_GROUPS):
        b_g = xc[:, SSD_INNER + g * SSD_STATE:SSD_INNER + (g + 1) * SSD_STATE]
        c_g = xc[:, SSD_INNER + (SSD_GROUPS + g) * SSD_STATE:SSD_INNER + (SSD_GROUPS + g + 1) * SSD_STATE]
        b_bf = b_g.astype(BF16)
        c_bf = c_g.astype(BF16)
        cb = _dot_nt(c_bf, b_bf)
        bt_bf = b_g.T.astype(BF16)
        for pp in range(SSD_HEADS // SSD_GROUPS // 2):
            p = g * (SSD_HEADS // SSD_GROUPS // 2) + pp
            ha, hb = 2 * p, 2 * p + 1
            x_pair = xc[:, p * LANES:(p + 1) * LANES]
            xdt = x_pair * per_lane(dt, ha, hb, lo)
            xdt_bf = xdt.astype(BF16)
            yd = []
            for hd in (ha, hb):
                seg = acs[:, hd:hd + 1] - acs_t[hd:hd + 1, :]
                decay = jnp.exp2(jnp.where(causal, seg, NEG))
                yd.append(_dot((decay * cb).astype(BF16), xdt_bf))
            y_diag = jnp.where(lo, yd[0], yd[1])
            st = _dot(bt_bf, (xdt * per_lane(dte, ha, hb, lo)).astype(BF16))
            h_prev = h_ref[p]
            y_off = _dot(c_bf, h_prev.astype(BF16)) * per_lane(exp_acs, ha, hb, lo)
            h_ref[p] = h_prev * per_lane(chunk_decay, ha, hb, lo_row) + st
            ys.append(y_diag + y_off + x_pair * dsk_ref[:, p * LANES:(p + 1) * LANES])
    y = jnp.concatenate(ys, axis=1)
    z = z_ref[0]
    y = y * _silu(z)
    ms = jnp.mean(y * y, axis=-1, keepdims=True)
    y_ref[0, :, 0:SSD_INNER] = y * lax.rsqrt(ms + EPS) * nw_ref[...]

    sc = sc_ref[0]
    u = sc[:, SC_WIDTH:2 * SC_WIDTH] * sc[:, 2 * SC_WIDTH:3 * SC_WIDTH]
    stail = stail_ref[...]
    cv = scw_ref[SC_CONV - 1:SC_CONV, :] * u
    for j in range(SC_CONV - 1):
        cv = cv + scw_ref[j:j + 1, :] * _shift_rows(u, stail, SC_CONV - 1 - j)
    stail_ref[...] = u[L - SUBLANES:L]
    y_ref[0, :, SSD_INNER:SSD_INNER + SC_WIDTH] = sc[:, 0:SC_WIDTH] * cv


def _ssd(z, xbc, sc, dtg, cw, cb, dtb, aneg, dsk, nw, scw):
    b, s, _ = z.shape
    L = SSD_CHUNK
    blk = lambda w: pl.BlockSpec((1, L, w), lambda i, c: (i, c, 0))
    par = lambda a: pl.BlockSpec(a.shape, lambda i, c: (0, 0))
    return pl.pallas_call(
        _ssd_kernel,
        grid=(b, s // L),
        in_specs=[blk(SSD_INNER), blk(SSD_XBC), blk(3 * SC_WIDTH), blk(LANES),
                  par(cw), par(cb), par(dtb), par(aneg), par(dsk), par(nw), par(scw)],
        out_specs=blk(SSD_INNER + SC_WIDTH),
        out_shape=jax.ShapeDtypeStruct((b, s, SSD_INNER + SC_WIDTH), F32),
        scratch_shapes=[
            pltpu.VMEM((SSD_HEADS // 2, SSD_STATE, LANES), F32),
            pltpu.VMEM((SUBLANES, SSD_XBC), F32),
            pltpu.VMEM((SUBLANES, SC_WIDTH), F32),
        ],
        compiler_params=pltpu.CompilerParams(
            dimension_semantics=("arbitrary", "arbitrary"), vmem_limit_bytes=VMEM_LIMIT_BYTES),
        name="ssd_sc",
    )(z, xbc, sc, dtg, cw, cb, dtb, aneg, dsk, nw, scw)


def _nsa_kernel(q_ref, dtg_ref, kc_ref, vct_ref, ks_ref, vst_ref, kw_ref, vwt_ref, ovt_ref,
                o_ref, sel_ref, qa_ref, acc_ref, m_ref, accw_ref, mw_ref, s_ref, *, seq):
    tq, tk = NSA_TQ, NSA_TK
    hd = NSA_HEAD_DIM
    nh = NSA_HEADS
    nc = seq // CMP_STRIDE
    nb = seq // SEL_BLOCK
    q0 = pl.program_id(1) * tq
    tpos = q0 + lax.broadcasted_iota(jnp.int32, (1, tq), 1)

    def heads_on_lanes(a):
        return jnp.concatenate([a] * nh, axis=1)

    qt = q_ref[0].T
    qa_ref[0:hd, :] = jnp.concatenate([qt[h * hd:(h + 1) * hd, :] for h in range(nh)], axis=1).astype(BF16)
    qa_ref[hd:LANES, :] = jnp.zeros((LANES - hd, nh * tq), BF16)
    gates = _sigmoid(dtg_ref[0]).T

    n_io = lax.broadcasted_iota(jnp.int32, (nc, tq), 0)
    cbias = jnp.where((n_io * CMP_STRIDE + (CMP_BLOCK - 1)) <= tpos, 0.0, NEG)
    s = _dot(kc_ref[0], qa_ref[...]) + heads_on_lanes(cbias)
    p = jnp.exp2(s - jnp.max(s, axis=0, keepdims=True))
    has_key = heads_on_lanes(jnp.where(tpos >= CMP_BLOCK - 1, 1.0, 0.0))
    p = p * (has_key / jnp.sum(p, axis=0, keepdims=True))
    o_cmp = _dot(vct_ref[0], p.astype(BF16))
    psum = p[:, 0:tq]
    for h in range(1, nh):
        psum = psum + p[:, h * tq:(h + 1) * tq]

    def load_vt(ref, key0, width):
        first = key0 // LANES
        return jnp.concatenate([ref[0, first + i] for i in range(width // LANES)], axis=1)

    ovt = ovt_ref[...]
    p1, p2, p3 = _split3(psum)
    imp = _dot(ovt, p1) + _dot(ovt, p2) + _dot(ovt, p3)
    j_io = lax.broadcasted_iota(jnp.int32, (nb, tq), 0)
    cur = jnp.right_shift(tpos, int(math.log2(SEL_BLOCK)))
    valid = j_io <= cur
    forced = (j_io == 0) | (valid & (j_io > cur - SEL_LOCAL))
    v = jnp.where(forced, REMOVED, jnp.where(valid, imp, NEG))
    j_f = j_io.astype(F32)

    def extract(_, carry):
        v, sel = carry
        m = jnp.max(v, axis=0, keepdims=True)
        first = jnp.min(jnp.where(v == m, j_f, float(nb)), axis=0, keepdims=True)
        hit = j_f == first
        return jnp.where(hit, REMOVED, v), jnp.where(hit, 1.0, sel)

    _, sel = lax.fori_loop(0, SEL_TOPK - 1 - SEL_LOCAL, extract, (v, jnp.where(forced, 1.0, 0.0)), unroll=True)
    sel_ref[...] = sel

    wt = NSA_WKEYS // 2
    w0 = pl.multiple_of(jnp.maximum(q0 - WINDOW, 0), tq)
    blocks_per_tile = tk // SEL_BLOCK
    bias_rows = 2 * SUBLANES
    for ref in (acc_ref, accw_ref):
        ref[...] = jnp.zeros_like(ref)
    for ref in (m_ref, mw_ref):
        ref[...] = jnp.full(ref.shape, NEG, F32)

    def scores_of(buf, rows, k_tile):
        def one_head(h):
            lanes = slice(h * tq, (h + 1) * tq)
            s_ref[buf, 0:rows, lanes] = _dot(k_tile, qa_ref[:, lanes])
        return one_head

    def update_of(buf, rows, vt, bias, m_r, acc_r):
        ones = jnp.where(lax.broadcasted_iota(jnp.int32, (bias_rows, rows), 0) == 0, 1.0, 0.0).astype(BF16)
        vt_aug = jnp.concatenate([vt, ones], axis=0)

        def one_head(h):
            lanes = slice(h * tq, (h + 1) * tq)
            s = s_ref[buf, 0:rows, lanes]
            if bias is not None:
                s = s + bias
            m_old = m_r[:, lanes]
            m_new = jnp.maximum(m_old, jnp.max(s, axis=0, keepdims=True))
            p = jnp.exp2(s - m_new).astype(BF16)
            acc_r[:, lanes] = jnp.exp2(m_old - m_new) * acc_r[:, lanes] + _dot(vt_aug, p)
            m_r[:, lanes] = m_new
        return one_head

    def stage(score_fn, update_fn):
        for h in range(nh):
            if score_fn is not None:
                score_fn(h)
            if update_fn is not None:
                update_fn(h)

    def win_scores(buf, i):
        k0 = pl.multiple_of(w0 + i * wt, LANES)
        return scores_of(buf, wt, kw_ref[0, pl.ds(k0, wt), :])

    def win_update(buf, i):
        k0 = pl.multiple_of(w0 + i * wt, LANES)
        d = (k0 - q0) + (lax.broadcasted_iota(jnp.int32, (wt, tq), 0)
                         - lax.broadcasted_iota(jnp.int32, (wt, tq), 1))
        bias = jnp.where(d <= 0, jnp.where(d > -WINDOW, 0.0, NEG), NEG)
        return update_of(buf, wt, load_vt(vwt_ref, k0, wt), bias, mw_ref, accw_ref)

    def sel_scores(buf, kt):
        k0 = pl.multiple_of(kt * tk, tk)
        chunk = sel_ref[pl.ds(pl.multiple_of(kt * blocks_per_tile, blocks_per_tile), blocks_per_tile), :]
        bias = jnp.concatenate([(1.0 - chunk) * NEG, jnp.zeros((bias_rows - blocks_per_tile, tq), F32)], axis=0)
        qa_ref[hd:hd + bias_rows, :] = heads_on_lanes(bias).astype(BF16)
        return scores_of(buf, tk, ks_ref[0, pl.ds(k0, tk), :])

    def sel_update(buf, kt, causal):
        k0 = pl.multiple_of(kt * tk, tk)
        bias = None
        if causal:
            bias = jnp.where(k0 + lax.broadcasted_iota(jnp.int32, (tk, tq), 0) <= tpos, 0.0, NEG)
        return update_of(buf, tk, load_vt(vst_ref, k0, tk), bias, m_ref, acc_ref)

    stage(win_scores(0, 0), None)
    stage(win_scores(1, 1), win_update(0, 0))
    stage(sel_scores(0, 0), win_update(1, 1))

    n_last = (q0 + tq + tk - 1) // tk - 1

    def trip(first, tiles):
        for i in range(tiles):
            stage(sel_scores((i + 1) % 2, first + i + 1), sel_update(i % 2, first + i, False))

    def quad(j, _):
        trip(4 * j, 4)
        return 0

    lax.fori_loop(0, n_last // 4, quad, 0)
    done4 = (n_last // 4) * 4

    def pair(j, _):
        trip(done4 + 2 * j, 2)
        return 0

    lax.fori_loop(0, (n_last - done4) // 2, pair, 0)

    @pl.when(n_last % 2 == 0)
    def _():
        stage(None, sel_update(0, n_last, True))

    @pl.when(n_last % 2 == 1)
    def _():
        stage(sel_scores(1, n_last), sel_update(0, n_last - 1, False))
        stage(None, sel_update(1, n_last, True))

    o_sel = acc_ref[0:hd, :] / acc_ref[hd:hd + 1, :]
    o_win = accw_ref[0:hd, :] / accw_ref[hd:hd + 1, :]

    def gate_row(branch):
        return jnp.concatenate([gates[GATE_COL + 3 * h + branch:GATE_COL + 3 * h + branch + 1, :]
                                for h in range(nh)], axis=1)

    out = o_cmp * gate_row(0) + o_sel * gate_row(1) + o_win * gate_row(2)
    o_ref[0] = jnp.concatenate([out[:, h * tq:(h + 1) * tq] for h in range(nh)], axis=0).T


def _nsa(q, dtg, kc, vct, ks, vst, kw, vwt, ovt):
    b, s, _ = q.shape
    tq = NSA_TQ
    nc = s // CMP_STRIDE
    per_b3 = lambda shape: pl.BlockSpec((1,) + shape, lambda i, j: (i, 0, 0))
    per_b4 = lambda shape: pl.BlockSpec((1,) + shape, lambda i, j: (i, 0, 0, 0))
    qblk = lambda w: pl.BlockSpec((1, tq, w), lambda i, j: (i, j, 0))
    return pl.pallas_call(
        functools.partial(_nsa_kernel, seq=s),
        grid=(b, s // tq),
        in_specs=[
            qblk(NSA_WIDTH), qblk(LANES),
            per_b3((nc, LANES)), per_b3((NSA_HEAD_DIM, nc)),
            per_b3((s, LANES)), per_b4((s // LANES, NSA_HEAD_DIM, LANES)),
            per_b3((s, LANES)), per_b4((s // LANES, NSA_HEAD_DIM, LANES)),
            pl.BlockSpec(ovt.shape, lambda i, j: (0, 0)),
        ],
        out_specs=qblk(NSA_WIDTH),
        out_shape=jax.ShapeDtypeStruct((b, s, NSA_WIDTH), F32),
        scratch_shapes=[
            pltpu.VMEM((s // SEL_BLOCK, tq), F32),
            pltpu.VMEM((LANES, NSA_HEADS * tq), BF16),
            pltpu.VMEM((NSA_HEAD_DIM + 2 * SUBLANES, NSA_HEADS * tq), F32),
            pltpu.VMEM((1, NSA_HEADS * tq), F32),
            pltpu.VMEM((NSA_HEAD_DIM + 2 * SUBLANES, NSA_HEADS * tq), F32),
            pltpu.VMEM((1, NSA_HEADS * tq), F32),
            pltpu.VMEM((2, NSA_TK, NSA_HEADS * tq), F32),
        ],
        compiler_params=pltpu.CompilerParams(
            dimension_semantics=("arbitrary", "arbitrary"), vmem_limit_bytes=VMEM_LIMIT_BYTES),
        name="nsa",
    )(q, dtg, kc, vct, ks, vst, kw, vwt, ovt)


def _ffn_kernel(x_ref, ya_ref, yn_ref, xn_ref, yan_ref, ynn_ref, wo_ref, nw_ref, wu_ref, cw_ref, cb_ref, wd_ref,
                fw_ref, o_ref, tail_ref, act_ref, x1_ref, h_ref, *, tiles_per_seq, final_norm):
    tm = x_ref.shape[0]
    na = ya_ref.shape[1]

    def mixed_residual(xr, yar, ynr):
        x1 = (xr[...] + _dot(yar[...].astype(BF16), wo_ref[0:na, :])
              + _dot(ynr[...].astype(BF16), wo_ref[na:, :]))
        ms = jnp.mean(x1 * x1, axis=-1, keepdims=True)
        return x1, (x1 * lax.rsqrt(ms + EPS) * nw_ref[...]).astype(BF16)

    @pl.when(pl.program_id(0) % tiles_per_seq == 0)
    def _():
        tail_ref[...] = jnp.zeros_like(tail_ref)

    @pl.when(pl.program_id(0) == 0)
    def _():
        x1_0, h_0 = mixed_residual(x_ref, ya_ref, yn_ref)
        x1_ref[...] = x1_0
        h_ref[...] = h_0

    h = h_ref[...]

    def conv_cols(c0):
        cols = slice(c0, c0 + FF_TILE)
        u = _dot(h, wu_ref[:, cols])
        tail = tail_ref[:, cols]
        cv = cb_ref[:, cols] + cw_ref[FFN_CONV - 1:FFN_CONV, cols] * u
        for t in range(FFN_CONV - 1):
            cv = cv + cw_ref[t:t + 1, cols] * _shift_rows(u, tail, FFN_CONV - 1 - t)
        tail_ref[:, cols] = u[tm - SUBLANES:tm]
        return cv

    for j in range(N_FF_TILES):
        gate = conv_cols(j * FF_TILE)
        val = conv_cols(D_FF_PAD + j * FF_TILE)
        act_ref[:, j * FF_TILE:(j + 1) * FF_TILE] = (_silu(gate) * val).astype(BF16)

    x1_next, h_next = mixed_residual(xn_ref, yan_ref, ynn_ref)
    x2 = x1_ref[...] + _dot(act_ref[...], wd_ref[...])
    if final_norm:
        ms2 = jnp.mean(x2 * x2, axis=-1, keepdims=True)
        x2 = x2 * lax.rsqrt(ms2 + EPS) * fw_ref[...]
    o_ref[...] = x2
    x1_ref[...] = x1_next
    h_ref[...] = h_next


def _ffn(x2d, ya, yn, wo, nw, wu_t, cw_t, cb_t, wd, fw, layer, seq, final_norm):
    t = x2d.shape[0]
    tm = FFN_TM
    last = t // tm - 1
    row = lambda i: (i, 0)
    nxt = lambda i: (jnp.minimum(i + 1, last), 0)
    c2 = lambda a: (pl.BlockSpec(a.shape, lambda i: (0, 0)) if a.ndim == 2
                    else pl.BlockSpec((None,) + a.shape[1:], lambda i: (layer, 0, 0)))
    return pl.pallas_call(
        functools.partial(_ffn_kernel, tiles_per_seq=seq // tm, final_norm=final_norm),
        grid=(t // tm,),
        in_specs=[
            pl.BlockSpec((tm, D_MODEL), row),
            pl.BlockSpec((tm, ya.shape[1]), row),
            pl.BlockSpec((tm, yn.shape[1]), row),
            pl.BlockSpec((tm, D_MODEL), nxt),
            pl.BlockSpec((tm, ya.shape[1]), nxt),
            pl.BlockSpec((tm, yn.shape[1]), nxt),
            c2(wo), c2(nw), c2(wu_t), c2(cw_t), c2(cb_t), c2(wd), c2(fw),
        ],
        out_specs=pl.BlockSpec((tm, D_MODEL), row),
        out_shape=jax.ShapeDtypeStruct((t, D_MODEL), F32),
        scratch_shapes=[
            pltpu.VMEM((SUBLANES, 2 * D_FF_PAD), F32),
            pltpu.VMEM((tm, D_FF_PAD), BF16),
            pltpu.VMEM((tm, D_MODEL), F32),
            pltpu.VMEM((tm, D_MODEL), BF16),
        ],
        compiler_params=pltpu.CompilerParams(
            dimension_semantics=("arbitrary",), vmem_limit_bytes=VMEM_LIMIT_BYTES),
        name="outproj_ffn",
    )(x2d, ya, yn, x2d, ya, yn, wo, nw, wu_t, cw_t, cb_t, wd, fw)


def _rope_tables(seq):
    half = NSA_HEAD_DIM // 2
    f32 = np.float32
    inv = (f32(1.0) / (f32(ROPE_THETA) ** (np.arange(half, dtype=f32) / f32(half)))).astype(f32)
    ang = (np.arange(seq, dtype=f32)[:, None] * inv[None, :]).astype(f32).astype(np.float64)
    cos, sin = np.cos(ang), np.sin(ang)
    scale = NSA_HEAD_DIM ** -0.5 * LOG2E
    one = np.ones_like(cos)
    zero = np.zeros_like(cos)
    rqc = np.concatenate([cos, cos, cos, cos], axis=1) * scale
    rqs = np.concatenate([-sin, sin, -sin, sin], axis=1) * scale
    rkc = np.concatenate([cos, cos, one, one], axis=1)
    rks = np.concatenate([-sin, sin, zero, zero], axis=1)
    return tuple(jnp.asarray(a.astype(f32)) for a in (rqc, rqs, rkc, rks))


def _compress_weights(kw1, kw2, vw1, vw2):
    half_tokens = CMP_BLOCK // 2
    hd, hid = NSA_HEAD_DIM, CMP_HIDDEN
    kw1r = kw1.reshape(2, half_tokens, hd, hid)
    vw1r = vw1.reshape(2, half_tokens, hd, hid)
    zeros = jnp.zeros((half_tokens, hd, hid), kw1.dtype)
    cols = []
    for w1r, is_k in ((kw1r, True), (vw1r, False)):
        for part in range(2):
            blk = w1r[part]
            rows = jnp.concatenate([blk, zeros] if is_k else [zeros, blk], axis=1)
            cols.append(rows.reshape(half_tokens * LANES, hid))
    wc = jnp.concatenate(cols, axis=1).astype(BF16)
    zk = jnp.zeros((hid, hd), kw2.dtype)
    w2bd = jnp.concatenate([jnp.concatenate([kw2, zk], axis=1),
                            jnp.concatenate([zk, vw2], axis=1)], axis=0).astype(BF16)
    return wc, w2bd


def _overlap_t(seq):
    nc = seq // CMP_STRIDE
    nb = seq // SEL_BLOCK
    cmp_start = np.arange(nc) * CMP_STRIDE
    slc_start = np.arange(nb) * SEL_BLOCK
    ov = ((cmp_start[None, :] < slc_start[:, None] + SEL_BLOCK)
          & (cmp_start[None, :] + CMP_BLOCK > slc_start[:, None])
          & (np.arange(nc)[None, :] < nc - 1))
    return jnp.asarray(ov.astype(np.float32), dtype=BF16)


def _pad_rows(a, rows):
    return jnp.concatenate([a, jnp.zeros((rows - a.shape[0],) + a.shape[1:], a.dtype)], axis=0)


def _pad_cols(a, cols):
    return jnp.concatenate([a, jnp.zeros(a.shape[:-1] + (cols - a.shape[-1],), a.dtype)], axis=-1)


def _ff_tiles(a):
    return jnp.concatenate([_pad_cols(a[:, :D_FF], D_FF_PAD), _pad_cols(a[:, D_FF:], D_FF_PAD)], axis=1)


def _up_weights_kernel(w_ref, o_ref):
    rows = w_ref.shape[1]
    w = w_ref[0]
    gap = jnp.zeros((rows, D_FF_PAD - D_FF), BF16)
    o_ref[0, :, 0:D_FF] = w[:, 0:D_FF].astype(BF16)
    o_ref[0, :, D_FF:D_FF_PAD] = gap
    o_ref[0, :, D_FF_PAD:D_FF_PAD + D_FF] = w[:, D_FF:2 * D_FF].astype(BF16)
    o_ref[0, :, D_FF_PAD + D_FF:2 * D_FF_PAD] = gap


def _up_weights(w_up):
    depth, d, n = w_up.shape
    rows = FFN_TM
    return pl.pallas_call(
        _up_weights_kernel,
        grid=(depth, d // rows),
        in_specs=[pl.BlockSpec((1, rows, n), lambda l, i: (l, i, 0))],
        out_specs=pl.BlockSpec((1, rows, 2 * D_FF_PAD), lambda l, i: (l, i, 0)),
        out_shape=jax.ShapeDtypeStruct((depth, d, 2 * D_FF_PAD), BF16),
        compiler_params=pltpu.CompilerParams(
            dimension_semantics=("arbitrary", "arbitrary"), vmem_limit_bytes=VMEM_LIMIT_BYTES),
        name="up_weights",
    )(w_up)


def _down_weights_kernel(w_ref, o_ref):
    rows = w_ref.shape[1]
    row = pl.program_id(1) * rows + lax.broadcasted_iota(jnp.int32, (rows, 1), 0)
    o_ref[0] = jnp.where(row < D_FF, w_ref[0], 0.0).astype(BF16)


def _down_weights(w_down):
    depth, _, d = w_down.shape
    rows = FF_TILE
    return pl.pallas_call(
        _down_weights_kernel,
        grid=(depth, D_FF_PAD // rows),
        in_specs=[pl.BlockSpec((1, rows, d), lambda l, i: (l, i, 0))],
        out_specs=pl.BlockSpec((1, rows, d), lambda l, i: (l, i, 0)),
        out_shape=jax.ShapeDtypeStruct((depth, D_FF_PAD, d), BF16),
        compiler_params=pltpu.CompilerParams(
            dimension_semantics=("arbitrary", "arbitrary"), vmem_limit_bytes=VMEM_LIMIT_BYTES),
        name="down_weights",
    )(w_down)


def kernel(x, attn_norm_w, w_in, ssd_conv_w, ssd_conv_b, ssd_dt_bias, ssd_a_log, ssd_d, ssd_norm_w, sc_conv_w,
           cmp_k_pos, cmp_k_w1, cmp_k_w2, cmp_v_pos, cmp_v_w1, cmp_v_w2, w_out, ffn_norm_w, ffn_w_up,
           ffn_conv_w, ffn_conv_b, ffn_w_down, final_norm_w):
    b, s, d = x.shape
    depth = w_in.shape[0]
    assert d == D_MODEL and s % NSA_TK == 0 and s // SEL_BLOCK >= SEL_TOPK and s % INPROJ_TM == 0
    assert INPROJ_TM % NSA_TK == 0 and s >= NSA_WKEYS and NSA_TK // SEL_BLOCK <= SUBLANES
    t = b * s
    ropes = _rope_tables(s)
    ovt = _overlap_t(s)
    nc = s // CMP_STRIDE
    half_tokens = CMP_BLOCK // 2
    x2d = x.reshape(t, d)
    w_in_t = jnp.transpose(w_in, (0, 2, 1))
    wo_all = w_out.astype(BF16)
    wu_all = _up_weights(ffn_w_up)
    wd_all = _down_weights(ffn_w_down)
    for l in range(depth):
        cw = _pad_rows(ssd_conv_w[l], SUBLANES)
        cb = ssd_conv_b[l][None, :]
        dtb = _pad_cols(ssd_dt_bias[l][None, :], LANES)
        aneg = _pad_cols(-jnp.exp(ssd_a_log[l].astype(F32))[None, :] * LOG2E, LANES)
        dsk = jnp.repeat(ssd_d[l].astype(F32), SSD_HEAD_DIM)[None, :]
        scw = _pad_rows(sc_conv_w[l], SUBLANES)
        wc, w2bd = _compress_weights(cmp_k_w1[l], cmp_k_w2[l], cmp_v_w1[l], cmp_v_w2[l])
        pos2 = _pad_rows(jnp.concatenate([cmp_k_pos[l], cmp_v_pos[l]], axis=1).reshape(2, half_tokens * LANES),
                         SUBLANES)
        cw_t = _ff_tiles(_pad_rows(ffn_conv_w[l], SUBLANES))
        cb_t = _ff_tiles(ffn_conv_b[l][None, :])

        z, xbc, sc, q, cpair, ks, kw, vst, vwt, dtg = _inproj(x2d, attn_norm_w[l][None, :], w_in_t, l, ropes, s)
        kc, vct = _compress(cpair.reshape(b, nc, half_tokens * LANES), pos2, wc, w2bd)
        r3 = lambda a: a.reshape(b, s, a.shape[-1])
        ya = _ssd(r3(z), r3(xbc), r3(sc), r3(dtg), cw, cb, dtb, aneg, dsk, ssd_norm_w[l][None, :], scw)
        vt4 = lambda a: a.reshape(b, s // LANES, NSA_HEAD_DIM, LANES)
        yn = _nsa(r3(q), r3(dtg), kc, vct, r3(ks), vt4(vst), r3(kw), vt4(vwt), ovt)
        x2d = _ffn(x2d, ya.reshape(t, -1), yn.reshape(t, -1), wo_all, ffn_norm_w[l][None, :], wu_all, cw_t, cb_t,
                   wd_all, final_norm_w[None, :], l, s, l == depth - 1)
    return x2d.reshape(b, s, d)
```

```python
import functools
import math

import numpy as np
import jax
import jax.numpy as jnp
from jax import lax
from jax.experimental import pallas as pl
from jax.experimental.pallas import tpu as pltpu

F32 = jnp.float32
BF16 = jnp.bfloat16

D_MODEL = 1024
SSD_HEADS = 8
SSD_HEAD_DIM = 64
SSD_INNER = SSD_HEADS * SSD_HEAD_DIM
SSD_GROUPS = 2
SSD_STATE = 128
SSD_CONV = 4
SSD_CHUNK = 256
SSD_XBC = SSD_INNER + 2 * SSD_GROUPS * SSD_STATE
SC_WIDTH = 256
SC_CONV = 3
NSA_HEADS = 4
NSA_HEAD_DIM = 64
NSA_WIDTH = NSA_HEADS * NSA_HEAD_DIM
CMP_BLOCK = 32
CMP_STRIDE = 16
CMP_HIDDEN = 128
SEL_BLOCK = 64
SEL_TOPK = 16
SEL_LOCAL = 2
WINDOW = 512
ROPE_THETA = 10000.0
D_FF = 2752
FFN_CONV = 3
EPS = 1e-6
NEG = -1e30
FORCE = 1e9
REMOVED = -3.0e38
LOG2E = 1.4426950408889634

LANES = 128
SUBLANES = 8
VMEM_LIMIT_BYTES = 56 * 1024 * 1024

D_FF_PAD = 2816
FF_TILE = 256
N_FF_TILES = D_FF_PAD // FF_TILE

COL_Z = 0
COL_XBC = COL_Z + SSD_INNER
COL_SC = COL_XBC + SSD_XBC
COL_Q = COL_SC + 3 * SC_WIDTH
COL_KV = COL_Q + NSA_WIDTH
COL_DTG = COL_KV + 6 * NSA_HEAD_DIM
N_PACK = COL_DTG + LANES
GATE_COL = SSD_HEADS
IN_DT0 = SSD_INNER + SSD_XBC
IN_DT1 = IN_DT0 + SSD_HEADS
IN_G0 = IN_DT1 + 3 * SC_WIDTH + NSA_WIDTH + 6 * NSA_HEAD_DIM
IN_G1 = IN_G0 + 3 * NSA_HEADS

INPROJ_TM = 512
NSA_TQ = 512
NSA_TK = 512
NSA_WKEYS = WINDOW + NSA_TQ
FFN_TM = 256


def _dot(a, b):
    return jnp.dot(a, b, preferred_element_type=F32)


def _dot_nt(a, b):
    return lax.dot_general(a, b, (((1,), (1,)), ((), ())), preferred_element_type=F32)


def _sigmoid(x):
    return 1.0 / (1.0 + jnp.exp(-x))


def _silu(x):
    h = 0.5 * x
    return h + h * jnp.tanh(h)


def _softplus(x):
    return jnp.maximum(x, 0.0) + jnp.log1p(jnp.exp(-jnp.abs(x)))


def _split3(a):
    a1 = a.astype(BF16)
    r1 = a - a1.astype(F32)
    a2 = r1.astype(BF16)
    r2 = r1 - a2.astype(F32)
    return a1, a2, r2.astype(BF16)


def _shift_rows(cur, tail, k):
    if k == 0:
        return cur
    rc = pltpu.roll(cur, k, 0)
    rt = pltpu.roll(tail, k, 0)
    row = lax.broadcasted_iota(jnp.int32, tail.shape, 0)
    first = jnp.where(row < k, rt, rc[0:SUBLANES])
    return jnp.concatenate([first, rc[SUBLANES:]], axis=0)


def _inproj_kernel(x_ref, nw_ref, w_ref, rqc_ref, rqs_ref, rkc_ref, rks_ref,
                   z_ref, xbc_ref, sc_ref, q_ref, cpair_ref, ks_ref, kw_ref, vst_ref, vwt_ref, dtg_ref,
                   wb_ref, cstage_ref):
    tm = x_ref.shape[0]

    @pl.when(pl.program_id(0) == 0)
    def _():
        def cast_rows(dst, src, n):
            for r in range(0, n, LANES):
                m = min(LANES, n - r)
                wb_ref[dst + r:dst + r + m, :] = w_ref[0, src + r:src + r + m, :].astype(BF16)

        cast_rows(0, 0, IN_DT0)
        cast_rows(IN_DT0, IN_DT1, IN_G0 - IN_DT1)
        tail = jnp.concatenate([w_ref[0, IN_DT0:IN_DT1, :], w_ref[0, IN_G0:IN_G1, :],
                                jnp.zeros((LANES - (IN_DT1 - IN_DT0) - (IN_G1 - IN_G0), D_MODEL), F32)], axis=0)
        wb_ref[COL_DTG:N_PACK, :] = tail.astype(BF16)

    x = x_ref[...]
    ms = jnp.mean(x * x, axis=-1, keepdims=True)
    h = (x * lax.rsqrt(ms + EPS) * nw_ref[...]).astype(BF16)

    def proj(a, b):
        return _dot_nt(h, wb_ref[a:b, :])

    z_ref[...] = proj(COL_Z, COL_XBC)
    xbc_ref[...] = proj(COL_XBC, COL_SC)
    sc_ref[...] = proj(COL_SC, COL_Q)
    dtg_ref[...] = proj(COL_DTG, N_PACK)

    lane = lax.broadcasted_iota(jnp.int32, (tm, LANES), 1)
    first_half = (lane % NSA_HEAD_DIM) < (NSA_HEAD_DIM // 2)

    def rope(v, c, s):
        partner = jnp.where(first_half, pltpu.roll(v, LANES - 32, 1), pltpu.roll(v, 32, 1))
        return v * c + partner * s

    rqc, rqs, rkc, rks = rqc_ref[...], rqs_ref[...], rkc_ref[...], rks_ref[...]
    q = proj(COL_Q, COL_KV)
    q_ref[:, 0:LANES] = rope(q[:, 0:LANES], rqc, rqs)
    q_ref[:, LANES:2 * LANES] = rope(q[:, LANES:2 * LANES], rqc, rqs)

    kv = proj(COL_KV, COL_DTG)
    cstage_ref[...] = rope(kv[:, 0:LANES], rkc, rks)
    group = CMP_BLOCK // 2
    for r in range(group):
        cpair_ref[:, r * LANES:(r + 1) * LANES] = cstage_ref[pl.ds(r, tm // group, stride=group), :]
    row = lax.broadcasted_iota(jnp.int32, (tm, LANES), 0)
    block_in_tile = (row % NSA_TK) // SEL_BLOCK
    onehot = jnp.where(lane - NSA_HEAD_DIM == block_in_tile, 1.0, 0.0)
    k_lanes = lane < NSA_HEAD_DIM
    for pair, k_ref, vt_ref, fill in ((1, ks_ref, vst_ref, onehot), (2, kw_ref, vwt_ref, 0.0)):
        p = rope(kv[:, pair * LANES:(pair + 1) * LANES], rkc, rks)
        k_ref[...] = jnp.where(k_lanes, p, fill).astype(BF16)
        pt = p.T
        for j in range(tm // LANES):
            vt_ref[j] = pt[NSA_HEAD_DIM:LANES, j * LANES:(j + 1) * LANES].astype(BF16)


def _inproj(x2d, nw, w_t, layer, ropes, seq):
    t = x2d.shape[0]
    tm = INPROJ_TM
    nt = t // tm
    pos_blocks = seq // tm
    row = lambda i: (i, 0)
    pos = lambda i: (i % pos_blocks, 0)
    const = lambda i: (0, 0)
    out_shapes = (
        jax.ShapeDtypeStruct((t, SSD_INNER), F32),
        jax.ShapeDtypeStruct((t, SSD_XBC), F32),
        jax.ShapeDtypeStruct((t, 3 * SC_WIDTH), F32),
        jax.ShapeDtypeStruct((t, NSA_WIDTH), F32),
        jax.ShapeDtypeStruct((t // (CMP_BLOCK // 2), (CMP_BLOCK // 2) * LANES), F32),
        jax.ShapeDtypeStruct((t, LANES), BF16),
        jax.ShapeDtypeStruct((t, LANES), BF16),
        jax.ShapeDtypeStruct((t // LANES, NSA_HEAD_DIM, LANES), BF16),
        jax.ShapeDtypeStruct((t // LANES, NSA_HEAD_DIM, LANES), BF16),
        jax.ShapeDtypeStruct((t, LANES), F32),
    )
    vt_spec = pl.BlockSpec((tm // LANES, NSA_HEAD_DIM, LANES), lambda i: (i, 0, 0))
    out_specs = (
        pl.BlockSpec((tm, SSD_INNER), row),
        pl.BlockSpec((tm, SSD_XBC), row),
        pl.BlockSpec((tm, 3 * SC_WIDTH), row),
        pl.BlockSpec((tm, NSA_WIDTH), row),
        pl.BlockSpec((tm // (CMP_BLOCK // 2), (CMP_BLOCK // 2) * LANES), row),
        pl.BlockSpec((tm, LANES), row),
        pl.BlockSpec((tm, LANES), row),
        vt_spec,
        vt_spec,
        pl.BlockSpec((tm, LANES), row),
    )
    in_specs = [
        pl.BlockSpec((tm, D_MODEL), row),
        pl.BlockSpec((1, D_MODEL), const),
        pl.BlockSpec((1, IN_G1, D_MODEL), lambda i: (layer, 0, 0), pipeline_mode=pl.Buffered(1)),
        pl.BlockSpec((tm, LANES), pos),
        pl.BlockSpec((tm, LANES), pos),
        pl.BlockSpec((tm, LANES), pos),
        pl.BlockSpec((tm, LANES), pos),
    ]
    return pl.pallas_call(
        _inproj_kernel,
        grid=(nt,),
        in_specs=in_specs,
        out_specs=out_specs,
        out_shape=out_shapes,
        scratch_shapes=[pltpu.VMEM((N_PACK, D_MODEL), BF16), pltpu.VMEM((tm, LANES), F32)],
        compiler_params=pltpu.CompilerParams(
            dimension_semantics=("arbitrary",), vmem_limit_bytes=VMEM_LIMIT_BYTES),
        name="inproj",
    )(x2d, nw, w_t, *ropes)


def _compress_kernel(x2_ref, pos_ref, wc_ref, w2_ref, kc_ref, vct_ref):
    nc = x2_ref.shape[1]
    x2 = x2_ref[0].astype(BF16)
    wc = wc_ref[...]
    y = _dot(x2, wc)
    r = _dot(pos_ref[...].astype(BF16), wc)
    h = CMP_HIDDEN

    def pre(base):
        bias = r[0:1, base:base + h] + r[1:2, base + h:base + 2 * h]
        return y[:, base:base + h] + pltpu.roll(y[:, base + h:base + 2 * h], nc - 1, 0) + bias

    hid = jnp.concatenate([jax.nn.gelu(pre(0)), jax.nn.gelu(pre(2 * h))], axis=1).astype(BF16)
    o = _dot(hid, w2_ref[...])
    kc_ref[0] = o.astype(BF16)
    vct_ref[0] = o.T[NSA_HEAD_DIM:LANES, :].astype(BF16)


def _compress(x2p, pos2, wc, w2bd):
    b, nc, width = x2p.shape
    return pl.pallas_call(
        _compress_kernel,
        grid=(b,),
        in_specs=[
            pl.BlockSpec((1, nc, width), lambda i: (i, 0, 0)),
            pl.BlockSpec(pos2.shape, lambda i: (0, 0)),
            pl.BlockSpec(wc.shape, lambda i: (0, 0)),
            pl.BlockSpec(w2bd.shape, lambda i: (0, 0)),
        ],
        out_specs=(
            pl.BlockSpec((1, nc, LANES), lambda i: (i, 0, 0)),
            pl.BlockSpec((1, NSA_HEAD_DIM, nc), lambda i: (i, 0, 0)),
        ),
        out_shape=(
            jax.ShapeDtypeStruct((b, nc, LANES), BF16),
            jax.ShapeDtypeStruct((b, NSA_HEAD_DIM, nc), BF16),
        ),
        compiler_params=pltpu.CompilerParams(
            dimension_semantics=("arbitrary",), vmem_limit_bytes=VMEM_LIMIT_BYTES),
        name="compress",
    )(x2p, pos2, wc, w2bd)


def _ssd_kernel(z_ref, xbc_ref, sc_ref, dtg_ref, cw_ref, cb_ref, dtb_ref, aneg_ref, dsk_ref, nw_ref, scw_ref,
                y_ref, h_ref, xtail_ref, stail_ref):
    L = SSD_CHUNK
    half = SSD_HEAD_DIM

    @pl.when(pl.program_id(1) == 0)
    def _():
        h_ref[...] = jnp.zeros_like(h_ref)
        xtail_ref[...] = jnp.zeros_like(xtail_ref)
        stail_ref[...] = jnp.zeros_like(stail_ref)

    xraw = xbc_ref[0]
    xtail = xtail_ref[...]
    conv = cb_ref[...] + cw_ref[SSD_CONV - 1:SSD_CONV, :] * xraw
    for j in range(SSD_CONV - 1):
        conv = conv + cw_ref[j:j + 1, :] * _shift_rows(xraw, xtail, SSD_CONV - 1 - j)
    xtail_ref[...] = xraw[L - SUBLANES:L]
    xc = _silu(conv)

    dt = _softplus(dtg_ref[0] + dtb_ref[...])
    a = dt * aneg_ref[...]
    row = lax.broadcasted_iota(jnp.int32, (L, L), 0)
    col = lax.broadcasted_iota(jnp.int32, (L, L), 1)
    causal = row >= col
    tri = jnp.where(causal, 1.0, 0.0).astype(BF16)
    a1, a2, a3 = _split3(a)
    acs = _dot(tri, a1) + _dot(tri, a2) + _dot(tri, a3)
    acs_t = acs.T
    exp_acs = jnp.exp2(acs)
    last = acs[L - 1:L, :]
    dte = jnp.exp2(last - acs)
    chunk_decay = jnp.exp2(last)

    lo = lax.broadcasted_iota(jnp.int32, (L, LANES), 1) < half
    lo_row = lax.broadcasted_iota(jnp.int32, (1, LANES), 1) < half

    def per_lane(m, ha, hb, mask):
        return jnp.where(mask, m[:, ha:ha + 1], m[:, hb:hb + 1])

    ys = []
    for g in range(SSD_GROUPS):
        b_g = xc[:, SSD_INNER + g * SSD_STATE:SSD_INNER + (g + 1) * SSD_STATE]
        c_g = xc[:, SSD_INNER + (SSD_GROUPS + g) * SSD_STATE:SSD_INNER + (SSD_GROUPS + g + 1) * SSD_STATE]
        b_bf = b_g.astype(BF16)
        c_bf = c_g.astype(BF16)
        cb = _dot_nt(c_bf, b_bf)
        bt_bf = b_g.T.astype(BF16)
        for pp in range(SSD_HEADS // SSD_GROUPS // 2):
            p = g * (SSD_HEADS // SSD_GROUPS // 2) + pp
            ha, hb = 2 * p, 2 * p + 1
            x_pair = xc[:, p * LANES:(p + 1) * LANES]
            xdt = x_pair * per_lane(dt, ha, hb, lo)
            xdt_bf = xdt.astype(BF16)
            yd = []
            for hd in (ha, hb):
                seg = acs[:, hd:hd + 1] - acs_t[hd:hd + 1, :]
                decay = jnp.exp2(jnp.where(causal, seg, NEG))
                yd.append(_dot((decay * cb).astype(BF16), xdt_bf))
            y_diag = jnp.where(lo, yd[0], yd[1])
            st = _dot(bt_bf, (xdt * per_lane(dte, ha, hb, lo)).astype(BF16))
            h_prev = h_ref[p]
            y_off = _dot(c_bf, h_prev.astype(BF16)) * per_lane(exp_acs, ha, hb, lo)
            h_ref[p] = h_prev * per_lane(chunk_decay, ha, hb, lo_row) + st
            ys.append(y_diag + y_off + x_pair * dsk_ref[:, p * LANES:(p + 1) * LANES])
    y = jnp.concatenate(ys, axis=1)
    z = z_ref[0]
    y = y * _silu(z)
    ms = jnp.mean(y * y, axis=-1, keepdims=True)
    y_ref[0, :, 0:SSD_INNER] = y * lax.rsqrt(ms + EPS) * nw_ref[...]

    sc = sc_ref[0]
    u = sc[:, SC_WIDTH:2 * SC_WIDTH] * sc[:, 2 * SC_WIDTH:3 * SC_WIDTH]
    stail = stail_ref[...]
    cv = scw_ref[SC_CONV - 1:SC_CONV, :] * u
    for j in range(SC_CONV - 1):
        cv = cv + scw_ref[j:j + 1, :] * _shift_rows(u, stail, SC_CONV - 1 - j)
    stail_ref[...] = u[L - SUBLANES:L]
    y_ref[0, :, SSD_INNER:SSD_INNER + SC_WIDTH] = sc[:, 0:SC_WIDTH] * cv


def _ssd(z, xbc, sc, dtg, cw, cb, dtb, aneg, dsk, nw, scw):
    b, s, _ = z.shape
    L = SSD_CHUNK
    blk = lambda w: pl.BlockSpec((1, L, w), lambda i, c: (i, c, 0))
    par = lambda a: pl.BlockSpec(a.shape, lambda i, c: (0, 0))
    return pl.pallas_call(
        _ssd_kernel,
        grid=(b, s // L),
        in_specs=[blk(SSD_INNER), blk(SSD_XBC), blk(3 * SC_WIDTH), blk(LANES),
                  par(cw), par(cb), par(dtb), par(aneg), par(dsk), par(nw), par(scw)],
        out_specs=blk(SSD_INNER + SC_WIDTH),
        out_shape=jax.ShapeDtypeStruct((b, s, SSD_INNER + SC_WIDTH), F32),
        scratch_shapes=[
            pltpu.VMEM((SSD_HEADS // 2, SSD_STATE, LANES), F32),
            pltpu.VMEM((SUBLANES, SSD_XBC), F32),
            pltpu.VMEM((SUBLANES, SC_WIDTH), F32),
        ],
        compiler_params=pltpu.CompilerParams(
            dimension_semantics=("arbitrary", "arbitrary"), vmem_limit_bytes=VMEM_LIMIT_BYTES),
        name="ssd_sc",
    )(z, xbc, sc, dtg, cw, cb, dtb, aneg, dsk, nw, scw)


def _nsa_kernel(q_ref, dtg_ref, kc_ref, vct_ref, ks_ref, vst_ref, kw_ref, vwt_ref, ovt_ref,
                o_ref, sel_ref, qa_ref, acc_ref, m_ref, accw_ref, mw_ref, s_ref, *, seq):
    tq, tk = NSA_TQ, NSA_TK
    hd = NSA_HEAD_DIM
    nh = NSA_HEADS
    nc = seq // CMP_STRIDE
    nb = seq // SEL_BLOCK
    q0 = pl.program_id(1) * tq
    tpos = q0 + lax.broadcasted_iota(jnp.int32, (1, tq), 1)

    def heads_on_lanes(a):
        return jnp.concatenate([a] * nh, axis=1)

    qt = q_ref[0].T
    qa_ref[0:hd, :] = jnp.concatenate([qt[h * hd:(h + 1) * hd, :] for h in range(nh)], axis=1).astype(BF16)
    qa_ref[hd:LANES, :] = jnp.zeros((LANES - hd, nh * tq), BF16)
    gates = _sigmoid(dtg_ref[0]).T

    n_io = lax.broadcasted_iota(jnp.int32, (nc, tq), 0)
    cbias = jnp.where((n_io * CMP_STRIDE + (CMP_BLOCK - 1)) <= tpos, 0.0, NEG)
    s = _dot(kc_ref[0], qa_ref[...]) + heads_on_lanes(cbias)
    p = jnp.exp2(s - jnp.max(s, axis=0, keepdims=True))
    has_key = heads_on_lanes(jnp.where(tpos >= CMP_BLOCK - 1, 1.0, 0.0))
    p = p * (has_key / jnp.sum(p, axis=0, keepdims=True))
    o_cmp = _dot(vct_ref[0], p.astype(BF16))
    psum = p[:, 0:tq]
    for h in range(1, nh):
        psum = psum + p[:, h * tq:(h + 1) * tq]

    def load_vt(ref, key0, width):
        first = key0 // LANES
        return jnp.concatenate([ref[0, first + i] for i in range(width // LANES)], axis=1)

    ovt = ovt_ref[...]
    p1, p2, p3 = _split3(psum)
    imp = _dot(ovt, p1) + _dot(ovt, p2) + _dot(ovt, p3)
    j_io = lax.broadcasted_iota(jnp.int32, (nb, tq), 0)
    cur = jnp.right_shift(tpos, int(math.log2(SEL_BLOCK)))
    valid = j_io <= cur
    forced = (j_io == 0) | (valid & (j_io > cur - SEL_LOCAL))
    v = jnp.where(forced, REMOVED, jnp.where(valid, imp, NEG))
    j_f = j_io.astype(F32)

    def extract(_, carry):
        v, sel = carry
        m = jnp.max(v, axis=0, keepdims=True)
        first = jnp.min(jnp.where(v == m, j_f, float(nb)), axis=0, keepdims=True)
        hit = j_f == first
        return jnp.where(hit, REMOVED, v), jnp.where(hit, 1.0, sel)

    _, sel = lax.fori_loop(0, SEL_TOPK - 1 - SEL_LOCAL, extract, (v, jnp.where(forced, 1.0, 0.0)), unroll=True)
    sel_ref[...] = sel

    wt = NSA_WKEYS // 2
    w0 = pl.multiple_of(jnp.maximum(q0 - WINDOW, 0), tq)
    blocks_per_tile = tk // SEL_BLOCK
    bias_rows = 2 * SUBLANES
    for ref in (acc_ref, accw_ref):
        ref[...] = jnp.zeros_like(ref)
    for ref in (m_ref, mw_ref):
        ref[...] = jnp.full(ref.shape, NEG, F32)

    def scores_of(buf, rows, k_tile):
        def one_head(h):
            lanes = slice(h * tq, (h + 1) * tq)
            s_ref[buf, 0:rows, lanes] = _dot(k_tile, qa_ref[:, lanes])
        return one_head

    def update_of(buf, rows, vt, bias, m_r, acc_r):
        ones = jnp.where(lax.broadcasted_iota(jnp.int32, (bias_rows, rows), 0) == 0, 1.0, 0.0).astype(BF16)
        vt_aug = jnp.concatenate([vt, ones], axis=0)

        def one_head(h):
            lanes = slice(h * tq, (h + 1) * tq)
            s = s_ref[buf, 0:rows, lanes]
            if bias is not None:
                s = s + bias
            m_old = m_r[:, lanes]
            m_new = jnp.maximum(m_old, jnp.max(s, axis=0, keepdims=True))
            p = jnp.exp2(s - m_new).astype(BF16)
            acc_r[:, lanes] = jnp.exp2(m_old - m_new) * acc_r[:, lanes] + _dot(vt_aug, p)
            m_r[:, lanes] = m_new
        return one_head

    def stage(score_fn, update_fn):
        for h in range(nh):
            if score_fn is not None:
                score_fn(h)
            if update_fn is not None:
                update_fn(h)

    def win_scores(buf, i):
        k0 = pl.multiple_of(w0 + i * wt, LANES)
        return scores_of(buf, wt, kw_ref[0, pl.ds(k0, wt), :])

    def win_update(buf, i):
        k0 = pl.multiple_of(w0 + i * wt, LANES)
        d = (k0 - q0) + (lax.broadcasted_iota(jnp.int32, (wt, tq), 0)
                         - lax.broadcasted_iota(jnp.int32, (wt, tq), 1))
        bias = jnp.where(d <= 0, jnp.where(d > -WINDOW, 0.0, NEG), NEG)
        return update_of(buf, wt, load_vt(vwt_ref, k0, wt), bias, mw_ref, accw_ref)

    def sel_scores(buf, kt):
        k0 = pl.multiple_of(kt * tk, tk)
        chunk = sel_ref[pl.ds(pl.multiple_of(kt * blocks_per_tile, blocks_per_tile), blocks_per_tile), :]
        bias = jnp.concatenate([(1.0 - chunk) * NEG, jnp.zeros((bias_rows - blocks_per_tile, tq), F32)], axis=0)
        qa_ref[hd:hd + bias_rows, :] = heads_on_lanes(bias).astype(BF16)
        return scores_of(buf, tk, ks_ref[0, pl.ds(k0, tk), :])

    def sel_update(buf, kt, causal):
        k0 = pl.multiple_of(kt * tk, tk)
        bias = None
        if causal:
            bias = jnp.where(k0 + lax.broadcasted_iota(jnp.int32, (tk, tq), 0) <= tpos, 0.0, NEG)
        return update_of(buf, tk, load_vt(vst_ref, k0, tk), bias, m_ref, acc_ref)

    stage(win_scores(0, 0), None)
    stage(win_scores(1, 1), win_update(0, 0))
    stage(sel_scores(0, 0), win_update(1, 1))

    n_last = (q0 + tq + tk - 1) // tk - 1

    def trip(first, tiles):
        for i in range(tiles):
            stage(sel_scores((i + 1) % 2, first + i + 1), sel_update(i % 2, first + i, False))

    def quad(j, _):
        trip(4 * j, 4)
        return 0

    lax.fori_loop(0, n_last // 4, quad, 0)
    done4 = (n_last // 4) * 4

    def pair(j, _):
        trip(done4 + 2 * j, 2)
        return 0

    lax.fori_loop(0, (n_last - done4) // 2, pair, 0)

    @pl.when(n_last % 2 == 0)
    def _():
        stage(None, sel_update(0, n_last, True))

    @pl.when(n_last % 2 == 1)
    def _():
        stage(sel_scores(1, n_last), sel_update(0, n_last - 1, False))
        stage(None, sel_update(1, n_last, True))

    o_sel = acc_ref[0:hd, :] / acc_ref[hd:hd + 1, :]
    o_win = accw_ref[0:hd, :] / accw_ref[hd:hd + 1, :]

    def gate_row(branch):
        return jnp.concatenate([gates[GATE_COL + 3 * h + branch:GATE_COL + 3 * h + branch + 1, :]
                                for h in range(nh)], axis=1)

    out = o_cmp * gate_row(0) + o_sel * gate_row(1) + o_win * gate_row(2)
    o_ref[0] = jnp.concatenate([out[:, h * tq:(h + 1) * tq] for h in range(nh)], axis=0).T


def _nsa(q, dtg, kc, vct, ks, vst, kw, vwt, ovt):
    b, s, _ = q.shape
    tq = NSA_TQ
    nc = s // CMP_STRIDE
    per_b3 = lambda shape: pl.BlockSpec((1,) + shape, lambda i, j: (i, 0, 0))
    per_b4 = lambda shape: pl.BlockSpec((1,) + shape, lambda i, j: (i, 0, 0, 0))
    qblk = lambda w: pl.BlockSpec((1, tq, w), lambda i, j: (i, j, 0))
    return pl.pallas_call(
        functools.partial(_nsa_kernel, seq=s),
        grid=(b, s // tq),
        in_specs=[
            qblk(NSA_WIDTH), qblk(LANES),
            per_b3((nc, LANES)), per_b3((NSA_HEAD_DIM, nc)),
            per_b3((s, LANES)), per_b4((s // LANES, NSA_HEAD_DIM, LANES)),
            per_b3((s, LANES)), per_b4((s // LANES, NSA_HEAD_DIM, LANES)),
            pl.BlockSpec(ovt.shape, lambda i, j: (0, 0)),
        ],
        out_specs=qblk(NSA_WIDTH),
        out_shape=jax.ShapeDtypeStruct((b, s, NSA_WIDTH), F32),
        scratch_shapes=[
            pltpu.VMEM((s // SEL_BLOCK, tq), F32),
            pltpu.VMEM((LANES, NSA_HEADS * tq), BF16),
            pltpu.VMEM((NSA_HEAD_DIM + 2 * SUBLANES, NSA_HEADS * tq), F32),
            pltpu.VMEM((1, NSA_HEADS * tq), F32),
            pltpu.VMEM((NSA_HEAD_DIM + 2 * SUBLANES, NSA_HEADS * tq), F32),
            pltpu.VMEM((1, NSA_HEADS * tq), F32),
            pltpu.VMEM((2, NSA_TK, NSA_HEADS * tq), F32),
        ],
        compiler_params=pltpu.CompilerParams(
            dimension_semantics=("arbitrary", "arbitrary"), vmem_limit_bytes=VMEM_LIMIT_BYTES),
        name="nsa",
    )(q, dtg, kc, vct, ks, vst, kw, vwt, ovt)


def _ffn_kernel(x_ref, ya_ref, yn_ref, xn_ref, yan_ref, ynn_ref, wo_ref, nw_ref, wu_ref, cw_ref, cb_ref, wd_ref,
                fw_ref, o_ref, tail_ref, act_ref, x1_ref, h_ref, *, tiles_per_seq, final_norm):
    tm = x_ref.shape[0]
    na = ya_ref.shape[1]

    def mixed_residual(xr, yar, ynr):
        x1 = (xr[...] + _dot(yar[...].astype(BF16), wo_ref[0:na, :])
              + _dot(ynr[...].astype(BF16), wo_ref[na:, :]))
        ms = jnp.mean(x1 * x1, axis=-1, keepdims=True)
        return x1, (x1 * lax.rsqrt(ms + EPS) * nw_ref[...]).astype(BF16)

    @pl.when(pl.program_id(0) % tiles_per_seq == 0)
    def _():
        tail_ref[...] = jnp.zeros_like(tail_ref)

    @pl.when(pl.program_id(0) == 0)
    def _():
        x1_0, h_0 = mixed_residual(x_ref, ya_ref, yn_ref)
        x1_ref[...] = x1_0
        h_ref[...] = h_0

    h = h_ref[...]

    def conv_cols(c0):
        cols = slice(c0, c0 + FF_TILE)
        u = _dot(h, wu_ref[:, cols])
        tail = tail_ref[:, cols]
        cv = cb_ref[:, cols] + cw_ref[FFN_CONV - 1:FFN_CONV, cols] * u
        for t in range(FFN_CONV - 1):
            cv = cv + cw_ref[t:t + 1, cols] * _shift_rows(u, tail, FFN_CONV - 1 - t)
        tail_ref[:, cols] = u[tm - SUBLANES:tm]
        return cv

    for j in range(N_FF_TILES):
        gate = conv_cols(j * FF_TILE)
        val = conv_cols(D_FF_PAD + j * FF_TILE)
        act_ref[:, j * FF_TILE:(j + 1) * FF_TILE] = (_silu(gate) * val).astype(BF16)

    x1_next, h_next = mixed_residual(xn_ref, yan_ref, ynn_ref)
    x2 = x1_ref[...] + _dot(act_ref[...], wd_ref[...])
    if final_norm:
        ms2 = jnp.mean(x2 * x2, axis=-1, keepdims=True)
        x2 = x2 * lax.rsqrt(ms2 + EPS) * fw_ref[...]
    o_ref[...] = x2
    x1_ref[...] = x1_next
    h_ref[...] = h_next


def _ffn(x2d, ya, yn, wo, nw, wu_t, cw_t, cb_t, wd, fw, layer, seq, final_norm):
    t = x2d.shape[0]
    tm = FFN_TM
    last = t // tm - 1
    row = lambda i: (i, 0)
    nxt = lambda i: (jnp.minimum(i + 1, last), 0)
    c2 = lambda a: (pl.BlockSpec(a.shape, lambda i: (0, 0)) if a.ndim == 2
                    else pl.BlockSpec((None,) + a.shape[1:], lambda i: (layer, 0, 0)))
    return pl.pallas_call(
        functools.partial(_ffn_kernel, tiles_per_seq=seq // tm, final_norm=final_norm),
        grid=(t // tm,),
        in_specs=[
            pl.BlockSpec((tm, D_MODEL), row),
            pl.BlockSpec((tm, ya.shape[1]), row),
            pl.BlockSpec((tm, yn.shape[1]), row),
            pl.BlockSpec((tm, D_MODEL), nxt),
            pl.BlockSpec((tm, ya.shape[1]), nxt),
            pl.BlockSpec((tm, yn.shape[1]), nxt),
            c2(wo), c2(nw), c2(wu_t), c2(cw_t), c2(cb_t), c2(wd), c2(fw),
        ],
        out_specs=pl.BlockSpec((tm, D_MODEL), row),
        out_shape=jax.ShapeDtypeStruct((t, D_MODEL), F32),
        scratch_shapes=[
            pltpu.VMEM((SUBLANES, 2 * D_FF_PAD), F32),
            pltpu.VMEM((tm, D_FF_PAD), BF16),
            pltpu.VMEM((tm, D_MODEL), F32),
            pltpu.VMEM((tm, D_MODEL), BF16),
        ],
        compiler_params=pltpu.CompilerParams(
            dimension_semantics=("arbitrary",), vmem_limit_bytes=VMEM_LIMIT_BYTES),
        name="outproj_ffn",
    )(x2d, ya, yn, x2d, ya, yn, wo, nw, wu_t, cw_t, cb_t, wd, fw)


def _rope_tables(seq):
    half = NSA_HEAD_DIM // 2
    f32 = np.float32
    inv = (f32(1.0) / (f32(ROPE_THETA) ** (np.arange(half, dtype=f32) / f32(half)))).astype(f32)
    ang = (np.arange(seq, dtype=f32)[:, None] * inv[None, :]).astype(f32).astype(np.float64)
    cos, sin = np.cos(ang), np.sin(ang)
    scale = NSA_HEAD_DIM ** -0.5 * LOG2E
    one = np.ones_like(cos)
    zero = np.zeros_like(cos)
    rqc = np.concatenate([cos, cos, cos, cos], axis=1) * scale
    rqs = np.concatenate([-sin, sin, -sin, sin], axis=1) * scale
    rkc = np.concatenate([cos, cos, one, one], axis=1)
    rks = np.concatenate([-sin, sin, zero, zero], axis=1)
    return tuple(jnp.asarray(a.astype(f32)) for a in (rqc, rqs, rkc, rks))


def _compress_weights(kw1, kw2, vw1, vw2):
    half_tokens = CMP_BLOCK // 2
    hd, hid = NSA_HEAD_DIM, CMP_HIDDEN
    kw1r = kw1.reshape(2, half_tokens, hd, hid)
    vw1r = vw1.reshape(2, half_tokens, hd, hid)
    zeros = jnp.zeros((half_tokens, hd, hid), kw1.dtype)
    cols = []
    for w1r, is_k in ((kw1r, True), (vw1r, False)):
        for part in range(2):
            blk = w1r[part]
            rows = jnp.concatenate([blk, zeros] if is_k else [zeros, blk], axis=1)
            cols.append(rows.reshape(half_tokens * LANES, hid))
    wc = jnp.concatenate(cols, axis=1).astype(BF16)
    zk = jnp.zeros((hid, hd), kw2.dtype)
    w2bd = jnp.concatenate([jnp.concatenate([kw2, zk], axis=1),
                            jnp.concatenate([zk, vw2], axis=1)], axis=0).astype(BF16)
    return wc, w2bd


def _overlap_t(seq):
    nc = seq // CMP_STRIDE
    nb = seq // SEL_BLOCK
    cmp_start = np.arange(nc) * CMP_STRIDE
    slc_start = np.arange(nb) * SEL_BLOCK
    ov = ((cmp_start[None, :] < slc_start[:, None] + SEL_BLOCK)
          & (cmp_start[None, :] + CMP_BLOCK > slc_start[:, None])
          & (np.arange(nc)[None, :] < nc - 1))
    return jnp.asarray(ov.astype(np.float32), dtype=BF16)


def _pad_rows(a, rows):
    return jnp.concatenate([a, jnp.zeros((rows - a.shape[0],) + a.shape[1:], a.dtype)], axis=0)


def _pad_cols(a, cols):
    return jnp.concatenate([a, jnp.zeros(a.shape[:-1] + (cols - a.shape[-1],), a.dtype)], axis=-1)


def _ff_tiles(a):
    return jnp.concatenate([_pad_cols(a[:, :D_FF], D_FF_PAD), _pad_cols(a[:, D_FF:], D_FF_PAD)], axis=1)


def _up_weights_kernel(w_ref, o_ref):
    rows = w_ref.shape[1]
    w = w_ref[0]
    gap = jnp.zeros((rows, D_FF_PAD - D_FF), BF16)
    o_ref[0, :, 0:D_FF] = w[:, 0:D_FF].astype(BF16)
    o_ref[0, :, D_FF:D_FF_PAD] = gap
    o_ref[0, :, D_FF_PAD:D_FF_PAD + D_FF] = w[:, D_FF:2 * D_FF].astype(BF16)
    o_ref[0, :, D_FF_PAD + D_FF:2 * D_FF_PAD] = gap


def _up_weights(w_up):
    depth, d, n = w_up.shape
    rows = FFN_TM
    return pl.pallas_call(
        _up_weights_kernel,
        grid=(depth, d // rows),
        in_specs=[pl.BlockSpec((1, rows, n), lambda l, i: (l, i, 0))],
        out_specs=pl.BlockSpec((1, rows, 2 * D_FF_PAD), lambda l, i: (l, i, 0)),
        out_shape=jax.ShapeDtypeStruct((depth, d, 2 * D_FF_PAD), BF16),
        compiler_params=pltpu.CompilerParams(
            dimension_semantics=("arbitrary", "arbitrary"), vmem_limit_bytes=VMEM_LIMIT_BYTES),
        name="up_weights",
    )(w_up)


def _down_weights_kernel(w_ref, o_ref):
    rows = w_ref.shape[1]
    row = pl.program_id(1) * rows + lax.broadcasted_iota(jnp.int32, (rows, 1), 0)
    o_ref[0] = jnp.where(row < D_FF, w_ref[0], 0.0).astype(BF16)


def _down_weights(w_down):
    depth, _, d = w_down.shape
    rows = D_FF_PAD // 2
    return pl.pallas_call(
        _down_weights_kernel,
        grid=(depth, D_FF_PAD // rows),
        in_specs=[pl.BlockSpec((1, rows, d), lambda l, i: (l, i, 0))],
        out_specs=pl.BlockSpec((1, rows, d), lambda l, i: (l, i, 0)),
        out_shape=jax.ShapeDtypeStruct((depth, D_FF_PAD, d), BF16),
        compiler_params=pltpu.CompilerParams(
            dimension_semantics=("arbitrary", "arbitrary"), vmem_limit_bytes=VMEM_LIMIT_BYTES),
        name="down_weights",
    )(w_down)


def kernel(x, attn_norm_w, w_in, ssd_conv_w, ssd_conv_b, ssd_dt_bias, ssd_a_log, ssd_d, ssd_norm_w, sc_conv_w,
           cmp_k_pos, cmp_k_w1, cmp_k_w2, cmp_v_pos, cmp_v_w1, cmp_v_w2, w_out, ffn_norm_w, ffn_w_up,
           ffn_conv_w, ffn_conv_b, ffn_w_down, final_norm_w):
    b, s, d = x.shape
    depth = w_in.shape[0]
    assert d == D_MODEL and s % NSA_TK == 0 and s // SEL_BLOCK >= SEL_TOPK and s % INPROJ_TM == 0
    assert INPROJ_TM % NSA_TK == 0 and s >= NSA_WKEYS and NSA_TK // SEL_BLOCK <= SUBLANES
    t = b * s
    ropes = _rope_tables(s)
    ovt = _overlap_t(s)
    nc = s // CMP_STRIDE
    half_tokens = CMP_BLOCK // 2
    x2d = x.reshape(t, d)
    w_in_t = jnp.transpose(w_in, (0, 2, 1))
    wo_all = w_out.astype(BF16)
    wu_all = _up_weights(ffn_w_up)
    wd_all = _down_weights(ffn_w_down)
    for l in range(depth):
        cw = _pad_rows(ssd_conv_w[l], SUBLANES)
        cb = ssd_conv_b[l][None, :]
        dtb = _pad_cols(ssd_dt_bias[l][None, :], LANES)
        aneg = _pad_cols(-jnp.exp(ssd_a_log[l].astype(F32))[None, :] * LOG2E, LANES)
        dsk = jnp.repeat(ssd_d[l].astype(F32), SSD_HEAD_DIM)[None, :]
        scw = _pad_rows(sc_conv_w[l], SUBLANES)
        wc, w2bd = _compress_weights(cmp_k_w1[l], cmp_k_w2[l], cmp_v_w1[l], cmp_v_w2[l])
        pos2 = _pad_rows(jnp.concatenate([cmp_k_pos[l], cmp_v_pos[l]], axis=1).reshape(2, half_tokens * LANES),
                         SUBLANES)
        cw_t = _ff_tiles(_pad_rows(ffn_conv_w[l], SUBLANES))
        cb_t = _ff_tiles(ffn_conv_b[l][None, :])

        z, xbc, sc, q, cpair, ks, kw, vst, vwt, dtg = _inproj(x2d, attn_norm_w[l][None, :], w_in_t, l, ropes, s)
        kc, vct = _compress(cpair.reshape(b, nc, half_tokens * LANES), pos2, wc, w2bd)
        r3 = lambda a: a.reshape(b, s, a.shape[-1])
        ya = _ssd(r3(z), r3(xbc), r3(sc), r3(dtg), cw, cb, dtb, aneg, dsk, ssd_norm_w[l][None, :], scw)
        vt4 = lambda a: a.reshape(b, s // LANES, NSA_HEAD_DIM, LANES)
        yn = _nsa(r3(q), r3(dtg), kc, vct, r3(ks), vt4(vst), r3(kw), vt4(vwt), ovt)
        x2d = _ffn(x2d, ya.reshape(t, -1), yn.reshape(t, -1), wo_all, ffn_norm_w[l][None, :], wu_all, cw_t, cb_t,
                   wd_all, final_norm_w[None, :], l, s, l == depth - 1)
    return x2d.reshape(b, s, d)
```

```python
import functools
import math

import numpy as np
import jax
import jax.numpy as jnp
from jax import lax
from jax.experimental import pallas as pl
from jax.experimental.pallas import tpu as pltpu

F32 = jnp.float32
BF16 = jnp.bfloat16

D_MODEL = 1024
SSD_HEADS = 8
SSD_HEAD_DIM = 64
SSD_INNER = SSD_HEADS * SSD_HEAD_DIM
SSD_GROUPS = 2
SSD_STATE = 128
SSD_CONV = 4
SSD_CHUNK = 256
SSD_XBC = SSD_INNER + 2 * SSD_GROUPS * SSD_STATE
SC_WIDTH = 256
SC_CONV = 3
NSA_HEADS = 4
NSA_HEAD_DIM = 64
NSA_WIDTH = NSA_HEADS * NSA_HEAD_DIM
CMP_BLOCK = 32
CMP_STRIDE = 16
CMP_HIDDEN = 128
SEL_BLOCK = 64
SEL_TOPK = 16
SEL_LOCAL = 2
WINDOW = 512
ROPE_THETA = 10000.0
D_FF = 2752
FFN_CONV = 3
EPS = 1e-6
NEG = -1e30
FORCE = 1e9
REMOVED = -3.0e38
LOG2E = 1.4426950408889634

LANES = 128
SUBLANES = 8
VMEM_LIMIT_BYTES = 56 * 1024 * 1024

D_FF_PAD = 2816
FF_TILE = 256
N_FF_TILES = D_FF_PAD // FF_TILE

COL_Z = 0
COL_XBC = COL_Z + SSD_INNER
COL_SC = COL_XBC + SSD_XBC
COL_Q = COL_SC + 3 * SC_WIDTH
COL_KV = COL_Q + NSA_WIDTH
COL_DTG = COL_KV + 6 * NSA_HEAD_DIM
N_PACK = COL_DTG + LANES
GATE_COL = SSD_HEADS
IN_DT0 = SSD_INNER + SSD_XBC
IN_DT1 = IN_DT0 + SSD_HEADS
IN_G0 = IN_DT1 + 3 * SC_WIDTH + NSA_WIDTH + 6 * NSA_HEAD_DIM
IN_G1 = IN_G0 + 3 * NSA_HEADS

INPROJ_TM = 512
NSA_TQ = 512
NSA_TK = 512
NSA_WKEYS = WINDOW + NSA_TQ
FFN_TM = 512


def _dot(a, b):
    return jnp.dot(a, b, preferred_element_type=F32)


def _dot_nt(a, b):
    return lax.dot_general(a, b, (((1,), (1,)), ((), ())), preferred_element_type=F32)


def _sigmoid(x):
    return 1.0 / (1.0 + jnp.exp(-x))


def _silu(x):
    h = 0.5 * x
    return h + h * jnp.tanh(h)


def _softplus(x):
    return jnp.maximum(x, 0.0) + jnp.log1p(jnp.exp(-jnp.abs(x)))


def _split3(a):
    a1 = a.astype(BF16)
    r1 = a - a1.astype(F32)
    a2 = r1.astype(BF16)
    r2 = r1 - a2.astype(F32)
    return a1, a2, r2.astype(BF16)


def _shift_rows(cur, tail, k):
    if k == 0:
        return cur
    rc = pltpu.roll(cur, k, 0)
    rt = pltpu.roll(tail, k, 0)
    row = lax.broadcasted_iota(jnp.int32, tail.shape, 0)
    first = jnp.where(row < k, rt, rc[0:SUBLANES])
    return jnp.concatenate([first, rc[SUBLANES:]], axis=0)


def _inproj_kernel(x_ref, nw_ref, w_ref, rqc_ref, rqs_ref, rkc_ref, rks_ref,
                   z_ref, xbc_ref, sc_ref, q_ref, cpair_ref, ks_ref, kw_ref, vst_ref, vwt_ref, dtg_ref,
                   wb_ref, cstage_ref):
    tm = x_ref.shape[0]

    @pl.when(pl.program_id(0) == 0)
    def _():
        def cast_rows(dst, src, n):
            for r in range(0, n, LANES):
                m = min(LANES, n - r)
                wb_ref[dst + r:dst + r + m, :] = w_ref[0, src + r:src + r + m, :].astype(BF16)

        cast_rows(0, 0, IN_DT0)
        cast_rows(IN_DT0, IN_DT1, IN_G0 - IN_DT1)
        tail = jnp.concatenate([w_ref[0, IN_DT0:IN_DT1, :], w_ref[0, IN_G0:IN_G1, :],
                                jnp.zeros((LANES - (IN_DT1 - IN_DT0) - (IN_G1 - IN_G0), D_MODEL), F32)], axis=0)
        wb_ref[COL_DTG:N_PACK, :] = tail.astype(BF16)

    x = x_ref[...]
    ms = jnp.mean(x * x, axis=-1, keepdims=True)
    h = (x * lax.rsqrt(ms + EPS) * nw_ref[...]).astype(BF16)

    def proj(a, b):
        return _dot_nt(h, wb_ref[a:b, :])

    z_ref[...] = proj(COL_Z, COL_XBC)
    xbc_ref[...] = proj(COL_XBC, COL_SC)
    sc_ref[...] = proj(COL_SC, COL_Q)
    dtg_ref[...] = proj(COL_DTG, N_PACK)

    lane = lax.broadcasted_iota(jnp.int32, (tm, LANES), 1)
    first_half = (lane % NSA_HEAD_DIM) < (NSA_HEAD_DIM // 2)

    def rope(v, c, s):
        partner = jnp.where(first_half, pltpu.roll(v, LANES - 32, 1), pltpu.roll(v, 32, 1))
        return v * c + partner * s

    rqc, rqs, rkc, rks = rqc_ref[...], rqs_ref[...], rkc_ref[...], rks_ref[...]
    q = proj(COL_Q, COL_KV)
    q_ref[:, 0:LANES] = rope(q[:, 0:LANES], rqc, rqs)
    q_ref[:, LANES:2 * LANES] = rope(q[:, LANES:2 * LANES], rqc, rqs)

    kv = proj(COL_KV, COL_DTG)
    cstage_ref[...] = rope(kv[:, 0:LANES], rkc, rks)
    group = CMP_BLOCK // 2
    for r in range(group):
        cpair_ref[:, r * LANES:(r + 1) * LANES] = cstage_ref[pl.ds(r, tm // group, stride=group), :]
    row = lax.broadcasted_iota(jnp.int32, (tm, LANES), 0)
    block_in_tile = (row % NSA_TK) // SEL_BLOCK
    onehot = jnp.where(lane - NSA_HEAD_DIM == block_in_tile, 1.0, 0.0)
    k_lanes = lane < NSA_HEAD_DIM
    for pair, k_ref, vt_ref, fill in ((1, ks_ref, vst_ref, onehot), (2, kw_ref, vwt_ref, 0.0)):
        p = rope(kv[:, pair * LANES:(pair + 1) * LANES], rkc, rks)
        k_ref[...] = jnp.where(k_lanes, p, fill).astype(BF16)
        pt = p.T
        for j in range(tm // LANES):
            vt_ref[j] = pt[NSA_HEAD_DIM:LANES, j * LANES:(j + 1) * LANES].astype(BF16)


def _inproj(x2d, nw, w_t, layer, ropes, seq):
    t = x2d.shape[0]
    tm = INPROJ_TM
    nt = t // tm
    pos_blocks = seq // tm
    row = lambda i: (i, 0)
    pos = lambda i: (i % pos_blocks, 0)
    const = lambda i: (0, 0)
    out_shapes = (
        jax.ShapeDtypeStruct((t, SSD_INNER), F32),
        jax.ShapeDtypeStruct((t, SSD_XBC), F32),
        jax.ShapeDtypeStruct((t, 3 * SC_WIDTH), F32),
        jax.ShapeDtypeStruct((t, NSA_WIDTH), F32),
        jax.ShapeDtypeStruct((t // (CMP_BLOCK // 2), (CMP_BLOCK // 2) * LANES), F32),
        jax.ShapeDtypeStruct((t, LANES), BF16),
        jax.ShapeDtypeStruct((t, LANES), BF16),
        jax.ShapeDtypeStruct((t // LANES, NSA_HEAD_DIM, LANES), BF16),
        jax.ShapeDtypeStruct((t // LANES, NSA_HEAD_DIM, LANES), BF16),
        jax.ShapeDtypeStruct((t, LANES), F32),
    )
    vt_spec = pl.BlockSpec((tm // LANES, NSA_HEAD_DIM, LANES), lambda i: (i, 0, 0))
    out_specs = (
        pl.BlockSpec((tm, SSD_INNER), row),
        pl.BlockSpec((tm, SSD_XBC), row),
        pl.BlockSpec((tm, 3 * SC_WIDTH), row),
        pl.BlockSpec((tm, NSA_WIDTH), row),
        pl.BlockSpec((tm // (CMP_BLOCK // 2), (CMP_BLOCK // 2) * LANES), row),
        pl.BlockSpec((tm, LANES), row),
        pl.BlockSpec((tm, LANES), row),
        vt_spec,
        vt_spec,
        pl.BlockSpec((tm, LANES), row),
    )
    in_specs = [
        pl.BlockSpec((tm, D_MODEL), row),
        pl.BlockSpec((1, D_MODEL), const),
        pl.BlockSpec((1, IN_G1, D_MODEL), lambda i: (layer, 0, 0), pipeline_mode=pl.Buffered(1)),
        pl.BlockSpec((tm, LANES), pos),
        pl.BlockSpec((tm, LANES), pos),
        pl.BlockSpec((tm, LANES), pos),
        pl.BlockSpec((tm, LANES), pos),
    ]
    return pl.pallas_call(
        _inproj_kernel,
        grid=(nt,),
        in_specs=in_specs,
        out_specs=out_specs,
        out_shape=out_shapes,
        scratch_shapes=[pltpu.VMEM((N_PACK, D_MODEL), BF16), pltpu.VMEM((tm, LANES), F32)],
        compiler_params=pltpu.CompilerParams(
            dimension_semantics=("arbitrary",), vmem_limit_bytes=VMEM_LIMIT_BYTES),
        name="inproj",
    )(x2d, nw, w_t, *ropes)


def _compress_kernel(x2_ref, pos_ref, wc_ref, w2_ref, kc_ref, vct_ref):
    nc = x2_ref.shape[1]
    x2 = x2_ref[0].astype(BF16)
    wc = wc_ref[...]
    y = _dot(x2, wc)
    r = _dot(pos_ref[...].astype(BF16), wc)
    h = CMP_HIDDEN

    def pre(base):
        bias = r[0:1, base:base + h] + r[1:2, base + h:base + 2 * h]
        return y[:, base:base + h] + pltpu.roll(y[:, base + h:base + 2 * h], nc - 1, 0) + bias

    hid = jnp.concatenate([jax.nn.gelu(pre(0)), jax.nn.gelu(pre(2 * h))], axis=1).astype(BF16)
    o = _dot(hid, w2_ref[...])
    kc_ref[0] = o.astype(BF16)
    vct_ref[0] = o.T[NSA_HEAD_DIM:LANES, :].astype(BF16)


def _compress(x2p, pos2, wc, w2bd):
    b, nc, width = x2p.shape
    return pl.pallas_call(
        _compress_kernel,
        grid=(b,),
        in_specs=[
            pl.BlockSpec((1, nc, width), lambda i: (i, 0, 0)),
            pl.BlockSpec(pos2.shape, lambda i: (0, 0)),
            pl.BlockSpec(wc.shape, lambda i: (0, 0)),
            pl.BlockSpec(w2bd.shape, lambda i: (0, 0)),
        ],
        out_specs=(
            pl.BlockSpec((1, nc, LANES), lambda i: (i, 0, 0)),
            pl.BlockSpec((1, NSA_HEAD_DIM, nc), lambda i: (i, 0, 0)),
        ),
        out_shape=(
            jax.ShapeDtypeStruct((b, nc, LANES), BF16),
            jax.ShapeDtypeStruct((b, NSA_HEAD_DIM, nc), BF16),
        ),
        compiler_params=pltpu.CompilerParams(
            dimension_semantics=("arbitrary",), vmem_limit_bytes=VMEM_LIMIT_BYTES),
        name="compress",
    )(x2p, pos2, wc, w2bd)


def _ssd_kernel(z_ref, xbc_ref, sc_ref, dtg_ref, cw_ref, cb_ref, dtb_ref, aneg_ref, dsk_ref, nw_ref, scw_ref,
                y_ref, h_ref, xtail_ref, stail_ref):
    L = SSD_CHUNK
    half = SSD_HEAD_DIM

    @pl.when(pl.program_id(1) == 0)
    def _():
        h_ref[...] = jnp.zeros_like(h_ref)
        xtail_ref[...] = jnp.zeros_like(xtail_ref)
        stail_ref[...] = jnp.zeros_like(stail_ref)

    xraw = xbc_ref[0]
    xtail = xtail_ref[...]
    conv = cb_ref[...] + cw_ref[SSD_CONV - 1:SSD_CONV, :] * xraw
    for j in range(SSD_CONV - 1):
        conv = conv + cw_ref[j:j + 1, :] * _shift_rows(xraw, xtail, SSD_CONV - 1 - j)
    xtail_ref[...] = xraw[L - SUBLANES:L]
    xc = _silu(conv)

    dt = _softplus(dtg_ref[0] + dtb_ref[...])
    a = dt * aneg_ref[...]
    row = lax.broadcasted_iota(jnp.int32, (L, L), 0)
    col = lax.broadcasted_iota(jnp.int32, (L, L), 1)
    causal = row >= col
    tri = jnp.where(causal, 1.0, 0.0).astype(BF16)
    a1, a2, a3 = _split3(a)
    acs = _dot(tri, a1) + _dot(tri, a2) + _dot(tri, a3)
    acs_t = acs.T
    exp_acs = jnp.exp2(acs)
    last = acs[L - 1:L, :]
    dte = jnp.exp2(last - acs)
    chunk_decay = jnp.exp2(last)

    lo = lax.broadcasted_iota(jnp.int32, (L, LANES), 1) < half
    lo_row = lax.broadcasted_iota(jnp.int32, (1, LANES), 1) < half

    def per_lane(m, ha, hb, mask):
        return jnp.where(mask, m[:, ha:ha + 1], m[:, hb:hb + 1])

    ys = []
    for g in range(SSD_GROUPS):
        b_g = xc[:, SSD_INNER + g * SSD_STATE:SSD_INNER + (g + 1) * SSD_STATE]
        c_g = xc[:, SSD_INNER + (SSD_GROUPS + g) * SSD_STATE:SSD_INNER + (SSD_GROUPS + g + 1) * SSD_STATE]
        b_bf = b_g.astype(BF16)
        c_bf = c_g.astype(BF16)
        cb = _dot_nt(c_bf, b_bf)
        bt_bf = b_g.T.astype(BF16)
        for pp in range(SSD_HEADS // SSD_GROUPS // 2):
            p = g * (SSD_HEADS // SSD_GROUPS // 2) + pp
            ha, hb = 2 * p, 2 * p + 1
            x_pair = xc[:, p * LANES:(p + 1) * LANES]
            xdt = x_pair * per_lane(dt, ha, hb, lo)
            xdt_bf = xdt.astype(BF16)
            yd = []
            for hd in (ha, hb):
                seg = acs[:, hd:hd + 1] - acs_t[hd:hd + 1, :]
                decay = jnp.exp2(jnp.where(causal, seg, NEG))
                yd.append(_dot((decay * cb).astype(BF16), xdt_bf))
            y_diag = jnp.where(lo, yd[0], yd[1])
            st = _dot(bt_bf, (xdt * per_lane(dte, ha, hb, lo)).astype(BF16))
            h_prev = h_ref[p]
            y_off = _dot(c_bf, h_prev.astype(BF16)) * per_lane(exp_acs, ha, hb, lo)
            h_ref[p] = h_prev * per_lane(chunk_decay, ha, hb, lo_row) + st
            ys.append(y_diag + y_off + x_pair * dsk_ref[:, p * LANES:(p + 1) * LANES])
    y = jnp.concatenate(ys, axis=1)
    z = z_ref[0]
    y = y * _silu(z)
    ms = jnp.mean(y * y, axis=-1, keepdims=True)
    y_ref[0, :, 0:SSD_INNER] = y * lax.rsqrt(ms + EPS) * nw_ref[...]

    sc = sc_ref[0]
    u = sc[:, SC_WIDTH:2 * SC_WIDTH] * sc[:, 2 * SC_WIDTH:3 * SC_WIDTH]
    stail = stail_ref[...]
    cv = scw_ref[SC_CONV - 1:SC_CONV, :] * u
    for j in range(SC_CONV - 1):
        cv = cv + scw_ref[j:j + 1, :] * _shift_rows(u, stail, SC_CONV - 1 - j)
    stail_ref[...] = u[L - SUBLANES:L]
    y_ref[0, :, SSD_INNER:SSD_INNER + SC_WIDTH] = sc[:, 0:SC_WIDTH] * cv


def _ssd(z, xbc, sc, dtg, cw, cb, dtb, aneg, dsk, nw, scw):
    b, s, _ = z.shape
    L = SSD_CHUNK
    blk = lambda w: pl.BlockSpec((1, L, w), lambda i, c: (i, c, 0))
    par = lambda a: pl.BlockSpec(a.shape, lambda i, c: (0, 0))
    return pl.pallas_call(
        _ssd_kernel,
        grid=(b, s // L),
        in_specs=[blk(SSD_INNER), blk(SSD_XBC), blk(3 * SC_WIDTH), blk(LANES),
                  par(cw), par(cb), par(dtb), par(aneg), par(dsk), par(nw), par(scw)],
        out_specs=blk(SSD_INNER + SC_WIDTH),
        out_shape=jax.ShapeDtypeStruct((b, s, SSD_INNER + SC_WIDTH), F32),
        scratch_shapes=[
            pltpu.VMEM((SSD_HEADS // 2, SSD_STATE, LANES), F32),
            pltpu.VMEM((SUBLANES, SSD_XBC), F32),
            pltpu.VMEM((SUBLANES, SC_WIDTH), F32),
        ],
        compiler_params=pltpu.CompilerParams(
            dimension_semantics=("arbitrary", "arbitrary"), vmem_limit_bytes=VMEM_LIMIT_BYTES),
        name="ssd_sc",
    )(z, xbc, sc, dtg, cw, cb, dtb, aneg, dsk, nw, scw)


def _nsa_kernel(q_ref, dtg_ref, kc_ref, vct_ref, ks_ref, vst_ref, kw_ref, vwt_ref, ovt_ref,
                o_ref, sel_ref, qa_ref, acc_ref, m_ref, accw_ref, mw_ref, s_ref, *, seq):
    tq, tk = NSA_TQ, NSA_TK
    hd = NSA_HEAD_DIM
    nh = NSA_HEADS
    nc = seq // CMP_STRIDE
    nb = seq // SEL_BLOCK
    q0 = pl.program_id(1) * tq
    tpos = q0 + lax.broadcasted_iota(jnp.int32, (1, tq), 1)

    def heads_on_lanes(a):
        return jnp.concatenate([a] * nh, axis=1)

    qt = q_ref[0].T
    qa_ref[0:hd, :] = jnp.concatenate([qt[h * hd:(h + 1) * hd, :] for h in range(nh)], axis=1).astype(BF16)
    qa_ref[hd:LANES, :] = jnp.zeros((LANES - hd, nh * tq), BF16)
    gates = _sigmoid(dtg_ref[0]).T

    n_io = lax.broadcasted_iota(jnp.int32, (nc, tq), 0)
    cbias = jnp.where((n_io * CMP_STRIDE + (CMP_BLOCK - 1)) <= tpos, 0.0, NEG)
    s = _dot(kc_ref[0], qa_ref[...]) + heads_on_lanes(cbias)
    p = jnp.exp2(s - jnp.max(s, axis=0, keepdims=True))
    has_key = heads_on_lanes(jnp.where(tpos >= CMP_BLOCK - 1, 1.0, 0.0))
    p = p * (has_key / jnp.sum(p, axis=0, keepdims=True))
    o_cmp = _dot(vct_ref[0], p.astype(BF16))
    psum = p[:, 0:tq]
    for h in range(1, nh):
        psum = psum + p[:, h * tq:(h + 1) * tq]

    def load_vt(ref, key0, width):
        first = key0 // LANES
        return jnp.concatenate([ref[0, first + i] for i in range(width // LANES)], axis=1)

    ovt = ovt_ref[...]
    p1, p2, p3 = _split3(psum)
    imp = _dot(ovt, p1) + _dot(ovt, p2) + _dot(ovt, p3)
    j_io = lax.broadcasted_iota(jnp.int32, (nb, tq), 0)
    cur = jnp.right_shift(tpos, int(math.log2(SEL_BLOCK)))
    valid = j_io <= cur
    forced = (j_io == 0) | (valid & (j_io > cur - SEL_LOCAL))
    v = jnp.where(forced, REMOVED, jnp.where(valid, imp, NEG))
    j_f = j_io.astype(F32)

    def extract(_, carry):
        v, sel = carry
        m = jnp.max(v, axis=0, keepdims=True)
        first = jnp.min(jnp.where(v == m, j_f, float(nb)), axis=0, keepdims=True)
        hit = j_f == first
        return jnp.where(hit, REMOVED, v), jnp.where(hit, 1.0, sel)

    _, sel = lax.fori_loop(0, SEL_TOPK - 1 - SEL_LOCAL, extract, (v, jnp.where(forced, 1.0, 0.0)), unroll=True)
    sel_ref[...] = sel

    wt = NSA_WKEYS // 2
    w0 = pl.multiple_of(jnp.maximum(q0 - WINDOW, 0), tq)
    blocks_per_tile = tk // SEL_BLOCK
    bias_rows = 2 * SUBLANES
    for ref in (acc_ref, accw_ref):
        ref[...] = jnp.zeros_like(ref)
    for ref in (m_ref, mw_ref):
        ref[...] = jnp.full(ref.shape, NEG, F32)

    def scores_of(buf, rows, k_tile):
        def one_head(h):
            lanes = slice(h * tq, (h + 1) * tq)
            s_ref[buf, 0:rows, lanes] = _dot(k_tile, qa_ref[:, lanes])
        return one_head

    def update_of(buf, rows, vt, bias, m_r, acc_r):
        ones = jnp.where(lax.broadcasted_iota(jnp.int32, (bias_rows, rows), 0) == 0, 1.0, 0.0).astype(BF16)
        vt_aug = jnp.concatenate([vt, ones], axis=0)

        def one_head(h):
            lanes = slice(h * tq, (h + 1) * tq)
            s = s_ref[buf, 0:rows, lanes]
            if bias is not None:
                s = s + bias
            m_old = m_r[:, lanes]
            m_new = jnp.maximum(m_old, jnp.max(s, axis=0, keepdims=True))
            p = jnp.exp2(s - m_new).astype(BF16)
            acc_r[:, lanes] = jnp.exp2(m_old - m_new) * acc_r[:, lanes] + _dot(vt_aug, p)
            m_r[:, lanes] = m_new
        return one_head

    def stage(score_fn, update_fn):
        for h in range(nh):
            if score_fn is not None:
                score_fn(h)
            if update_fn is not None:
                update_fn(h)

    def win_scores(buf, i):
        k0 = pl.multiple_of(w0 + i * wt, LANES)
        return scores_of(buf, wt, kw_ref[0, pl.ds(k0, wt), :])

    def win_update(buf, i):
        k0 = pl.multiple_of(w0 + i * wt, LANES)
        d = (k0 - q0) + (lax.broadcasted_iota(jnp.int32, (wt, tq), 0)
                         - lax.broadcasted_iota(jnp.int32, (wt, tq), 1))
        bias = jnp.where(d <= 0, jnp.where(d > -WINDOW, 0.0, NEG), NEG)
        return update_of(buf, wt, load_vt(vwt_ref, k0, wt), bias, mw_ref, accw_ref)

    def sel_scores(buf, kt):
        k0 = pl.multiple_of(kt * tk, tk)
        chunk = sel_ref[pl.ds(pl.multiple_of(kt * blocks_per_tile, blocks_per_tile), blocks_per_tile), :]
        bias = jnp.concatenate([(1.0 - chunk) * NEG, jnp.zeros((bias_rows - blocks_per_tile, tq), F32)], axis=0)
        qa_ref[hd:hd + bias_rows, :] = heads_on_lanes(bias).astype(BF16)
        return scores_of(buf, tk, ks_ref[0, pl.ds(k0, tk), :])

    def sel_update(buf, kt, causal):
        k0 = pl.multiple_of(kt * tk, tk)
        bias = None
        if causal:
            bias = jnp.where(k0 + lax.broadcasted_iota(jnp.int32, (tk, tq), 0) <= tpos, 0.0, NEG)
        return update_of(buf, tk, load_vt(vst_ref, k0, tk), bias, m_ref, acc_ref)

    stage(win_scores(0, 0), None)
    stage(win_scores(1, 1), win_update(0, 0))
    stage(sel_scores(0, 0), win_update(1, 1))

    n_last = (q0 + tq + tk - 1) // tk - 1

    def trip(first, tiles):
        for i in range(tiles):
            stage(sel_scores((i + 1) % 2, first + i + 1), sel_update(i % 2, first + i, False))

    def quad(j, _):
        trip(4 * j, 4)
        return 0

    lax.fori_loop(0, n_last // 4, quad, 0)
    done4 = (n_last // 4) * 4

    def pair(j, _):
        trip(done4 + 2 * j, 2)
        return 0

    lax.fori_loop(0, (n_last - done4) // 2, pair, 0)

    @pl.when(n_last % 2 == 0)
    def _():
        stage(None, sel_update(0, n_last, True))

    @pl.when(n_last % 2 == 1)
    def _():
        stage(sel_scores(1, n_last), sel_update(0, n_last - 1, False))
        stage(None, sel_update(1, n_last, True))

    o_sel = acc_ref[0:hd, :] / acc_ref[hd:hd + 1, :]
    o_win = accw_ref[0:hd, :] / accw_ref[hd:hd + 1, :]

    def gate_row(branch):
        return jnp.concatenate([gates[GATE_COL + 3 * h + branch:GATE_COL + 3 * h + branch + 1, :]
                                for h in range(nh)], axis=1)

    out = o_cmp * gate_row(0) + o_sel * gate_row(1) + o_win * gate_row(2)
    o_ref[0] = jnp.concatenate([out[:, h * tq:(h + 1) * tq] for h in range(nh)], axis=0).T


def _nsa(q, dtg, kc, vct, ks, vst, kw, vwt, ovt):
    b, s, _ = q.shape
    tq = NSA_TQ
    nc = s // CMP_STRIDE
    per_b3 = lambda shape: pl.BlockSpec((1,) + shape, lambda i, j: (i, 0, 0))
    per_b4 = lambda shape: pl.BlockSpec((1,) + shape, lambda i, j: (i, 0, 0, 0))
    qblk = lambda w: pl.BlockSpec((1, tq, w), lambda i, j: (i, j, 0))
    return pl.pallas_call(
        functools.partial(_nsa_kernel, seq=s),
        grid=(b, s // tq),
        in_specs=[
            qblk(NSA_WIDTH), qblk(LANES),
            per_b3((nc, LANES)), per_b3((NSA_HEAD_DIM, nc)),
            per_b3((s, LANES)), per_b4((s // LANES, NSA_HEAD_DIM, LANES)),
            per_b3((s, LANES)), per_b4((s // LANES, NSA_HEAD_DIM, LANES)),
            pl.BlockSpec(ovt.shape, lambda i, j: (0, 0)),
        ],
        out_specs=qblk(NSA_WIDTH),
        out_shape=jax.ShapeDtypeStruct((b, s, NSA_WIDTH), F32),
        scratch_shapes=[
            pltpu.VMEM((s // SEL_BLOCK, tq), F32),
            pltpu.VMEM((LANES, NSA_HEADS * tq), BF16),
            pltpu.VMEM((NSA_HEAD_DIM + 2 * SUBLANES, NSA_HEADS * tq), F32),
            pltpu.VMEM((1, NSA_HEADS * tq), F32),
            pltpu.VMEM((NSA_HEAD_DIM + 2 * SUBLANES, NSA_HEADS * tq), F32),
            pltpu.VMEM((1, NSA_HEADS * tq), F32),
            pltpu.VMEM((2, NSA_TK, NSA_HEADS * tq), F32),
        ],
        compiler_params=pltpu.CompilerParams(
            dimension_semantics=("arbitrary", "arbitrary"), vmem_limit_bytes=VMEM_LIMIT_BYTES),
        name="nsa",
    )(q, dtg, kc, vct, ks, vst, kw, vwt, ovt)


def _ffn_kernel(x_ref, ya_ref, yn_ref, xn_ref, yan_ref, ynn_ref, wo_ref, nw_ref, wu_ref, cw_ref, cb_ref, wd_ref,
                fw_ref, o_ref, tail_ref, act_ref, x1_ref, h_ref, *, tiles_per_seq, final_norm):
    tm = x_ref.shape[0]
    na = ya_ref.shape[1]

    def mixed_residual(xr, yar, ynr):
        x1 = (xr[...] + _dot(yar[...].astype(BF16), wo_ref[0:na, :])
              + _dot(ynr[...].astype(BF16), wo_ref[na:, :]))
        ms = jnp.mean(x1 * x1, axis=-1, keepdims=True)
        return x1, (x1 * lax.rsqrt(ms + EPS) * nw_ref[...]).astype(BF16)

    @pl.when(pl.program_id(0) % tiles_per_seq == 0)
    def _():
        tail_ref[...] = jnp.zeros_like(tail_ref)

    @pl.when(pl.program_id(0) == 0)
    def _():
        x1_0, h_0 = mixed_residual(x_ref, ya_ref, yn_ref)
        x1_ref[...] = x1_0
        h_ref[...] = h_0

    h = h_ref[...]

    def conv_cols(c0):
        cols = slice(c0, c0 + FF_TILE)
        u = _dot(h, wu_ref[:, cols])
        tail = tail_ref[:, cols]
        cv = cb_ref[:, cols] + cw_ref[FFN_CONV - 1:FFN_CONV, cols] * u
        for t in range(FFN_CONV - 1):
            cv = cv + cw_ref[t:t + 1, cols] * _shift_rows(u, tail, FFN_CONV - 1 - t)
        tail_ref[:, cols] = u[tm - SUBLANES:tm]
        return cv

    for j in range(N_FF_TILES):
        gate = conv_cols(j * FF_TILE)
        val = conv_cols(D_FF_PAD + j * FF_TILE)
        act_ref[:, j * FF_TILE:(j + 1) * FF_TILE] = (_silu(gate) * val).astype(BF16)

    x1_next, h_next = mixed_residual(xn_ref, yan_ref, ynn_ref)
    x2 = x1_ref[...] + _dot(act_ref[...], wd_ref[...])
    if final_norm:
        ms2 = jnp.mean(x2 * x2, axis=-1, keepdims=True)
        x2 = x2 * lax.rsqrt(ms2 + EPS) * fw_ref[...]
    o_ref[...] = x2
    x1_ref[...] = x1_next
    h_ref[...] = h_next


def _ffn(x2d, ya, yn, wo, nw, wu_t, cw_t, cb_t, wd, fw, layer, seq, final_norm):
    t = x2d.shape[0]
    tm = FFN_TM
    last = t // tm - 1
    row = lambda i: (i, 0)
    nxt = lambda i: (jnp.minimum(i + 1, last), 0)
    c2 = lambda a: (pl.BlockSpec(a.shape, lambda i: (0, 0)) if a.ndim == 2
                    else pl.BlockSpec((None,) + a.shape[1:], lambda i: (layer, 0, 0)))
    return pl.pallas_call(
        functools.partial(_ffn_kernel, tiles_per_seq=seq // tm, final_norm=final_norm),
        grid=(t // tm,),
        in_specs=[
            pl.BlockSpec((tm, D_MODEL), row),
            pl.BlockSpec((tm, ya.shape[1]), row),
            pl.BlockSpec((tm, yn.shape[1]), row),
            pl.BlockSpec((tm, D_MODEL), nxt),
            pl.BlockSpec((tm, ya.shape[1]), nxt),
            pl.BlockSpec((tm, yn.shape[1]), nxt),
            c2(wo), c2(nw), c2(wu_t), c2(cw_t), c2(cb_t), c2(wd), c2(fw),
        ],
        out_specs=pl.BlockSpec((tm, D_MODEL), row),
        out_shape=jax.ShapeDtypeStruct((t, D_MODEL), F32),
        scratch_shapes=[
            pltpu.VMEM((SUBLANES, 2 * D_FF_PAD), F32),
            pltpu.VMEM((tm, D_FF_PAD), BF16),
            pltpu.VMEM((tm, D_MODEL), F32),
            pltpu.VMEM((tm, D_MODEL), BF16),
        ],
        compiler_params=pltpu.CompilerParams(
            dimension_semantics=("arbitrary",), vmem_limit_bytes=VMEM_LIMIT_BYTES),
        name="outproj_ffn",
    )(x2d, ya, yn, x2d, ya, yn, wo, nw, wu_t, cw_t, cb_t, wd, fw)


def _rope_tables(seq):
    half = NSA_HEAD_DIM // 2
    f32 = np.float32
    inv = (f32(1.0) / (f32(ROPE_THETA) ** (np.arange(half, dtype=f32) / f32(half)))).astype(f32)
    ang = (np.arange(seq, dtype=f32)[:, None] * inv[None, :]).astype(f32).astype(np.float64)
    cos, sin = np.cos(ang), np.sin(ang)
    scale = NSA_HEAD_DIM ** -0.5 * LOG2E
    one = np.ones_like(cos)
    zero = np.zeros_like(cos)
    rqc = np.concatenate([cos, cos, cos, cos], axis=1) * scale
    rqs = np.concatenate([-sin, sin, -sin, sin], axis=1) * scale
    rkc = np.concatenate([cos, cos, one, one], axis=1)
    rks = np.concatenate([-sin, sin, zero, zero], axis=1)
    return tuple(jnp.asarray(a.astype(f32)) for a in (rqc, rqs, rkc, rks))


def _compress_weights(kw1, kw2, vw1, vw2):
    half_tokens = CMP_BLOCK // 2
    hd, hid = NSA_HEAD_DIM, CMP_HIDDEN
    kw1r = kw1.reshape(2, half_tokens, hd, hid)
    vw1r = vw1.reshape(2, half_tokens, hd, hid)
    zeros = jnp.zeros((half_tokens, hd, hid), kw1.dtype)
    cols = []
    for w1r, is_k in ((kw1r, True), (vw1r, False)):
        for part in range(2):
            blk = w1r[part]
            rows = jnp.concatenate([blk, zeros] if is_k else [zeros, blk], axis=1)
            cols.append(rows.reshape(half_tokens * LANES, hid))
    wc = jnp.concatenate(cols, axis=1).astype(BF16)
    zk = jnp.zeros((hid, hd), kw2.dtype)
    w2bd = jnp.concatenate([jnp.concatenate([kw2, zk], axis=1),
                            jnp.concatenate([zk, vw2], axis=1)], axis=0).astype(BF16)
    return wc, w2bd


def _overlap_t(seq):
    nc = seq // CMP_STRIDE
    nb = seq // SEL_BLOCK
    cmp_start = np.arange(nc) * CMP_STRIDE
    slc_start = np.arange(nb) * SEL_BLOCK
    ov = ((cmp_start[None, :] < slc_start[:, None] + SEL_BLOCK)
          & (cmp_start[None, :] + CMP_BLOCK > slc_start[:, None])
          & (np.arange(nc)[None, :] < nc - 1))
    return jnp.asarray(ov.astype(np.float32), dtype=BF16)


def _pad_rows(a, rows):
    return jnp.concatenate([a, jnp.zeros((rows - a.shape[0],) + a.shape[1:], a.dtype)], axis=0)


def _pad_cols(a, cols):
    return jnp.concatenate([a, jnp.zeros(a.shape[:-1] + (cols - a.shape[-1],), a.dtype)], axis=-1)


def _ff_tiles(a):
    return jnp.concatenate([_pad_cols(a[:, :D_FF], D_FF_PAD), _pad_cols(a[:, D_FF:], D_FF_PAD)], axis=1)


def _up_weights_kernel(w_ref, o_ref):
    rows = w_ref.shape[1]
    w = w_ref[0]
    gap = jnp.zeros((rows, D_FF_PAD - D_FF), BF16)
    o_ref[0, :, 0:D_FF] = w[:, 0:D_FF].astype(BF16)
    o_ref[0, :, D_FF:D_FF_PAD] = gap
    o_ref[0, :, D_FF_PAD:D_FF_PAD + D_FF] = w[:, D_FF:2 * D_FF].astype(BF16)
    o_ref[0, :, D_FF_PAD + D_FF:2 * D_FF_PAD] = gap


def _up_weights(w_up):
    depth, d, n = w_up.shape
    rows = 256
    return pl.pallas_call(
        _up_weights_kernel,
        grid=(depth, d // rows),
        in_specs=[pl.BlockSpec((1, rows, n), lambda l, i: (l, i, 0))],
        out_specs=pl.BlockSpec((1, rows, 2 * D_FF_PAD), lambda l, i: (l, i, 0)),
        out_shape=jax.ShapeDtypeStruct((depth, d, 2 * D_FF_PAD), BF16),
        compiler_params=pltpu.CompilerParams(
            dimension_semantics=("arbitrary", "arbitrary"), vmem_limit_bytes=VMEM_LIMIT_BYTES),
        name="up_weights",
    )(w_up)


def _cast_weights_kernel(w_ref, o_ref):
    o_ref[...] = w_ref[...].astype(BF16)


def _cast_weights(w):
    depth, r, c = w.shape
    return pl.pallas_call(
        _cast_weights_kernel,
        grid=(depth,),
        in_specs=[pl.BlockSpec((1, r, c), lambda l: (l, 0, 0))],
        out_specs=pl.BlockSpec((1, r, c), lambda l: (l, 0, 0)),
        out_shape=jax.ShapeDtypeStruct((depth, r, c), BF16),
        compiler_params=pltpu.CompilerParams(
            dimension_semantics=("arbitrary",), vmem_limit_bytes=VMEM_LIMIT_BYTES),
        name="cast_weights",
    )(w)


def _down_weights_kernel(w_ref, o_ref):
    rows = w_ref.shape[1]
    row = pl.program_id(1) * rows + lax.broadcasted_iota(jnp.int32, (rows, 1), 0)
    o_ref[0] = jnp.where(row < D_FF, w_ref[0], 0.0).astype(BF16)


def _down_weights(w_down):
    depth, _, d = w_down.shape
    rows = D_FF_PAD // 2
    return pl.pallas_call(
        _down_weights_kernel,
        grid=(depth, D_FF_PAD // rows),
        in_specs=[pl.BlockSpec((1, rows, d), lambda l, i: (l, i, 0))],
        out_specs=pl.BlockSpec((1, rows, d), lambda l, i: (l, i, 0)),
        out_shape=jax.ShapeDtypeStruct((depth, D_FF_PAD, d), BF16),
        compiler_params=pltpu.CompilerParams(
            dimension_semantics=("arbitrary", "arbitrary"), vmem_limit_bytes=VMEM_LIMIT_BYTES),
        name="down_weights",
    )(w_down)


def kernel(x, attn_norm_w, w_in, ssd_conv_w, ssd_conv_b, ssd_dt_bias, ssd_a_log, ssd_d, ssd_norm_w, sc_conv_w,
           cmp_k_pos, cmp_k_w1, cmp_k_w2, cmp_v_pos, cmp_v_w1, cmp_v_w2, w_out, ffn_norm_w, ffn_w_up,
           ffn_conv_w, ffn_conv_b, ffn_w_down, final_norm_w):
    b, s, d = x.shape
    depth = w_in.shape[0]
    assert d == D_MODEL and s % NSA_TK == 0 and s // SEL_BLOCK >= SEL_TOPK and s % INPROJ_TM == 0
    assert INPROJ_TM % NSA_TK == 0 and s >= NSA_WKEYS and NSA_TK // SEL_BLOCK <= SUBLANES
    t = b * s
    ropes = _rope_tables(s)
    ovt = _overlap_t(s)
    nc = s // CMP_STRIDE
    half_tokens = CMP_BLOCK // 2
    x2d = x.reshape(t, d)
    w_in_t = jnp.transpose(w_in, (0, 2, 1))
    wo_all = _cast_weights(w_out.astype(F32))
    wu_all = _up_weights(ffn_w_up)
    wd_all = _down_weights(ffn_w_down)
    for l in range(depth):
        cw = _pad_rows(ssd_conv_w[l], SUBLANES)
        cb = ssd_conv_b[l][None, :]
        dtb = _pad_cols(ssd_dt_bias[l][None, :], LANES)
        aneg = _pad_cols(-jnp.exp(ssd_a_log[l].astype(F32))[None, :] * LOG2E, LANES)
        dsk = jnp.repeat(ssd_d[l].astype(F32), SSD_HEAD_DIM)[None, :]
        scw = _pad_rows(sc_conv_w[l], SUBLANES)
        wc, w2bd = _compress_weights(cmp_k_w1[l], cmp_k_w2[l], cmp_v_w1[l], cmp_v_w2[l])
        pos2 = _pad_rows(jnp.concatenate([cmp_k_pos[l], cmp_v_pos[l]], axis=1).reshape(2, half_tokens * LANES),
                         SUBLANES)
        cw_t = _ff_tiles(_pad_rows(ffn_conv_w[l], SUBLANES))
        cb_t = _ff_tiles(ffn_conv_b[l][None, :])

        z, xbc, sc, q, cpair, ks, kw, vst, vwt, dtg = _inproj(x2d, attn_norm_w[l][None, :], w_in_t, l, ropes, s)
        kc, vct = _compress(cpair.reshape(b, nc, half_tokens * LANES), pos2, wc, w2bd)
        r3 = lambda a: a.reshape(b, s, a.shape[-1])
        ya = _ssd(r3(z), r3(xbc), r3(sc), r3(dtg), cw, cb, dtb, aneg, dsk, ssd_norm_w[l][None, :], scw)
        vt4 = lambda a: a.reshape(b, s // LANES, NSA_HEAD_DIM, LANES)
        yn = _nsa(r3(q), r3(dtg), kc, vct, r3(ks), vt4(vst), r3(kw), vt4(vwt), ovt)
        x2d = _ffn(x2d, ya.reshape(t, -1), yn.reshape(t, -1), wo_all, ffn_norm_w[l][None, :], wu_all, cw_t, cb_t,
                   wd_all, final_norm_w[None, :], l, s, l == depth - 1)
    return x2d.reshape(b, s, d)
```
